```python
import math
import jax
import jax.numpy as jnp
from jax import lax
import numpy as np

D_MODEL = 1024
BATCH = 32
SEQ = 2048
DEPTH = 2

GRID_W = 64
CTX_LEN = 256
F32 = jnp.float32
NORM_EPS = 1e-6
HG_HEADS = 4
HG_HEAD_DIM = 128
HG_WIDTH = HG_HEADS * HG_HEAD_DIM
HG_CHUNK = 32
HG_EXP_CLIP = 30.0
AT_HEADS = 8
AT_KV_HEADS = 2
AT_HEAD_DIM = 64
AT_GROUP = AT_HEADS // AT_KV_HEADS
AT_Q_BLOCK = 128
ROPE_THETA = 10000.0
HY_WIDTH = 512
HY_EMB_DIM = 33
HY_BANDS = (HY_EMB_DIM - 1) // 2
HY_FILTER_WIDTH = 64
HY_INNER = 2
HY_FAST_DECAY = 0.3
HY_SLOW_DECAY = 1.5
HY_TARGET = 1e-2
D_FF = 2816
SPLIT_SIZES = (HG_WIDTH, HG_WIDTH, HG_WIDTH, HG_WIDTH, HG_WIDTH,
               AT_HEADS * AT_HEAD_DIM, AT_KV_HEADS * AT_HEAD_DIM, AT_KV_HEADS * AT_HEAD_DIM,
               3 * HY_WIDTH, 3 * D_MODEL)
D_IN = sum(SPLIT_SIZES)
SPLIT_POINTS = tuple(sum(SPLIT_SIZES[:i + 1]) for i in range(len(SPLIT_SIZES) - 1))

kernel_name = "hybrid_dit_hgrn2_gqa_hyena_ctxprefix"


def rmsnorm(x, g):
    xf = x.astype(F32)
    y = xf * lax.rsqrt(jnp.mean(xf * xf, axis=-1, keepdims=True) + NORM_EPS)
    return (y * g.astype(F32)).astype(x.dtype)


def modulate(x, g, shift, scale):
    return rmsnorm(x, g) * (1 + scale) + shift


def dwconv3(x, w, b):
    xp = jnp.pad(x, ((0, 0), (1, 1), (0, 0)))
    return xp[:, :-2] * w[0] + xp[:, 1:-1] * w[1] + xp[:, 2:] * w[2] + b


def rev(a):
    return a[:, ::-1]


def hg_heads(z):
    return z.astype(F32).reshape(z.shape[:2] + (HG_HEADS, HG_HEAD_DIM))


def hgrn_decay(zf, lb):
    zf = hg_heads(zf)
    lb = lb.reshape(HG_HEADS, HG_HEAD_DIM)
    log_f = jax.nn.log_sigmoid(zf) + jnp.log1p(lb * jnp.exp(jnp.minimum(-zf, HG_EXP_CLIP)))
    k = (1.0 - lb) * jax.nn.sigmoid(-zf)
    return log_f, k


def hgrn_chunk_scan(q, log_f, k, v, s0):
    B, L, H, _ = q.shape
    n = L // HG_CHUNK
    mask = jnp.tril(jnp.ones((HG_CHUNK, HG_CHUNK), dtype=bool))[:, :, None]

    def to_chunks(a):
        return a.reshape(B, n, HG_CHUNK, H, a.shape[-1]).transpose(1, 0, 3, 2, 4)

    def step(S, inp):
        qc, lfc, kc, vc = inp
        G = jnp.cumsum(lfc, axis=2)
        diff = G[:, :, :, None, :] - G[:, :, None, :, :]
        decay = jnp.where(mask, jnp.exp(jnp.where(mask, diff, 0.0)), 0.0)
        A = jnp.einsum('bhtk,bhtsk,bhsk->bhts', qc, decay, kc)
        o = jnp.einsum('bhts,bhsv->bhtv', A, vc) + jnp.einsum('bhtk,bhkv->bhtv', qc * jnp.exp(G), S)
        G_last = G[:, :, -1:, :]
        S = jnp.exp(G_last[:, :, 0, :])[..., None] * S + jnp.einsum('bhsk,bhsv->bhkv', kc * jnp.exp(G_last - G), vc)
        return S, o

    S, o = lax.scan(step, s0, (to_chunks(q), to_chunks(log_f), to_chunks(k), to_chunks(v)))
    o = o.transpose(1, 0, 3, 2, 4).reshape(B, L, H, v.shape[-1])
    return o, S


def hgrn_final_state(log_f, k, v):
    rest = lax.cumsum(log_f, axis=1, reverse=True) - log_f
    return jnp.einsum('blhk,blhv->bhkv', k * jnp.exp(rest), v)


def hgrn_out(o, zg, g_norm, dtype):
    B, L = o.shape[:2]
    return (rmsnorm(o, g_norm).reshape(B, L, HG_WIDTH) * jax.nn.silu(zg.astype(F32))).astype(dtype)


def axial_rope(L):
    rows = L // GRID_W
    row = jnp.repeat(jnp.arange(rows), GRID_W).astype(F32)
    col = jnp.tile(jnp.arange(GRID_W), rows).astype(F32)
    n_freq = AT_HEAD_DIM // 4
    inv = ROPE_THETA ** (-jnp.arange(n_freq, dtype=F32) / n_freq)
    ang = jnp.concatenate([row[:, None] * inv, col[:, None] * inv], axis=-1)
    return jnp.cos(ang), jnp.sin(ang)


def apply_rope(x, cos, sin):
    xf = x.astype(F32).reshape(x.shape[:-1] + (AT_HEAD_DIM // 2, 2))
    x1, x2 = xf[..., 0], xf[..., 1]
    c = cos[None, :, None, :]
    s = sin[None, :, None, :]
    out = jnp.stack([x1 * c - x2 * s, x1 * s + x2 * c], axis=-1)
    return out.reshape(x.shape).astype(x.dtype)


def attn_heads(zq, zk, zv, q_g, k_g):
    B, L = zq.shape[:2]
    q = rmsnorm(zq.reshape(B, L, AT_HEADS, AT_HEAD_DIM), q_g)
    k = rmsnorm(zk.reshape(B, L, AT_KV_HEADS, AT_HEAD_DIM), k_g)
    v = zv.reshape(B, L, AT_KV_HEADS, AT_HEAD_DIM)
    return q, k, v


def gqa_softmax(q, k, v):
    s = jnp.einsum('bqhgd,bkhd->bhgqk', q, k).astype(F32) * (AT_HEAD_DIM ** -0.5)
    p = jax.nn.softmax(s, axis=-1).astype(v.dtype)
    return jnp.einsum('bhgqk,bkhd->bqhgd', p, v)


def attend_latent(q, k_all, v_all):
    B, L = q.shape[:2]
    nb = L // AT_Q_BLOCK
    qb = q.reshape(B, nb, AT_Q_BLOCK, AT_KV_HEADS, AT_GROUP, AT_HEAD_DIM).transpose(1, 0, 2, 3, 4, 5)
    out = lax.map(lambda qblk: gqa_softmax(qblk, k_all, v_all), qb)
    return out.transpose(1, 0, 2, 3, 4, 5).reshape(B, L, AT_HEADS * AT_HEAD_DIM)


def hyena_filters(L, w1, b1, wi, bi, freq, w_last):
    t = jnp.linspace(0.0, 1.0, L, dtype=F32)[:, None]
    w = 2.0 * math.pi * jnp.arange(L, dtype=F32)[:, None] / L
    f = jnp.linspace(1e-4, HY_BANDS - 1, HY_BANDS, dtype=F32)[None, :]
    z = jnp.concatenate([t, jnp.cos(f * w), -jnp.sin(f * w)], axis=-1)
    fr = freq.astype(F32)
    h = jnp.sin(fr * (z @ w1.astype(F32) + b1.astype(F32)))
    for j in range(HY_INNER):
        h = jnp.sin(fr * (h @ wi[j].astype(F32) + bi[j].astype(F32)))
    h = h @ w_last.astype(F32)
    max_decay = math.log(HY_TARGET) / HY_FAST_DECAY
    min_decay = math.log(HY_TARGET) / HY_SLOW_DECAY
    deltas = jnp.abs(jnp.linspace(min_decay, max_decay, HY_WIDTH, dtype=F32))
    decay = jnp.exp(-t * deltas)
    return h[:, :HY_WIDTH] * decay, h[:, HY_WIDTH:] * decay


def bidir_fftconv(u, h_f, h_b):
    L = u.shape[1]
    kern = jnp.concatenate([(h_f[0] + h_b[0])[None], h_f[1:], jnp.zeros_like(h_f[:1]), h_b[:0:-1]], axis=0)
    U = jnp.fft.rfft(u, n=2 * L, axis=1)
    K = jnp.fft.rfft(kern, axis=0)
    return jnp.fft.irfft(U * K[None], n=2 * L, axis=1)[:, :L]


def hyena(z, conv_w, conv_b, filt, d_bias):
    zc = dwconv3(z, conv_w, conv_b).astype(F32)
    x0, x1, v = jnp.split(zc, 3, axis=-1)
    h_f, h_b = filt
    u = v * x1
    y = bidir_fftconv(u, h_f, h_b) + u * d_bias.astype(F32)
    return (y * x0).astype(z.dtype)


def mixer(hx, hc, p, lb_f, lb_b, need_ctx):
    B, L, _ = hx.shape
    Lc = hc.shape[1]
    dt = hx.dtype
    px = jnp.split(hx @ p['w_in'], SPLIT_POINTS, axis=-1)
    pc = jnp.split(hc @ p['w_in'], SPLIT_POINTS, axis=-1)

    qx, vx, qc, vc = hg_heads(px[0]), hg_heads(px[3]), hg_heads(pc[0]), hg_heads(pc[3])
    lfx_f, kx_f = hgrn_decay(px[1], lb_f)
    lfx_b, kx_b = hgrn_decay(px[2], lb_b)
    lfc_f, kc_f = hgrn_decay(pc[1], lb_f)
    lfc_b, kc_b = hgrn_decay(pc[2], lb_b)
    if need_ctx:
        s0 = jnp.zeros((B, HG_HEADS, HG_HEAD_DIM, HG_HEAD_DIM), F32)
        oc_f, sf = hgrn_chunk_scan(qc, lfc_f, kc_f, vc, s0)
        oc_b, sb = hgrn_chunk_scan(rev(qc), rev(lfc_b), rev(kc_b), rev(vc), s0)
        a_c = hgrn_out(oc_f + rev(oc_b), pc[4], p['hg_norm'], dt)
    else:
        sf = hgrn_final_state(lfc_f, kc_f, vc)
        sb = hgrn_final_state(rev(lfc_b), rev(kc_b), rev(vc))
    ox_f, _ = hgrn_chunk_scan(qx, lfx_f, kx_f, vx, sf)
    ox_b, _ = hgrn_chunk_scan(rev(qx), rev(lfx_b), rev(kx_b), rev(vx), sb)
    a_x = hgrn_out(ox_f + rev(ox_b), px[4], p['hg_norm'], dt)

    cos, sin = axial_rope(L)
    aq_x, ak_x, av_x = attn_heads(px[5], px[6], px[7], p['q_norm'], p['k_norm'])
    aq_c, ak_c, av_c = attn_heads(pc[5], pc[6], pc[7], p['q_norm'], p['k_norm'])
    aq_x = apply_rope(aq_x, cos, sin)
    ak_x = apply_rope(ak_x, cos, sin)
    k_all = jnp.concatenate([ak_c, ak_x], axis=1)
    v_all = jnp.concatenate([av_c, av_x], axis=1)
    b_x = attend_latent(aq_x, k_all, v_all)

    filt_args = (p['hy_w1'], p['hy_b1'], p['hy_wi'], p['hy_bi'], p['hy_freq'], p['hy_w_last'])
    c_x = hyena(px[8], p['hy_conv_w'], p['hy_conv_b'], hyena_filters(L, *filt_args), p['hy_bias'])

    def merge(parts, a, b, c):
        g_a, g_b, g_c = jnp.split(parts[9], 3, axis=-1)
        m = (jax.nn.sigmoid(g_a) * (a @ p['w_oa']) + jax.nn.sigmoid(g_b) * (b @ p['w_ob'])
             + jax.nn.sigmoid(g_c) * (c @ p['w_oc']))
        return m @ p['w_out']

    yx = merge(px, a_x, b_x, c_x)
    if need_ctx:
        b_c = gqa_softmax(aq_c.reshape(B, Lc, AT_KV_HEADS, AT_GROUP, AT_HEAD_DIM), ak_c, av_c).reshape(B, Lc, AT_HEADS * AT_HEAD_DIM)
        c_c = hyena(pc[8], p['hy_conv_w'], p['hy_conv_b'], hyena_filters(Lc, *filt_args), p['hy_bias'])
        return yx, merge(pc, a_c, b_c, c_c)
    return yx, None


def conv_ffn(h, w_up, cw, cb, w_down):
    u = dwconv3(h @ w_up, cw, cb)
    a, b = jnp.split(u, 2, axis=-1)
    return (jax.nn.silu(a) * b) @ w_down


def setup_inputs(seed: int = 0) -> dict:
    key = jax.random.key(seed)
    D = D_MODEL
    specs = [
        ('x', (BATCH, SEQ, D), 1.0, 0.0),
        ('c', (BATCH, D), 1.0, 0.0),
        ('ctx', (BATCH, CTX_LEN, D), 1.0, 0.0),
        ('c_ctx', (D,), 1.0, 0.0),
        ('w_ada', (DEPTH, D, 6 * D), 0.5 * D ** -0.5, 0.0),
        ('b_ada', (DEPTH, 6 * D), 0.02, 0.0),
        ('g_pre_mix', (DEPTH, D), 0.02, 1.0),
        ('g_post_mix', (DEPTH, D), 0.02, 1.0),
        ('g_pre_ffn', (DEPTH, D), 0.02, 1.0),
        ('g_post_ffn', (DEPTH, D), 0.02, 1.0),
        ('w_in', (DEPTH, D, D_IN), D ** -0.5, 0.0),
        ('hg_lower_bounds', (DEPTH, 2, HG_WIDTH), 0.1, 0.0),
        ('hg_norm', (DEPTH, HG_HEAD_DIM), 0.02, 1.0),
        ('q_norm', (DEPTH, AT_HEAD_DIM), 0.02, 1.0),
        ('k_norm', (DEPTH, AT_HEAD_DIM), 0.02, 1.0),
        ('hy_conv_w', (DEPTH, 3, 3 * HY_WIDTH), 0.5, 0.0),
        ('hy_conv_b', (DEPTH, 3 * HY_WIDTH), 0.02, 0.0),
        ('hy_w1', (DEPTH, HY_EMB_DIM, HY_FILTER_WIDTH), HY_EMB_DIM ** -0.5, 0.0),
        ('hy_b1', (DEPTH, HY_FILTER_WIDTH), 0.1, 0.0),
        ('hy_wi', (DEPTH, HY_INNER, HY_FILTER_WIDTH, HY_FILTER_WIDTH), HY_FILTER_WIDTH ** -0.5, 0.0),
        ('hy_bi', (DEPTH, HY_INNER, HY_FILTER_WIDTH), 0.1, 0.0),
        ('hy_freq', (DEPTH, HY_FILTER_WIDTH), 0.02, 1.0),
        ('hy_w_last', (DEPTH, HY_FILTER_WIDTH, 2 * HY_WIDTH), 0.05 * HY_FILTER_WIDTH ** -0.5, 0.0),
        ('hy_bias', (DEPTH, HY_WIDTH), 1.0, 0.0),
        ('w_oa', (DEPTH, HG_WIDTH, D), HG_WIDTH ** -0.5, 0.0),
        ('w_ob', (DEPTH, AT_HEADS * AT_HEAD_DIM, D), (AT_HEADS * AT_HEAD_DIM) ** -0.5, 0.0),
        ('w_oc', (DEPTH, HY_WIDTH, D), HY_WIDTH ** -0.5, 0.0),
        ('w_out', (DEPTH, D, D), D ** -0.5, 0.0),
        ('w_up', (DEPTH, D, 2 * D_FF), D ** -0.5, 0.0),
        ('ffn_conv_w', (DEPTH, 3, 2 * D_FF), 0.5, 0.0),
        ('ffn_conv_b', (DEPTH, 2 * D_FF), 0.02, 0.0),
        ('w_down', (DEPTH, D_FF, D), D_FF ** -0.5, 0.0),
    ]
    keys = jax.random.split(key, len(specs))
    return {name: off + scale * jax.random.normal(k, shape, jnp.float32)
            for (name, shape, scale, off), k in zip(specs, keys)}


def reference(x, c, ctx, c_ctx, w_ada, b_ada, g_pre_mix, g_post_mix, g_pre_ffn, g_post_ffn,
              w_in, hg_lower_bounds, hg_norm, q_norm, k_norm, hy_conv_w, hy_conv_b, hy_w1, hy_b1,
              hy_wi, hy_bi, hy_freq, hy_w_last, hy_bias, w_oa, w_ob, w_oc, w_out,
              w_up, ffn_conv_w, ffn_conv_b, w_down):
    lbp = jax.nn.softmax(hg_lower_bounds.astype(F32), axis=0)
    lower = jnp.cumsum(lbp, axis=0) - lbp[0]
    src_x = jax.nn.silu(c)
    src_c = jax.nn.silu(c_ctx)
    for l in range(DEPTH):
        need_ctx = l < DEPTH - 1
        mx = jnp.split((src_x @ w_ada[l] + b_ada[l])[:, None, :], 6, axis=-1)
        mc = jnp.split((src_c @ w_ada[l] + b_ada[l])[None, None, :], 6, axis=-1)
        p = {'w_in': w_in[l], 'hg_norm': hg_norm[l], 'q_norm': q_norm[l], 'k_norm': k_norm[l],
             'hy_conv_w': hy_conv_w[l], 'hy_conv_b': hy_conv_b[l], 'hy_w1': hy_w1[l], 'hy_b1': hy_b1[l],
             'hy_wi': hy_wi[l], 'hy_bi': hy_bi[l], 'hy_freq': hy_freq[l], 'hy_w_last': hy_w_last[l],
             'hy_bias': hy_bias[l], 'w_oa': w_oa[l], 'w_ob': w_ob[l], 'w_oc': w_oc[l], 'w_out': w_out[l]}
        hx = modulate(x, g_pre_mix[l], mx[0], mx[1])
        hc = modulate(ctx, g_pre_mix[l], mc[0], mc[1])
        yx, yc = mixer(hx, hc, p, lower[l, 0], lower[l, 1], need_ctx)
        x = x + mx[2] * rmsnorm(yx, g_post_mix[l])
        fx = conv_ffn(modulate(x, g_pre_ffn[l], mx[3], mx[4]), w_up[l], ffn_conv_w[l], ffn_conv_b[l], w_down[l])
        x = x + mx[5] * rmsnorm(fx, g_post_ffn[l])
        if need_ctx:
            ctx = ctx + mc[2] * rmsnorm(yc, g_post_mix[l])
            fc = conv_ffn(modulate(ctx, g_pre_ffn[l], mc[3], mc[4]), w_up[l], ffn_conv_w[l], ffn_conv_b[l], w_down[l])
            ctx = ctx + mc[5] * rmsnorm(fc, g_post_ffn[l])
    return x
```

```python
import functools
import math

import jax
import jax.numpy as jnp
import numpy as np
from jax import lax
from jax.experimental import pallas as pl
from jax.experimental.pallas import tpu as pltpu

F32 = jnp.float32
BF16 = jnp.bfloat16

NORM_EPS = 1e-6
GRID_W = 64
HG_HEADS = 4
HG_DIM = 128
HG_WIDTH = HG_HEADS * HG_DIM
HG_EXP_CLIP = 30.0
AT_HEADS = 8
AT_KV_HEADS = 2
AT_DIM = 64
AT_GROUP = AT_HEADS // AT_KV_HEADS
AT_WIDTH = AT_HEADS * AT_DIM
AT_KV_WIDTH = AT_KV_HEADS * AT_DIM
ROPE_THETA = 10000.0
HY_WIDTH = 512
HY_EMB_DIM = 33
HY_BANDS = (HY_EMB_DIM - 1) // 2
HY_FILTER_WIDTH = 64
HY_INNER = 2
HY_FAST_DECAY = 0.3
HY_SLOW_DECAY = 1.5
HY_TARGET = 1e-2

LANE = 128
BF16_SUBLANES = 16
ROW_TILE = 256
HG_CHUNK = 128
HG_LEVELS = tuple(HG_CHUNK >> (j + 1) for j in range(int(math.log2(HG_CHUNK))))
MM_ROWS = 1024
VMEM_CAP = 56 * 1024 * 1024


def _cparams(n_axes, vmem_mb):
    return pltpu.CompilerParams(
        dimension_semantics=("arbitrary",) * n_axes,
        vmem_limit_bytes=min(int(vmem_mb) * 1024 * 1024, VMEM_CAP))


def _dot(a, b):
    return jnp.dot(a, b, preferred_element_type=F32)


def _dot_nt(a, b):
    return lax.dot_general(a, b, (((1,), (1,)), ((), ())), preferred_element_type=F32)


def _split_bf16(a):
    hi = a.astype(BF16)
    lo = (a - hi.astype(F32)).astype(BF16)
    return hi, lo


def _dot3(a, b):
    ah, al = _split_bf16(a)
    bh, bl = _split_bf16(b)
    return _dot(ah, bh) + (_dot(ah, bl) + _dot(al, bh))


def _rms(x, g):
    return x * lax.rsqrt(jnp.mean(x * x, axis=-1, keepdims=True) + NORM_EPS) * g


def _sigmoid(x):
    return 1.0 / (1.0 + jnp.exp(-x))


def _ada_kernel(src_ref, w_ref, b_ref, o_ref):
    s = src_ref[...]
    s = s * _sigmoid(s)
    o_ref[0] = _dot3(s, w_ref[0]) + b_ref[0]


def _ada(src, w_ada, b_ada):
    depth, d, d6 = w_ada.shape
    rp = src.shape[0]
    tn = d
    return pl.pallas_call(
        _ada_kernel,
        grid=(depth, d6 // tn),
        in_specs=[pl.BlockSpec((rp, d), lambda l, j: (0, 0)),
                  pl.BlockSpec((1, d, tn), lambda l, j: (l, 0, j)),
                  pl.BlockSpec((1, 1, tn), lambda l, j: (l, 0, j))],
        out_specs=pl.BlockSpec((1, rp, tn), lambda l, j: (l, 0, j)),
        out_shape=jax.ShapeDtypeStruct((depth, rp, d6), F32),
        compiler_params=_cparams(2, 32),
        name="ada",
    )(src, w_ada, b_ada.reshape(depth, 1, d6))


def _mod_kernel(x_ref, m_ref, g_ref, o_ref, *, k_shift, k_scale):
    y = _rms(x_ref[0], g_ref[...])
    shift = m_ref[0, k_shift:k_shift + 1, :]
    scale = m_ref[0, k_scale:k_scale + 1, :]
    o_ref[0] = (y * (1.0 + scale) + shift).astype(o_ref.dtype)


def _modulate(xs, mods, g, k_shift, k_scale, n_ctx_tiles):
    b, s, d = xs.shape
    nt = s // ROW_TILE
    ctx_row = b

    def mrow(bi, i):
        return (jnp.where(i < n_ctx_tiles, ctx_row, bi), 0, 0)

    return pl.pallas_call(
        functools.partial(_mod_kernel, k_shift=k_shift, k_scale=k_scale),
        grid=(b, nt),
        in_specs=[pl.BlockSpec((1, ROW_TILE, d), lambda bi, i: (bi, i, 0)),
                  pl.BlockSpec((1, 6, d), mrow),
                  pl.BlockSpec((1, d), lambda bi, i: (0, 0))],
        out_specs=pl.BlockSpec((1, ROW_TILE, d), lambda bi, i: (bi, i, 0)),
        out_shape=jax.ShapeDtypeStruct((b, s, d), BF16),
        compiler_params=_cparams(2, 16),
        name="modulate",
    )(xs, mods, g.reshape(1, d))


def _mm_kernel(a_ref, b_ref, o_ref):
    o_ref[...] = _dot(a_ref[...], b_ref[...]).astype(o_ref.dtype)


def _matmul(a, w, out_dtype, tn, name):
    m, k = a.shape
    n = w.shape[1]
    tm = MM_ROWS
    assert m % tm == 0 and n % tn == 0
    return pl.pallas_call(
        _mm_kernel,
        grid=(m // tm, n // tn),
        in_specs=[pl.BlockSpec((tm, k), lambda i, j: (i, 0)),
                  pl.BlockSpec((k, tn), lambda i, j: (0, j))],
        out_specs=pl.BlockSpec((tm, tn), lambda i, j: (i, j)),
        out_shape=jax.ShapeDtypeStruct((m, n), out_dtype),
        compiler_params=_cparams(2, 48),
        name=name,
    )(a, w)


def _mm3_kernel(a_ref, b_ref, o_ref):
    o_ref[...] = _dot3(a_ref[...], b_ref[...])


def _matmul_f32(a, w, tm, name):
    m, k = a.shape
    n = w.shape[1]
    return pl.pallas_call(
        _mm3_kernel,
        grid=(m // tm,),
        in_specs=[pl.BlockSpec((tm, k), lambda i: (i, 0)),
                  pl.BlockSpec((k, n), lambda i: (0, 0))],
        out_specs=pl.BlockSpec((tm, n), lambda i: (i, 0)),
        out_shape=jax.ShapeDtypeStruct((m, n), F32),
        compiler_params=_cparams(1, 48),
        name=name,
    )(a, w)


def _hg_scan_matrix(reverse):
    t_n = HG_CHUNK
    t = np.arange(t_n)[:, None]
    u = np.arange(t_n)[None, :]
    rows = [(u >= t) if reverse else (u <= t)]
    for w in HG_LEVELS:
        base = (t // (2 * w)) * (2 * w)
        mid = base + w
        upper = (t - base) >= w
        if reverse:
            m = np.where(upper, (u >= mid) & (u < t), (u >= t) & (u < mid))
        else:
            m = np.where(upper, (u >= mid) & (u <= t), (u > t) & (u < mid))
        rows.append(m)
    return np.concatenate(rows, axis=0).astype(np.float32)


def _hgrn_unit(q, z, v, lb, p_mat, masks, st, reverse):
    t_n = HG_CHUNK
    e = jnp.exp(-jnp.abs(z))
    ope = 1.0 + e
    log_sig = jnp.minimum(z, 0.0) - jnp.log(ope)
    e_clip = jnp.exp(jnp.minimum(-z, HG_EXP_CLIP))
    lf = log_sig + jnp.log(1.0 + lb * e_clip)
    kk = (1.0 - lb) * (jnp.where(z >= 0.0, e, 1.0) / ope)

    hi, lo = _split_bf16(lf)
    x = _dot(p_mat, jnp.concatenate([hi, lo], axis=1))
    x = x[:, :HG_DIM] + x[:, HG_DIM:]
    g = x[0:t_n]

    a = jnp.zeros((t_n, t_n), F32)
    for j in range(len(HG_LEVELS)):
        ew = jnp.exp(x[(j + 1) * t_n:(j + 2) * t_n])
        pw = _dot_nt((q * ew).astype(BF16), (kk * ew).astype(BF16))
        a = jnp.where(masks[j], pw, a)
    dqk = jnp.sum(q * kk, axis=1, keepdims=True)

    g_last = g[0:1] if reverse else g[t_n - 1:t_n]
    vb = v.astype(BF16)
    o = (_dot(a.astype(BF16), vb) + dqk * v
         + _dot_nt((q * jnp.exp(g)).astype(BF16), st.astype(BF16)))
    kd = (kk * jnp.exp(g_last - g)).astype(BF16)
    st_new = st * jnp.exp(g_last) + _dot(v.T.astype(BF16), kd)
    return o, st_new


def _hgrn_kernel(qf_ref, zf_ref, vf_ref, qb_ref, zb_ref, vb_ref, lb_ref, pf_ref, pb_ref,
                 of_ref, ob_ref, s_ref):
    @pl.when(pl.program_id(1) == 0)
    def _():
        s_ref[...] = jnp.zeros_like(s_ref)

    t_n = HG_CHUNK
    ti = lax.broadcasted_iota(jnp.int32, (t_n, t_n), 0)
    si = lax.broadcasted_iota(jnp.int32, (t_n, t_n), 1)
    tx = ti ^ si
    dirs = ((qf_ref, zf_ref, vf_ref, pf_ref, of_ref), (qb_ref, zb_ref, vb_ref, pb_ref, ob_ref))
    for d, (q_ref, z_ref, v_ref, p_ref, o_ref) in enumerate(dirs):
        reverse = d == 1
        later = (ti < si) if reverse else (ti > si)
        masks = [later & (tx >= w) & (tx < 2 * w) for w in HG_LEVELS]
        p_mat = p_ref[...]
        for h in range(HG_HEADS):
            sl = slice(h * HG_DIM, (h + 1) * HG_DIM)
            o, st_new = _hgrn_unit(q_ref[0, :, sl], z_ref[0, :, sl], v_ref[0, :, sl],
                                   lb_ref[d:d + 1, sl], p_mat, masks,
                                   s_ref[d * HG_HEADS + h], reverse)
            o_ref[0, :, sl] = o
            s_ref[d * HG_HEADS + h] = st_new


def _hgrn(p_hg, lb, n_ctx_rows):
    b, s, _ = p_hg.shape
    nb = s // HG_CHUNK
    nc = n_ctx_rows // HG_CHUNK

    def bidx(n):
        return jnp.where(n < nc, nc - 1 - n, nb - 1 - (n - nc))

    blk = (1, HG_CHUNK, HG_WIDTH)
    pf = jnp.asarray(_hg_scan_matrix(False)).astype(BF16)
    pb = jnp.asarray(_hg_scan_matrix(True)).astype(BF16)
    pshape = pf.shape
    return pl.pallas_call(
        _hgrn_kernel,
        grid=(b, nb),
        in_specs=[pl.BlockSpec(blk, lambda bi, n: (bi, n, 0)),
                  pl.BlockSpec(blk, lambda bi, n: (bi, n, 1)),
                  pl.BlockSpec(blk, lambda bi, n: (bi, n, 3)),
                  pl.BlockSpec(blk, lambda bi, n: (bi, bidx(n), 0)),
                  pl.BlockSpec(blk, lambda bi, n: (bi, bidx(n), 2)),
                  pl.BlockSpec(blk, lambda bi, n: (bi, bidx(n), 3)),
                  pl.BlockSpec((2, HG_WIDTH), lambda bi, n: (0, 0)),
                  pl.BlockSpec(pshape, lambda bi, n: (0, 0)),
                  pl.BlockSpec(pshape, lambda bi, n: (0, 0))],
        out_specs=[pl.BlockSpec(blk, lambda bi, n: (bi, n, 0)),
                   pl.BlockSpec(blk, lambda bi, n: (bi, bidx(n), 0))],
        out_shape=[jax.ShapeDtypeStruct((b, s, HG_WIDTH), F32)] * 2,
        scratch_shapes=[pltpu.VMEM((2 * HG_HEADS, HG_DIM, HG_DIM), F32)],
        compiler_params=_cparams(2, 40),
        name="hgrn",
    )(p_hg, p_hg, p_hg, p_hg, p_hg, p_hg, lb, pf, pb)


def _rope(x, cos, sin_signed, first_half):
    n = x.shape[-1]
    half = AT_DIM // 2
    partner = jnp.where(first_half, pltpu.roll(x, n - half, axis=1), pltpu.roll(x, half, axis=1))
    return x * cos + partner * sin_signed


def _head_norm(x, gain, group_mean):
    ms = _dot((x * x).astype(BF16), group_mean)
    return x * lax.rsqrt(ms + NORM_EPS) * gain


def _attn_kernel(q_ref, k_ref, v_ref, cq_ref, sq_ref, ck_ref, sk_ref, gq_ref, gk_ref, mq_ref, mk_ref,
                 o_ref, k_scr, *, q_off, n_ctx_tiles, n_ctx_rows):
    i = pl.program_id(1)

    @pl.when(i == 0)
    def _():
        kr = k_ref[0].astype(F32)
        lane = lax.broadcasted_iota(jnp.int32, kr.shape, 1)
        kn = _head_norm(kr, gk_ref[...], mk_ref[...])
        k_scr[...] = _rope(kn, ck_ref[...], sk_ref[...], (lane % AT_DIM) < AT_DIM // 2).astype(BF16)

    qr = q_ref[0].astype(F32)
    lane = lax.broadcasted_iota(jnp.int32, qr.shape, 1)
    qn = _head_norm(qr, gq_ref[...], mq_ref[...])
    qn = _rope(qn, cq_ref[...], sq_ref[...], (lane % AT_DIM) < AT_DIM // 2)
    qn = qn * (AT_DIM ** -0.5)
    lane_t = lax.broadcasted_iota(jnp.int32, (ROW_TILE, LANE), 1)
    kv0 = lane_t < AT_DIM

    def attend(n_keys):
        keys = k_scr[0:n_keys, :]
        vals = v_ref[0, 0:n_keys, :]
        for j in range(AT_WIDTH // LANE):
            qt = qn[:, j * LANE:(j + 1) * LANE]
            outs = []
            for sel in (kv0, ~kv0):
                s = _dot_nt(jnp.where(sel, qt, 0.0).astype(BF16), keys)
                m = jnp.max(s, axis=1, keepdims=True)
                p = jnp.exp(s - m)
                den = jnp.sum(p, axis=1, keepdims=True)
                outs.append(_dot(p.astype(BF16), vals) / den)
            o_ref[0, :, j * LANE:(j + 1) * LANE] = jnp.where(kv0, outs[0], outs[1]).astype(o_ref.dtype)

    n_all = k_scr.shape[0]
    if q_off < n_ctx_tiles:
        @pl.when(i + q_off < n_ctx_tiles)
        def _():
            attend(n_ctx_rows)

        @pl.when(i + q_off >= n_ctx_tiles)
        def _():
            attend(n_all)
    else:
        attend(n_all)


def _attention(p_rest, col_q, col_k, col_v, tabs, gq, gk, q_off, n_ctx_rows):
    b, s, _ = p_rest.shape
    nt = s // ROW_TILE
    cq, sq, ck, sk = tabs
    mq = jnp.asarray(np.kron(np.eye(AT_HEADS), np.full((AT_DIM, AT_DIM), 1.0 / AT_DIM)), BF16)
    mk = jnp.asarray(np.kron(np.eye(AT_KV_HEADS), np.full((AT_DIM, AT_DIM), 1.0 / AT_DIM)), BF16)
    kern = functools.partial(_attn_kernel, q_off=q_off, n_ctx_tiles=n_ctx_rows // ROW_TILE,
                             n_ctx_rows=n_ctx_rows)
    return pl.pallas_call(
        kern,
        grid=(b, nt - q_off),
        in_specs=[pl.BlockSpec((1, ROW_TILE, AT_WIDTH), lambda bi, i: (bi, i + q_off, col_q)),
                  pl.BlockSpec((1, s, AT_KV_WIDTH), lambda bi, i: (bi, 0, col_k)),
                  pl.BlockSpec((1, s, AT_KV_WIDTH), lambda bi, i: (bi, 0, col_v)),
                  pl.BlockSpec((ROW_TILE, AT_WIDTH), lambda bi, i: (i + q_off, 0)),
                  pl.BlockSpec((ROW_TILE, AT_WIDTH), lambda bi, i: (i + q_off, 0)),
                  pl.BlockSpec((s, AT_KV_WIDTH), lambda bi, i: (0, 0)),
                  pl.BlockSpec((s, AT_KV_WIDTH), lambda bi, i: (0, 0)),
                  pl.BlockSpec((1, AT_WIDTH), lambda bi, i: (0, 0)),
                  pl.BlockSpec((1, AT_KV_WIDTH), lambda bi, i: (0, 0)),
                  pl.BlockSpec((AT_WIDTH, AT_WIDTH), lambda bi, i: (0, 0)),
                  pl.BlockSpec((AT_KV_WIDTH, AT_KV_WIDTH), lambda bi, i: (0, 0))],
        out_specs=pl.BlockSpec((1, ROW_TILE, AT_WIDTH), lambda bi, i: (bi, i + q_off, 0)),
        out_shape=jax.ShapeDtypeStruct((b, s, AT_WIDTH), BF16),
        scratch_shapes=[pltpu.VMEM((s, AT_KV_WIDTH), BF16)],
        compiler_params=_cparams(2, 48),
        name="attention",
    )(p_rest, p_rest, p_rest, cq, sq, ck, sk, gq, gk, mq, mk)


def _rope_tables(n_ctx_rows, n_lat_rows):
    rows = n_lat_rows // GRID_W
    row = jnp.repeat(jnp.arange(rows), GRID_W).astype(F32)
    col = jnp.tile(jnp.arange(GRID_W), rows).astype(F32)
    n_freq = AT_DIM // 4
    inv = ROPE_THETA ** (-jnp.arange(n_freq, dtype=F32) / n_freq)
    ang = jnp.concatenate([row[:, None] * inv, col[:, None] * inv], axis=-1)
    cos = jnp.concatenate([jnp.cos(ang), jnp.cos(ang)], axis=-1)
    sin = jnp.concatenate([-jnp.sin(ang), jnp.sin(ang)], axis=-1)
    cos = jnp.concatenate([jnp.ones((n_ctx_rows, AT_DIM), F32), cos], axis=0)
    sin = jnp.concatenate([jnp.zeros((n_ctx_rows, AT_DIM), F32), sin], axis=0)
    return (jnp.tile(cos, (1, AT_HEADS)), jnp.tile(sin, (1, AT_HEADS)),
            jnp.tile(cos, (1, AT_KV_HEADS)), jnp.tile(sin, (1, AT_KV_HEADS)))


def _conv3(x, prev_row, next_row, w, bias):
    n = x.shape[0]
    r = lax.broadcasted_iota(jnp.int32, x.shape, 0)
    xm = jnp.where(r == 0, prev_row, pltpu.roll(x, 1, axis=0))
    xp = jnp.where(r == n - 1, next_row, pltpu.roll(x, n - 1, axis=0))
    return xm * w[0:1] + x * w[1:2] + xp * w[2:3] + bias


def _halo_specs(width, col, row_off, n_rows):
    per = ROW_TILE // BF16_SUBLANES
    last = n_rows // BF16_SUBLANES - 1
    return [
        pl.BlockSpec((1, ROW_TILE, width), lambda bi, i: (bi, i + row_off, col)),
        pl.BlockSpec((1, BF16_SUBLANES, width),
                     lambda bi, i: (bi, jnp.maximum((i + row_off) * per - 1, 0), col)),
        pl.BlockSpec((1, BF16_SUBLANES, width),
                     lambda bi, i: (bi, jnp.minimum((i + row_off + 1) * per, last), col)),
    ]


def _halo_rows(prev_ref, next_ref, is_first, is_last, cols=None):
    sl = slice(None) if cols is None else cols
    prev_row = prev_ref[0, BF16_SUBLANES - 1:BF16_SUBLANES, sl].astype(F32)
    next_row = next_ref[0, 0:1, sl].astype(F32)
    prev_row = jnp.where(is_first, 0.0, prev_row)
    next_row = jnp.where(is_last, 0.0, next_row)
    return prev_row, next_row


def _hypre_kernel(z_ref, zp_ref, zn_ref, w_ref, b_ref, db_ref, u_ref, ud_ref, x0_ref):
    i = pl.program_id(1)
    prev_row, next_row = _halo_rows(zp_ref, zn_ref, i == 0, i == pl.num_programs(1) - 1)
    zc = _conv3(z_ref[0].astype(F32), prev_row, next_row, w_ref[...], b_ref[...])
    x0 = zc[:, :HY_WIDTH]
    x1 = zc[:, HY_WIDTH:2 * HY_WIDTH]
    v = zc[:, 2 * HY_WIDTH:]
    u = v * x1
    u_ref[0] = u.astype(BF16)
    ud_ref[0] = (u * db_ref[...]).astype(BF16)
    x0_ref[0] = x0.astype(BF16)


def _hyena_pre(p_rest, col, row_off, n_rows, conv_w, conv_b, d_bias):
    b, s, _ = p_rest.shape
    width = 3 * HY_WIDTH
    out = jax.ShapeDtypeStruct((b, n_rows, HY_WIDTH), BF16)
    ospec = pl.BlockSpec((1, ROW_TILE, HY_WIDTH), lambda bi, i: (bi, i, 0))
    return pl.pallas_call(
        _hypre_kernel,
        grid=(b, n_rows // ROW_TILE),
        in_specs=_halo_specs(width, col, row_off, s) + [
            pl.BlockSpec((3, width), lambda bi, i: (0, 0)),
            pl.BlockSpec((1, width), lambda bi, i: (0, 0)),
            pl.BlockSpec((1, HY_WIDTH), lambda bi, i: (0, 0))],
        out_specs=[ospec, ospec, ospec],
        out_shape=[out, out, out],
        compiler_params=_cparams(2, 32),
        name="hyena_pre",
    )(p_rest, p_rest, p_rest, conv_w, conv_b.reshape(1, width), d_bias.reshape(1, HY_WIDTH))


def _hyfilt_kernel(z_ref, t_ref, dl_ref, w1_ref, b1_ref, wi_ref, bi_ref, fr_ref, wl_ref, o_ref):
    fr = fr_ref[...]
    h = jnp.sin(fr * (_dot3(z_ref[...], w1_ref[...]) + b1_ref[...]))
    for j in range(HY_INNER):
        h = jnp.sin(fr * (_dot3(h, wi_ref[j]) + bi_ref[j]))
    h = _dot3(h, wl_ref[...])
    decay = jnp.exp(-t_ref[...] * dl_ref[...])
    hf = h[:, :HY_WIDTH] * decay
    hb = h[:, HY_WIDTH:] * decay
    o_ref[...] = jnp.concatenate([hf + hb, hf - hb], axis=1)


def _pad2(a, rows, cols):
    return jnp.pad(a, ((0, rows - a.shape[0]), (0, cols - a.shape[1])))


def _hyena_filter_sums(n, w1, b1, wi, bi, freq, w_last):
    t = jnp.linspace(0.0, 1.0, n, dtype=F32)[:, None]
    w = 2.0 * math.pi * jnp.arange(n, dtype=F32)[:, None] / n
    f = jnp.linspace(1e-4, HY_BANDS - 1, HY_BANDS, dtype=F32)[None, :]
    z = jnp.concatenate([t, jnp.cos(f * w), -jnp.sin(f * w)], axis=-1)
    max_decay = math.log(HY_TARGET) / HY_FAST_DECAY
    min_decay = math.log(HY_TARGET) / HY_SLOW_DECAY
    deltas = jnp.abs(jnp.linspace(min_decay, max_decay, HY_WIDTH, dtype=F32))[None, :]
    zp = _pad2(z, n, LANE)
    w1p = _pad2(w1, LANE, LANE)
    b1p = _pad2(b1[None, :], 1, LANE)
    wip = jnp.stack([_pad2(wi[j], LANE, LANE) for j in range(HY_INNER)])
    bip = jnp.stack([_pad2(bi[j][None, :], 1, LANE) for j in range(HY_INNER)])
    frp = _pad2(freq[None, :], 1, LANE)
    wlp = _pad2(w_last, LANE, 2 * HY_WIDTH)
    tr = min(n, ROW_TILE)
    full = lambda shape: pl.BlockSpec(shape, lambda i: (0,) * len(shape))
    return pl.pallas_call(
        _hyfilt_kernel,
        grid=(n // tr,),
        in_specs=[pl.BlockSpec((tr, LANE), lambda i: (i, 0)),
                  pl.BlockSpec((tr, 1), lambda i: (i, 0)),
                  full((1, HY_WIDTH)), full((LANE, LANE)), full((1, LANE)),
                  full((HY_INNER, LANE, LANE)), full((HY_INNER, 1, LANE)), full((1, LANE)),
                  full((LANE, 2 * HY_WIDTH))],
        out_specs=pl.BlockSpec((tr, 2 * HY_WIDTH), lambda i: (i, 0)),
        out_shape=jax.ShapeDtypeStruct((n, 2 * HY_WIDTH), F32),
        compiler_params=_cparams(1, 32),
        name="hyena_filter",
    )(zp, t, deltas, w1p, b1p, wip, bip, frp, wlp)


def _dft_tables(n):
    f = jnp.arange(n, dtype=jnp.int32)[:, None]
    t = jnp.arange(n, dtype=jnp.int32)[None, :]
    ang = ((f * t) % (2 * n)).astype(F32) * (math.pi / n)
    nyq = jnp.where(t % 2 == 0, 1.0, -1.0).astype(F32)
    return jnp.stack([jnp.cos(ang), jnp.where(f == 0, nyq, jnp.sin(ang))])


def _hyfwd_kernel(u_ref, f_ref, co_ref, o_ref):
    u = u_ref[0]
    ure = _dot(f_ref[0], u)
    uim = _dot(f_ref[1], u)
    o_ref[0, 0] = (ure * co_ref[0] - uim * co_ref[1]).astype(BF16)
    o_ref[0, 1] = (ure * co_ref[2] + uim * co_ref[3]).astype(BF16)


def _hyinv_kernel(y_ref, ft_ref, ud_ref, x0_ref, o_ref):
    y = _dot(ft_ref[0], y_ref[0, 0]) + _dot(ft_ref[1], y_ref[0, 1])
    o_ref[0] = ((y + ud_ref[0].astype(F32)) * x0_ref[0].astype(F32)).astype(BF16)


def _hyena_conv(u, ud, x0, tables, coef):
    b, n, c = u.shape
    tf = min(n, ROW_TILE)
    f_bf = tables.astype(BF16)
    ft_bf = jnp.swapaxes(tables, 1, 2).astype(BF16)
    spec = pl.pallas_call(
        _hyfwd_kernel,
        grid=(n // tf, b),
        in_specs=[pl.BlockSpec((1, n, c), lambda j, bi: (bi, 0, 0)),
                  pl.BlockSpec((2, tf, n), lambda j, bi: (0, j, 0)),
                  pl.BlockSpec((4, tf, c), lambda j, bi: (0, j, 0))],
        out_specs=pl.BlockSpec((1, 2, tf, c), lambda j, bi: (bi, 0, j, 0)),
        out_shape=jax.ShapeDtypeStruct((b, 2, n, c), BF16),
        compiler_params=_cparams(2, 40),
        name="hyena_dft",
    )(u, f_bf, coef)
    return pl.pallas_call(
        _hyinv_kernel,
        grid=(b, n // tf),
        in_specs=[pl.BlockSpec((1, 2, n, c), lambda bi, j: (bi, 0, 0, 0)),
                  pl.BlockSpec((2, tf, n), lambda bi, j: (0, j, 0)),
                  pl.BlockSpec((1, tf, c), lambda bi, j: (bi, j, 0)),
                  pl.BlockSpec((1, tf, c), lambda bi, j: (bi, j, 0))],
        out_specs=pl.BlockSpec((1, tf, c), lambda bi, j: (bi, j, 0)),
        out_shape=jax.ShapeDtypeStruct((b, n, c), BF16),
        compiler_params=_cparams(2, 40),
        name="hyena_idft",
    )(spec, ft_bf, ud, x0)


def _hyena_coef(tables, hsum_hdiff):
    n = tables.shape[1]
    c = HY_WIDTH
    r = _matmul_f32(tables.reshape(2 * n, n), hsum_hdiff, min(n, ROW_TILE), "hyena_kernel_dft")
    k_re = r[:n, :c]
    k_im = r[n:, c:]
    k_nyq = r[n:n + 1, :c]
    first = (jnp.arange(n) == 0)[:, None]
    scale = jnp.where(first, 1.0 / (2 * n), 2.0 / (2 * n)).astype(F32)
    zero = jnp.zeros_like(k_im)
    return jnp.stack([k_re * scale,
                      jnp.where(first, zero, k_im * scale),
                      jnp.where(first, zero, k_im * scale),
                      jnp.where(first, k_nyq, k_re) * scale])


def _merge_kernel(*refs, n_ctx_tiles, row_off, has_ctx):
    if has_ctx:
        (of_ref, ob_ref, zg_ref, att_ref, cx_ref, cc_ref, gate_ref, x_ref, m_ref, ghg_ref, gpost_ref,
         woa_ref, wob_ref, woc_ref, wout_ref, o_ref) = refs
    else:
        (of_ref, ob_ref, zg_ref, att_ref, cx_ref, gate_ref, x_ref, m_ref, ghg_ref, gpost_ref,
         woa_ref, wob_ref, woc_ref, wout_ref, o_ref) = refs
    o = of_ref[0] + ob_ref[0]
    ghg = ghg_ref[...]
    a = jnp.concatenate([_rms(o[:, h * HG_DIM:(h + 1) * HG_DIM], ghg) for h in range(HG_HEADS)], axis=1)
    zg = zg_ref[0]
    a = a * (zg * _sigmoid(zg))
    c = cx_ref[0]
    if has_ctx:
        c = jnp.where(pl.program_id(1) + row_off < n_ctx_tiles, cc_ref[0], c)
    d = x_ref.shape[-1]
    ya = _dot(a.astype(BF16), woa_ref[...])
    yb = _dot(att_ref[0], wob_ref[...])
    yc = _dot(c, woc_ref[...])
    m = (_sigmoid(gate_ref[0, :, 0:d].astype(F32)) * ya
         + _sigmoid(gate_ref[0, :, d:2 * d].astype(F32)) * yb
         + _sigmoid(gate_ref[0, :, 2 * d:3 * d].astype(F32)) * yc)
    y = _dot(m.astype(BF16), wout_ref[...])
    o_ref[0] = x_ref[0] + m_ref[0, 2:3, :] * _rms(y, gpost_ref[...])


def _merge(o_f, o_b, p_hg, att, c_x, c_c, p_rest, xs, mods, g_hg, g_post, w_oa, w_ob, w_oc, w_out,
           row_off, n_ctx_rows):
    b, s, d = xs.shape
    nct = n_ctx_rows // ROW_TILE
    n_tiles = s // ROW_TILE - row_off
    has_ctx = c_c is not None
    ctx_row = b

    def stream(width, col=0):
        return pl.BlockSpec((1, ROW_TILE, width), lambda bi, i: (bi, i + row_off, col))

    def full(shape):
        return pl.BlockSpec(shape, lambda bi, i: (0,) * len(shape))

    in_specs = [stream(HG_WIDTH), stream(HG_WIDTH), stream(HG_WIDTH, 4), stream(AT_WIDTH),
                pl.BlockSpec((1, ROW_TILE, HY_WIDTH),
                             lambda bi, i: (bi, jnp.maximum(i + row_off - nct, 0), 0))]
    args = [o_f, o_b, p_hg, att, c_x]
    if has_ctx:
        in_specs.append(pl.BlockSpec((1, ROW_TILE, HY_WIDTH),
                                     lambda bi, i: (bi, jnp.minimum(i + row_off, nct - 1), 0)))
        args.append(c_c)
    in_specs += [stream(3 * d), stream(d),
                 pl.BlockSpec((1, 6, d), lambda bi, i: (jnp.where(i + row_off < nct, ctx_row, bi), 0, 0)),
                 full((1, HG_DIM)), full((1, d)),
                 full((HG_WIDTH, d)), full((AT_WIDTH, d)), full((HY_WIDTH, d)), full((d, d))]
    args += [p_rest, xs, mods, g_hg.reshape(1, HG_DIM), g_post.reshape(1, d), w_oa, w_ob, w_oc, w_out]
    return pl.pallas_call(
        functools.partial(_merge_kernel, n_ctx_tiles=nct, row_off=row_off, has_ctx=has_ctx),
        grid=(b, n_tiles),
        in_specs=in_specs,
        out_specs=pl.BlockSpec((1, ROW_TILE, d), lambda bi, i: (bi, i, 0)),
        out_shape=jax.ShapeDtypeStruct((b, n_tiles * ROW_TILE, d), F32),
        compiler_params=_cparams(2, 48),
        name="merge",
    )(*args)


FFN_COLS = 256


def _ffn_kernel(u_ref, up_ref, un_ref, w_ref, b_ref, x_ref, m_ref, g_ref, wd_ref, o_ref, *,
                first_tiles, last_tiles):
    i = pl.program_id(1)
    is_first = functools.reduce(jnp.logical_or, [i == t for t in first_tiles])
    is_last = functools.reduce(jnp.logical_or, [i == t for t in last_tiles])
    d_ff = wd_ref.shape[0]
    acc = jnp.zeros(o_ref.shape[1:], F32)
    for j in range(d_ff // FFN_COLS):
        halves = []
        for base in (0, d_ff):
            cols = slice(base + j * FFN_COLS, base + (j + 1) * FFN_COLS)
            prev_row, next_row = _halo_rows(up_ref, un_ref, is_first, is_last, cols)
            halves.append(_conv3(u_ref[0, :, cols].astype(F32), prev_row, next_row,
                                 w_ref[:, cols], b_ref[:, cols]))
        act = halves[0] * _sigmoid(halves[0]) * halves[1]
        acc = acc + _dot(act.astype(BF16), wd_ref[j * FFN_COLS:(j + 1) * FFN_COLS, :])
    o_ref[0] = x_ref[0] + m_ref[0, 5:6, :] * _rms(acc, g_ref[...])


def _ffn_tail(pu, xs, mods, conv_w, conv_b, g_post, w_down, n_ctx_rows):
    b, s, d = xs.shape
    d_ff = w_down.shape[0]
    nt = s // ROW_TILE
    nct = n_ctx_rows // ROW_TILE
    first_tiles = tuple(sorted({0, nct}))
    last_tiles = tuple(sorted({nct - 1, nt - 1} - {-1}))
    ctx_row = b
    full = lambda shape: pl.BlockSpec(shape, lambda bi, i: (0,) * len(shape))
    return pl.pallas_call(
        functools.partial(_ffn_kernel, first_tiles=first_tiles, last_tiles=last_tiles),
        grid=(b, nt),
        in_specs=_halo_specs(2 * d_ff, 0, 0, s) + [
            full((3, 2 * d_ff)), full((1, 2 * d_ff)),
            pl.BlockSpec((1, ROW_TILE, d), lambda bi, i: (bi, i, 0)),
            pl.BlockSpec((1, 6, d), lambda bi, i: (jnp.where(i < nct, ctx_row, bi), 0, 0)),
            full((1, d)), full((d_ff, d))],
        out_specs=pl.BlockSpec((1, ROW_TILE, d), lambda bi, i: (bi, i, 0)),
        out_shape=jax.ShapeDtypeStruct((b, s, d), F32),
        compiler_params=_cparams(2, 48),
        name="ffn_tail",
    )(pu, pu, pu, conv_w, conv_b.reshape(1, 2 * d_ff), xs, mods, g_post.reshape(1, d), w_down)


def _deinterleave():
    return np.concatenate([np.arange(0, AT_DIM, 2), np.arange(1, AT_DIM, 2)])


def _q_head_order():
    return [h for j in range(AT_GROUP) for h in (j, AT_GROUP + j)]


def _largest_tile(n, cap):
    best = LANE
    for t in range(LANE, cap + 1, LANE):
        if n % t == 0:
            best = t
    return best


def kernel(x, c, ctx, c_ctx, w_ada, b_ada, g_pre_mix, g_post_mix, g_pre_ffn, g_post_ffn, w_in, hg_lower_bounds, hg_norm, q_norm, k_norm, hy_conv_w, hy_conv_b, hy_w1, hy_b1, hy_wi, hy_bi, hy_freq, hy_w_last, hy_bias, w_oa, w_ob, w_oc, w_out, w_up, ffn_conv_w, ffn_conv_b, w_down):
    bsz, n_lat, d = x.shape
    n_ctx = ctx.shape[1]
    depth = w_ada.shape[0]
    d_ff = w_down.shape[1]
    assert AT_KV_HEADS == 2 and AT_GROUP * LANE == AT_WIDTH and AT_KV_WIDTH == LANE
    assert n_ctx % ROW_TILE == 0 and n_lat % ROW_TILE == 0 and n_lat % GRID_W == 0
    assert (bsz * (n_ctx + n_lat)) % MM_ROWS == 0 and (bsz * n_lat) % MM_ROWS == 0

    lbp = jax.nn.softmax(hg_lower_bounds.astype(F32), axis=0)
    lower = jnp.cumsum(lbp, axis=0) - lbp[0]

    rp = -(-(bsz + 1) // 8) * 8
    src = jnp.concatenate([c, c_ctx[None, :], jnp.zeros((rp - bsz - 1, d), F32)], axis=0)
    mods_all = _ada(src, w_ada, b_ada).reshape(depth, rp, 6, d)

    o_q = 5 * HG_WIDTH
    o_k = o_q + AT_WIDTH
    o_v = o_k + AT_KV_WIDTH
    o_hy = o_v + AT_KV_WIDTH
    o_gate = o_hy + 3 * HY_WIDTH
    deint = _deinterleave()
    q_cols = np.concatenate([o_q + h * AT_DIM + deint for h in _q_head_order()])
    k_cols = np.concatenate([o_k + g * AT_DIM + deint for g in range(AT_KV_HEADS)])
    rest_cols = np.concatenate([np.arange(o_gate, o_gate + 3 * d), np.arange(o_hy, o_hy + 3 * HY_WIDTH),
                                q_cols, k_cols, np.arange(o_v, o_v + AT_KV_WIDTH)])
    col_hy = (3 * d) // (3 * HY_WIDTH)
    col_q = (3 * d + 3 * HY_WIDTH) // AT_WIDTH
    col_k = (3 * d + 3 * HY_WIDTH + AT_WIDTH) // AT_KV_WIDTH
    col_v = col_k + 1
    assert (3 * d) % (3 * HY_WIDTH) == 0 and (3 * d + 3 * HY_WIDTH) % AT_WIDTH == 0
    ob_rows = np.concatenate([np.arange(h * AT_DIM, (h + 1) * AT_DIM) for h in _q_head_order()])

    rope_tabs = _rope_tables(n_ctx, n_lat)
    dft_lat = _dft_tables(n_lat)
    dft_ctx = _dft_tables(n_ctx)
    nct = n_ctx // ROW_TILE

    xs = jnp.concatenate([ctx, x], axis=1)
    s_all = n_ctx + n_lat
    for l in range(depth):
        need_ctx = l < depth - 1
        mods = mods_all[l]
        w_hg = w_in[l][:, :5 * HG_WIDTH].astype(BF16)
        w_rest = w_in[l][:, rest_cols].astype(BF16)

        h = _modulate(xs, mods, g_pre_mix[l], 0, 1, nct).reshape(bsz * s_all, d)
        p_hg = _matmul(h, w_hg, F32, _largest_tile(5 * HG_WIDTH, 1280), "proj_hgrn").reshape(bsz, s_all, -1)
        p_rest = _matmul(h, w_rest, BF16, _largest_tile(w_rest.shape[1], 1792), "proj_rest").reshape(bsz, s_all, -1)

        o_f, o_b = _hgrn(p_hg, lower[l], n_ctx)

        gq = jnp.tile(q_norm[l][deint], AT_HEADS)[None, :]
        gk = jnp.tile(k_norm[l][deint], AT_KV_HEADS)[None, :]
        row_off = 0 if need_ctx else nct
        att = _attention(p_rest, col_q, col_k, col_v, rope_tabs, gq, gk, row_off, n_ctx)

        filt_args = (hy_w1[l], hy_b1[l], hy_wi[l], hy_bi[l], hy_freq[l], hy_w_last[l])
        coef = _hyena_coef(dft_lat, _hyena_filter_sums(n_lat, *filt_args))
        c_x = _hyena_conv(*_hyena_pre(p_rest, col_hy, nct, n_lat, hy_conv_w[l], hy_conv_b[l], hy_bias[l]),
                          dft_lat, coef)
        c_c = None
        if need_ctx:
            coef_c = _hyena_coef(dft_ctx, _hyena_filter_sums(n_ctx, *filt_args))
            c_c = _hyena_conv(*_hyena_pre(p_rest, col_hy, 0, n_ctx, hy_conv_w[l], hy_conv_b[l], hy_bias[l]),
                              dft_ctx, coef_c)

        xs = _merge(o_f, o_b, p_hg, att, c_x, c_c, p_rest, xs, mods, hg_norm[l], g_post_mix[l],
                    w_oa[l].astype(BF16), w_ob[l][ob_rows].astype(BF16), w_oc[l].astype(BF16),
                    w_out[l].astype(BF16), row_off, n_ctx)
        n_ctx_now = n_ctx if need_ctx else 0
        s_now = xs.shape[1]
        h2 = _modulate(xs, mods, g_pre_ffn[l], 3, 4, n_ctx_now // ROW_TILE).reshape(bsz * s_now, d)
        pu = _matmul(h2, w_up[l].astype(BF16), BF16, _largest_tile(2 * d_ff, 1408), "ffn_up").reshape(bsz, s_now, -1)
        xs = _ffn_tail(pu, xs, mods, ffn_conv_w[l], ffn_conv_b[l], g_post_ffn[l],
                       w_down[l].astype(BF16), n_ctx_now)
    return xs
```

```python
import functools
import math

import jax
import jax.numpy as jnp
import numpy as np
from jax import lax
from jax.experimental import pallas as pl
from jax.experimental.pallas import tpu as pltpu

F32 = jnp.float32
BF16 = jnp.bfloat16

NORM_EPS = 1e-6
GRID_W = 64
HG_HEADS = 4
HG_DIM = 128
HG_WIDTH = HG_HEADS * HG_DIM
HG_EXP_CLIP = 30.0
AT_HEADS = 8
AT_KV_HEADS = 2
AT_DIM = 64
AT_GROUP = AT_HEADS // AT_KV_HEADS
AT_WIDTH = AT_HEADS * AT_DIM
AT_KV_WIDTH = AT_KV_HEADS * AT_DIM
ROPE_THETA = 10000.0
HY_WIDTH = 512
HY_EMB_DIM = 33
HY_BANDS = (HY_EMB_DIM - 1) // 2
HY_FILTER_WIDTH = 64
HY_INNER = 2
HY_FAST_DECAY = 0.3
HY_SLOW_DECAY = 1.5
HY_TARGET = 1e-2

LANE = 128
BF16_SUBLANES = 16
ROW_TILE = 256
HG_CHUNK = 128
HG_LEVELS = tuple(HG_CHUNK >> (j + 1) for j in range(int(math.log2(HG_CHUNK))))
MM_ROWS = 1024
DFT_ROWS = 512
VMEM_CAP = 56 * 1024 * 1024


def _cparams(n_axes, vmem_mb):
    return pltpu.CompilerParams(
        dimension_semantics=("arbitrary",) * n_axes,
        vmem_limit_bytes=min(int(vmem_mb) * 1024 * 1024, VMEM_CAP))


def _dot(a, b):
    return jnp.dot(a, b, preferred_element_type=F32)


def _dot_nt(a, b):
    return lax.dot_general(a, b, (((1,), (1,)), ((), ())), preferred_element_type=F32)


def _split_bf16(a):
    hi = a.astype(BF16)
    lo = (a - hi.astype(F32)).astype(BF16)
    return hi, lo


def _dot3(a, b):
    ah, al = _split_bf16(a)
    bh, bl = _split_bf16(b)
    return _dot(ah, bh) + (_dot(ah, bl) + _dot(al, bh))


def _rms(x, g):
    return x * lax.rsqrt(jnp.mean(x * x, axis=-1, keepdims=True) + NORM_EPS) * g


def _sigmoid(x):
    return 1.0 / (1.0 + jnp.exp(-x))


def _ada_kernel(src_ref, w_ref, b_ref, o_ref):
    s = src_ref[...]
    s = s * _sigmoid(s)
    o_ref[0] = _dot3(s, w_ref[0]) + b_ref[0]


def _ada(src, w_ada, b_ada):
    depth, d, d6 = w_ada.shape
    rp = src.shape[0]
    tn = d
    return pl.pallas_call(
        _ada_kernel,
        grid=(depth, d6 // tn),
        in_specs=[pl.BlockSpec((rp, d), lambda l, j: (0, 0)),
                  pl.BlockSpec((1, d, tn), lambda l, j: (l, 0, j)),
                  pl.BlockSpec((1, 1, tn), lambda l, j: (l, 0, j))],
        out_specs=pl.BlockSpec((1, rp, tn), lambda l, j: (l, 0, j)),
        out_shape=jax.ShapeDtypeStruct((depth, rp, d6), F32),
        compiler_params=_cparams(2, 32),
        name="ada",
    )(src, w_ada, b_ada.reshape(depth, 1, d6))


def _mod_kernel(x_ref, m_ref, g_ref, o_ref, *, k_shift, k_scale):
    y = _rms(x_ref[0], g_ref[...])
    shift = m_ref[0, k_shift:k_shift + 1, :]
    scale = m_ref[0, k_scale:k_scale + 1, :]
    o_ref[0] = (y * (1.0 + scale) + shift).astype(o_ref.dtype)


def _modulate(xs, mods, g, k_shift, k_scale, n_ctx_tiles):
    b, s, d = xs.shape
    nt = s // ROW_TILE
    ctx_row = b

    def mrow(bi, i):
        return (jnp.where(i < n_ctx_tiles, ctx_row, bi), 0, 0)

    return pl.pallas_call(
        functools.partial(_mod_kernel, k_shift=k_shift, k_scale=k_scale),
        grid=(b, nt),
        in_specs=[pl.BlockSpec((1, ROW_TILE, d), lambda bi, i: (bi, i, 0)),
                  pl.BlockSpec((1, 6, d), mrow),
                  pl.BlockSpec((1, d), lambda bi, i: (0, 0))],
        out_specs=pl.BlockSpec((1, ROW_TILE, d), lambda bi, i: (bi, i, 0)),
        out_shape=jax.ShapeDtypeStruct((b, s, d), BF16),
        compiler_params=_cparams(2, 16),
        name="modulate",
    )(xs, mods, g.reshape(1, d))


def _mm_kernel(a_ref, b_ref, o_ref):
    o_ref[...] = _dot(a_ref[...], b_ref[...]).astype(o_ref.dtype)


def _matmul(a, w, out_dtype, tn, name):
    m, k = a.shape
    n = w.shape[1]
    tm = MM_ROWS
    assert m % tm == 0 and n % tn == 0
    return pl.pallas_call(
        _mm_kernel,
        grid=(m // tm, n // tn),
        in_specs=[pl.BlockSpec((tm, k), lambda i, j: (i, 0)),
                  pl.BlockSpec((k, tn), lambda i, j: (0, j))],
        out_specs=pl.BlockSpec((tm, tn), lambda i, j: (i, j)),
        out_shape=jax.ShapeDtypeStruct((m, n), out_dtype),
        compiler_params=_cparams(2, 48),
        name=name,
    )(a, w)


def _mm3_kernel(a_ref, b_ref, o_ref):
    o_ref[...] = _dot3(a_ref[...], b_ref[...])


def _matmul_f32(a, w, tm, name):
    m, k = a.shape
    n = w.shape[1]
    return pl.pallas_call(
        _mm3_kernel,
        grid=(m // tm,),
        in_specs=[pl.BlockSpec((tm, k), lambda i: (i, 0)),
                  pl.BlockSpec((k, n), lambda i: (0, 0))],
        out_specs=pl.BlockSpec((tm, n), lambda i: (i, 0)),
        out_shape=jax.ShapeDtypeStruct((m, n), F32),
        compiler_params=_cparams(1, 48),
        name=name,
    )(a, w)


def _hg_scan_matrix(reverse):
    t_n = HG_CHUNK
    t = np.arange(t_n)[:, None]
    u = np.arange(t_n)[None, :]
    rows = [(u >= t) if reverse else (u <= t)]
    for w in HG_LEVELS:
        base = (t // (2 * w)) * (2 * w)
        mid = base + w
        upper = (t - base) >= w
        if reverse:
            m = np.where(upper, (u >= mid) & (u < t), (u >= t) & (u < mid))
        else:
            m = np.where(upper, (u >= mid) & (u <= t), (u > t) & (u < mid))
        rows.append(m)
    return np.concatenate(rows, axis=0).astype(np.float32)


def _hgrn_kernel(qf_ref, zf_ref, vf_ref, qb_ref, zb_ref, vb_ref, lb_ref, pf_ref, pb_ref,
                 of_ref, ob_ref, s_ref):
    @pl.when(pl.program_id(1) == 0)
    def _():
        s_ref[...] = jnp.zeros_like(s_ref)

    t_n = HG_CHUNK
    ti = lax.broadcasted_iota(jnp.int32, (t_n, t_n), 0)
    si = lax.broadcasted_iota(jnp.int32, (t_n, t_n), 1)
    tx = ti ^ si
    dirs = ((qf_ref, zf_ref, vf_ref, pf_ref, of_ref), (qb_ref, zb_ref, vb_ref, pb_ref, ob_ref))
    masks = []
    for reverse in (False, True):
        later = (ti < si) if reverse else (ti > si)
        masks.append([later & (tx >= w) & (tx < 2 * w) for w in HG_LEVELS])
    units = [(d, h) for d in range(2) for h in range(HG_HEADS)]
    cols = lambda h: slice(h * HG_DIM, (h + 1) * HG_DIM)

    kk, cat = {}, {}
    for d, h in units:
        z = dirs[d][1][0, :, cols(h)]
        lb = lb_ref[d:d + 1, cols(h)]
        e = jnp.exp(-jnp.abs(z))
        ope = 1.0 + e
        log_sig = jnp.minimum(z, 0.0) - jnp.log(ope)
        e_clip = jnp.exp(jnp.minimum(-z, HG_EXP_CLIP))
        lf = log_sig + jnp.log(1.0 + lb * e_clip)
        kk[d, h] = (1.0 - lb) * (jnp.where(z >= 0.0, e, 1.0) / ope)
        hi, lo = _split_bf16(lf)
        cat[d, h] = jnp.concatenate([hi, lo], axis=1)

    x = {}
    for d, h in units:
        xx = _dot(dirs[d][3][...], cat[d, h])
        x[d, h] = xx[:, :HG_DIM] + xx[:, HG_DIM:]

    a = {u: jnp.zeros((t_n, t_n), F32) for u in units}
    for j in range(len(HG_LEVELS)):
        for d, h in units:
            q = dirs[d][0][0, :, cols(h)]
            ew = jnp.exp(x[d, h][(j + 1) * t_n:(j + 2) * t_n])
            pw = _dot_nt((q * ew).astype(BF16), (kk[d, h] * ew).astype(BF16))
            a[d, h] = jnp.where(masks[d][j], pw, a[d, h])

    for d, h in units:
        q = dirs[d][0][0, :, cols(h)]
        v = dirs[d][2][0, :, cols(h)]
        g = x[d, h][0:t_n]
        g_last = g[0:1] if d == 1 else g[t_n - 1:t_n]
        st = s_ref[d * HG_HEADS + h]
        dqk = jnp.sum(q * kk[d, h], axis=1, keepdims=True)
        o = (_dot(a[d, h].astype(BF16), v.astype(BF16)) + dqk * v
             + _dot_nt((q * jnp.exp(g)).astype(BF16), st.astype(BF16)))
        kd = (kk[d, h] * jnp.exp(g_last - g)).astype(BF16)
        dirs[d][4][0, :, cols(h)] = o
        s_ref[d * HG_HEADS + h] = st * jnp.exp(g_last) + _dot(v.T.astype(BF16), kd)


def _hgrn(p_hg, lb, n_ctx_rows):
    b, s, _ = p_hg.shape
    nb = s // HG_CHUNK
    nc = n_ctx_rows // HG_CHUNK

    def bidx(n):
        return jnp.where(n < nc, nc - 1 - n, nb - 1 - (n - nc))

    blk = (1, HG_CHUNK, HG_WIDTH)
    pf = jnp.asarray(_hg_scan_matrix(False)).astype(BF16)
    pb = jnp.asarray(_hg_scan_matrix(True)).astype(BF16)
    pshape = pf.shape
    return pl.pallas_call(
        _hgrn_kernel,
        grid=(b, nb),
        in_specs=[pl.BlockSpec(blk, lambda bi, n: (bi, n, 0)),
                  pl.BlockSpec(blk, lambda bi, n: (bi, n, 1)),
                  pl.BlockSpec(blk, lambda bi, n: (bi, n, 3)),
                  pl.BlockSpec(blk, lambda bi, n: (bi, bidx(n), 0)),
                  pl.BlockSpec(blk, lambda bi, n: (bi, bidx(n), 2)),
                  pl.BlockSpec(blk, lambda bi, n: (bi, bidx(n), 3)),
                  pl.BlockSpec((2, HG_WIDTH), lambda bi, n: (0, 0)),
                  pl.BlockSpec(pshape, lambda bi, n: (0, 0)),
                  pl.BlockSpec(pshape, lambda bi, n: (0, 0))],
        out_specs=[pl.BlockSpec(blk, lambda bi, n: (bi, n, 0)),
                   pl.BlockSpec(blk, lambda bi, n: (bi, bidx(n), 0))],
        out_shape=[jax.ShapeDtypeStruct((b, s, HG_WIDTH), F32)] * 2,
        scratch_shapes=[pltpu.VMEM((2 * HG_HEADS, HG_DIM, HG_DIM), F32)],
        compiler_params=_cparams(2, 40),
        name="hgrn",
    )(p_hg, p_hg, p_hg, p_hg, p_hg, p_hg, lb, pf, pb)


def _rope(x, cos, sin_signed, first_half):
    n = x.shape[-1]
    half = AT_DIM // 2
    partner = jnp.where(first_half, pltpu.roll(x, n - half, axis=1), pltpu.roll(x, half, axis=1))
    return x * cos + partner * sin_signed


def _head_norm(x, gain, group_mean):
    ms = _dot((x * x).astype(BF16), group_mean)
    return x * lax.rsqrt(ms + NORM_EPS) * gain


def _attn_kernel(q_ref, k_ref, v_ref, cq_ref, sq_ref, ck_ref, sk_ref, gq_ref, gk_ref, mq_ref, mk_ref,
                 o_ref, k_scr, v_scr, *, q_off, n_ctx_tiles, n_ctx_rows):
    i = pl.program_id(1)

    @pl.when(i == 0)
    def _():
        kr = k_ref[0].astype(F32)
        lane = lax.broadcasted_iota(jnp.int32, kr.shape, 1)
        kn = _head_norm(kr, gk_ref[...], mk_ref[...])
        k_scr[...] = _rope(kn, ck_ref[...], sk_ref[...], (lane % AT_DIM) < AT_DIM // 2).astype(BF16)
        v = v_ref[0]
        one = jnp.ones_like(v)
        v_scr[0] = jnp.where(lane < AT_DIM, v, one)
        v_scr[1] = jnp.where(lane < AT_DIM, one, v)

    qr = q_ref[0].astype(F32)
    lane = lax.broadcasted_iota(jnp.int32, qr.shape, 1)
    qn = _head_norm(qr, gq_ref[...], mq_ref[...])
    qn = _rope(qn, cq_ref[...], sq_ref[...], (lane % AT_DIM) < AT_DIM // 2)
    qn = qn * (AT_DIM ** -0.5 * math.log2(math.e))
    lane_t = lax.broadcasted_iota(jnp.int32, (ROW_TILE, LANE), 1)
    kv0 = lane_t < AT_DIM

    def attend(n_keys):
        keys = k_scr[0:n_keys, :]
        heads = [(j, g) for j in range(AT_WIDTH // LANE) for g in range(AT_KV_HEADS)]

        def scores(j, g):
            qt = qn[:, j * LANE:(j + 1) * LANE]
            return _dot_nt(jnp.where(kv0 if g == 0 else ~kv0, qt, 0.0).astype(BF16), keys)

        s_next = scores(*heads[0])
        outs = {}
        for n, (j, g) in enumerate(heads):
            s = s_next
            if n + 1 < len(heads):
                s_next = scores(*heads[n + 1])
            p = jnp.exp2(s - jnp.max(s, axis=1, keepdims=True))
            outs[g] = _dot(p.astype(BF16), v_scr[g, 0:n_keys, :])
            if g == AT_KV_HEADS - 1:
                num = jnp.where(kv0, outs[0], outs[1])
                den = pltpu.roll(jnp.where(kv0, outs[1], outs[0]), AT_DIM, axis=1)
                o_ref[0, :, j * LANE:(j + 1) * LANE] = (num / den).astype(o_ref.dtype)

    n_all = k_scr.shape[0]
    if q_off < n_ctx_tiles:
        @pl.when(i + q_off < n_ctx_tiles)
        def _():
            attend(n_ctx_rows)

        @pl.when(i + q_off >= n_ctx_tiles)
        def _():
            attend(n_all)
    else:
        attend(n_all)


def _attention(p_rest, col_q, col_k, col_v, tabs, gq, gk, q_off, n_ctx_rows):
    b, s, _ = p_rest.shape
    nt = s // ROW_TILE
    cq, sq, ck, sk = tabs
    mq = jnp.asarray(np.kron(np.eye(AT_HEADS), np.full((AT_DIM, AT_DIM), 1.0 / AT_DIM)), BF16)
    mk = jnp.asarray(np.kron(np.eye(AT_KV_HEADS), np.full((AT_DIM, AT_DIM), 1.0 / AT_DIM)), BF16)
    kern = functools.partial(_attn_kernel, q_off=q_off, n_ctx_tiles=n_ctx_rows // ROW_TILE,
                             n_ctx_rows=n_ctx_rows)
    return pl.pallas_call(
        kern,
        grid=(b, nt - q_off),
        in_specs=[pl.BlockSpec((1, ROW_TILE, AT_WIDTH), lambda bi, i: (bi, i + q_off, col_q)),
                  pl.BlockSpec((1, s, AT_KV_WIDTH), lambda bi, i: (bi, 0, col_k)),
                  pl.BlockSpec((1, s, AT_KV_WIDTH), lambda bi, i: (bi, 0, col_v)),
                  pl.BlockSpec((ROW_TILE, AT_WIDTH), lambda bi, i: (i + q_off, 0)),
                  pl.BlockSpec((ROW_TILE, AT_WIDTH), lambda bi, i: (i + q_off, 0)),
                  pl.BlockSpec((s, AT_KV_WIDTH), lambda bi, i: (0, 0)),
                  pl.BlockSpec((s, AT_KV_WIDTH), lambda bi, i: (0, 0)),
                  pl.BlockSpec((1, AT_WIDTH), lambda bi, i: (0, 0)),
                  pl.BlockSpec((1, AT_KV_WIDTH), lambda bi, i: (0, 0)),
                  pl.BlockSpec((AT_WIDTH, AT_WIDTH), lambda bi, i: (0, 0)),
                  pl.BlockSpec((AT_KV_WIDTH, AT_KV_WIDTH), lambda bi, i: (0, 0))],
        out_specs=pl.BlockSpec((1, ROW_TILE, AT_WIDTH), lambda bi, i: (bi, i + q_off, 0)),
        out_shape=jax.ShapeDtypeStruct((b, s, AT_WIDTH), BF16),
        scratch_shapes=[pltpu.VMEM((s, AT_KV_WIDTH), BF16),
                        pltpu.VMEM((AT_KV_HEADS, s, AT_KV_WIDTH), BF16)],
        compiler_params=_cparams(2, 48),
        name="attention",
    )(p_rest, p_rest, p_rest, cq, sq, ck, sk, gq, gk, mq, mk)


def _rope_tables(n_ctx_rows, n_lat_rows):
    rows = n_lat_rows // GRID_W
    row = jnp.repeat(jnp.arange(rows), GRID_W).astype(F32)
    col = jnp.tile(jnp.arange(GRID_W), rows).astype(F32)
    n_freq = AT_DIM // 4
    inv = ROPE_THETA ** (-jnp.arange(n_freq, dtype=F32) / n_freq)
    ang = jnp.concatenate([row[:, None] * inv, col[:, None] * inv], axis=-1)
    cos = jnp.concatenate([jnp.cos(ang), jnp.cos(ang)], axis=-1)
    sin = jnp.concatenate([-jnp.sin(ang), jnp.sin(ang)], axis=-1)
    cos = jnp.concatenate([jnp.ones((n_ctx_rows, AT_DIM), F32), cos], axis=0)
    sin = jnp.concatenate([jnp.zeros((n_ctx_rows, AT_DIM), F32), sin], axis=0)
    return (jnp.tile(cos, (1, AT_HEADS)), jnp.tile(sin, (1, AT_HEADS)),
            jnp.tile(cos, (1, AT_KV_HEADS)), jnp.tile(sin, (1, AT_KV_HEADS)))


def _shift_matrices(n):
    i = np.arange(n)
    down = i[:, None] - 1 == i[None, :]
    up = i[:, None] + 1 == i[None, :]
    return jnp.asarray(np.stack([down, up]).astype(np.float32), BF16)


def _conv3(xb, shift_ref, prev_row, next_row, w, bias):
    n, c = xb.shape
    sub = 8
    r = lax.broadcasted_iota(jnp.int32, (sub, c), 0)
    x = xb.astype(F32)
    if shift_ref is None:
        xm = pltpu.roll(x, 1, axis=0)
        xp = pltpu.roll(x, n - 1, axis=0)
    else:
        xm = _dot(shift_ref[0], xb)
        xp = _dot(shift_ref[1], xb)
    xm = jnp.concatenate([jnp.where(r == 0, prev_row, xm[0:sub]), xm[sub:]], axis=0)
    xp = jnp.concatenate([xp[:n - sub], jnp.where(r == sub - 1, next_row, xp[n - sub:])], axis=0)
    return xm * w[0:1] + x * w[1:2] + xp * w[2:3] + bias


def _halo_specs(width, col, row_off, n_rows):
    per = ROW_TILE // BF16_SUBLANES
    last = n_rows // BF16_SUBLANES - 1
    return [
        pl.BlockSpec((1, ROW_TILE, width), lambda bi, i: (bi, i + row_off, col)),
        pl.BlockSpec((1, BF16_SUBLANES, width),
                     lambda bi, i: (bi, jnp.maximum((i + row_off) * per - 1, 0), col)),
        pl.BlockSpec((1, BF16_SUBLANES, width),
                     lambda bi, i: (bi, jnp.minimum((i + row_off + 1) * per, last), col)),
    ]


def _halo_rows(prev_ref, next_ref, is_first, is_last, cols=None):
    sl = slice(None) if cols is None else cols
    prev_row = prev_ref[0, BF16_SUBLANES - 1:BF16_SUBLANES, sl].astype(F32)
    next_row = next_ref[0, 0:1, sl].astype(F32)
    prev_row = jnp.where(is_first, 0.0, prev_row)
    next_row = jnp.where(is_last, 0.0, next_row)
    return prev_row, next_row


def _hypre_kernel(z_ref, zp_ref, zn_ref, sh_ref, w_ref, b_ref, db_ref, u_ref, ud_ref, x0_ref):
    i = pl.program_id(1)
    prev_row, next_row = _halo_rows(zp_ref, zn_ref, i == 0, i == pl.num_programs(1) - 1)
    zc = _conv3(z_ref[0], sh_ref, prev_row, next_row, w_ref[...], b_ref[...])
    x0 = zc[:, :HY_WIDTH]
    x1 = zc[:, HY_WIDTH:2 * HY_WIDTH]
    v = zc[:, 2 * HY_WIDTH:]
    u = v * x1
    u_ref[0] = u.astype(BF16)
    ud_ref[0] = (u * db_ref[...]).astype(BF16)
    x0_ref[0] = x0.astype(BF16)


def _hyena_pre(p_rest, col, row_off, n_rows, conv_w, conv_b, d_bias):
    b, s, _ = p_rest.shape
    width = 3 * HY_WIDTH
    out = jax.ShapeDtypeStruct((b, n_rows, HY_WIDTH), BF16)
    ospec = pl.BlockSpec((1, ROW_TILE, HY_WIDTH), lambda bi, i: (bi, i, 0))
    return pl.pallas_call(
        _hypre_kernel,
        grid=(b, n_rows // ROW_TILE),
        in_specs=_halo_specs(width, col, row_off, s) + [
            pl.BlockSpec((2, ROW_TILE, ROW_TILE), lambda bi, i: (0, 0, 0)),
            pl.BlockSpec((3, width), lambda bi, i: (0, 0)),
            pl.BlockSpec((1, width), lambda bi, i: (0, 0)),
            pl.BlockSpec((1, HY_WIDTH), lambda bi, i: (0, 0))],
        out_specs=[ospec, ospec, ospec],
        out_shape=[out, out, out],
        compiler_params=_cparams(2, 32),
        name="hyena_pre",
    )(p_rest, p_rest, p_rest, _shift_matrices(ROW_TILE), conv_w, conv_b.reshape(1, width),
      d_bias.reshape(1, HY_WIDTH))


def _hyfilt_kernel(z_ref, t_ref, dl_ref, w1_ref, b1_ref, wi_ref, bi_ref, fr_ref, wl_ref, o_ref):
    fr = fr_ref[...]
    h = jnp.sin(fr * (_dot3(z_ref[...], w1_ref[...]) + b1_ref[...]))
    for j in range(HY_INNER):
        h = jnp.sin(fr * (_dot3(h, wi_ref[j]) + bi_ref[j]))
    h = _dot3(h, wl_ref[...])
    decay = jnp.exp(-t_ref[...] * dl_ref[...])
    hf = h[:, :HY_WIDTH] * decay
    hb = h[:, HY_WIDTH:] * decay
    o_ref[...] = jnp.concatenate([hf + hb, hf - hb], axis=1)


def _pad2(a, rows, cols):
    return jnp.pad(a, ((0, rows - a.shape[0]), (0, cols - a.shape[1])))


def _hyena_filter_sums(n, w1, b1, wi, bi, freq, w_last):
    t = jnp.linspace(0.0, 1.0, n, dtype=F32)[:, None]
    w = 2.0 * math.pi * jnp.arange(n, dtype=F32)[:, None] / n
    f = jnp.linspace(1e-4, HY_BANDS - 1, HY_BANDS, dtype=F32)[None, :]
    z = jnp.concatenate([t, jnp.cos(f * w), -jnp.sin(f * w)], axis=-1)
    max_decay = math.log(HY_TARGET) / HY_FAST_DECAY
    min_decay = math.log(HY_TARGET) / HY_SLOW_DECAY
    deltas = jnp.abs(jnp.linspace(min_decay, max_decay, HY_WIDTH, dtype=F32))[None, :]
    zp = _pad2(z, n, LANE)
    w1p = _pad2(w1, LANE, LANE)
    b1p = _pad2(b1[None, :], 1, LANE)
    wip = jnp.stack([_pad2(wi[j], LANE, LANE) for j in range(HY_INNER)])
    bip = jnp.stack([_pad2(bi[j][None, :], 1, LANE) for j in range(HY_INNER)])
    frp = _pad2(freq[None, :], 1, LANE)
    wlp = _pad2(w_last, LANE, 2 * HY_WIDTH)
    tr = min(n, ROW_TILE)
    full = lambda shape: pl.BlockSpec(shape, lambda i: (0,) * len(shape))
    return pl.pallas_call(
        _hyfilt_kernel,
        grid=(n // tr,),
        in_specs=[pl.BlockSpec((tr, LANE), lambda i: (i, 0)),
                  pl.BlockSpec((tr, 1), lambda i: (i, 0)),
                  full((1, HY_WIDTH)), full((LANE, LANE)), full((1, LANE)),
                  full((HY_INNER, LANE, LANE)), full((HY_INNER, 1, LANE)), full((1, LANE)),
                  full((LANE, 2 * HY_WIDTH))],
        out_specs=pl.BlockSpec((tr, 2 * HY_WIDTH), lambda i: (i, 0)),
        out_shape=jax.ShapeDtypeStruct((n, 2 * HY_WIDTH), F32),
        compiler_params=_cparams(1, 32),
        name="hyena_filter",
    )(zp, t, deltas, w1p, b1p, wip, bip, frp, wlp)


def _dft_tables(n):
    f = jnp.arange(n, dtype=jnp.int32)[:, None]
    t = jnp.arange(n, dtype=jnp.int32)[None, :]
    ang = ((f * t) % (2 * n)).astype(F32) * (math.pi / n)
    nyq = jnp.where(t % 2 == 0, 1.0, -1.0).astype(F32)
    return jnp.stack([jnp.cos(ang), jnp.where(f == 0, nyq, jnp.sin(ang))])


def _hyfwd_kernel(u_ref, f_ref, co_ref, o_ref):
    u = u_ref[0]
    ure = _dot(f_ref[0], u)
    uim = _dot(f_ref[1], u)
    o_ref[0, 0] = (ure * co_ref[0] - uim * co_ref[1]).astype(BF16)
    o_ref[0, 1] = (ure * co_ref[2] + uim * co_ref[3]).astype(BF16)


def _hyinv_kernel(y_ref, ft_ref, ud_ref, x0_ref, o_ref):
    y = _dot(ft_ref[0], y_ref[0, 0]) + _dot(ft_ref[1], y_ref[0, 1])
    o_ref[0] = ((y + ud_ref[0].astype(F32)) * x0_ref[0].astype(F32)).astype(BF16)


def _hyena_conv(u, ud, x0, tables, coef):
    b, n, c = u.shape
    tf = min(n, DFT_ROWS)
    f_bf = tables.astype(BF16)
    ft_bf = jnp.swapaxes(tables, 1, 2).astype(BF16)
    spec = pl.pallas_call(
        _hyfwd_kernel,
        grid=(n // tf, b),
        in_specs=[pl.BlockSpec((1, n, c), lambda j, bi: (bi, 0, 0)),
                  pl.BlockSpec((2, tf, n), lambda j, bi: (0, j, 0)),
                  pl.BlockSpec((4, tf, c), lambda j, bi: (0, j, 0))],
        out_specs=pl.BlockSpec((1, 2, tf, c), lambda j, bi: (bi, 0, j, 0)),
        out_shape=jax.ShapeDtypeStruct((b, 2, n, c), BF16),
        compiler_params=_cparams(2, 40),
        name="hyena_dft",
    )(u, f_bf, coef)
    return pl.pallas_call(
        _hyinv_kernel,
        grid=(b, n // tf),
        in_specs=[pl.BlockSpec((1, 2, n, c), lambda bi, j: (bi, 0, 0, 0)),
                  pl.BlockSpec((2, tf, n), lambda bi, j: (0, j, 0)),
                  pl.BlockSpec((1, tf, c), lambda bi, j: (bi, j, 0)),
                  pl.BlockSpec((1, tf, c), lambda bi, j: (bi, j, 0))],
        out_specs=pl.BlockSpec((1, tf, c), lambda bi, j: (bi, j, 0)),
        out_shape=jax.ShapeDtypeStruct((b, n, c), BF16),
        compiler_params=_cparams(2, 40),
        name="hyena_idft",
    )(spec, ft_bf, ud, x0)


def _hyena_coef(tables, hsum_hdiff):
    n = tables.shape[1]
    c = HY_WIDTH
    r = _matmul_f32(tables.reshape(2 * n, n), hsum_hdiff, min(n, ROW_TILE), "hyena_kernel_dft")
    k_re = r[:n, :c]
    k_im = r[n:, c:]
    k_nyq = r[n:n + 1, :c]
    first = (jnp.arange(n) == 0)[:, None]
    scale = jnp.where(first, 1.0 / (2 * n), 2.0 / (2 * n)).astype(F32)
    zero = jnp.zeros_like(k_im)
    return jnp.stack([k_re * scale,
                      jnp.where(first, zero, k_im * scale),
                      jnp.where(first, zero, k_im * scale),
                      jnp.where(first, k_nyq, k_re) * scale])


def _merge_kernel(*refs, n_ctx_tiles, row_off, has_ctx):
    if has_ctx:
        (of_ref, ob_ref, zg_ref, att_ref, cx_ref, cc_ref, gate_ref, x_ref, m_ref, ghg_ref, gpost_ref,
         gffn_ref, woa_ref, wob_ref, woc_ref, wout_ref, o_ref, h_ref) = refs
    else:
        (of_ref, ob_ref, zg_ref, att_ref, cx_ref, gate_ref, x_ref, m_ref, ghg_ref, gpost_ref,
         gffn_ref, woa_ref, wob_ref, woc_ref, wout_ref, o_ref, h_ref) = refs
    o = of_ref[0] + ob_ref[0]
    ghg = ghg_ref[...]
    a = jnp.concatenate([_rms(o[:, h * HG_DIM:(h + 1) * HG_DIM], ghg) for h in range(HG_HEADS)], axis=1)
    zg = zg_ref[0]
    a = a * (zg * _sigmoid(zg))
    c = cx_ref[0]
    if has_ctx:
        c = jnp.where(pl.program_id(1) + row_off < n_ctx_tiles, cc_ref[0], c)
    d = x_ref.shape[-1]
    ya = _dot(a.astype(BF16), woa_ref[...])
    yb = _dot(att_ref[0], wob_ref[...])
    yc = _dot(c, woc_ref[...])
    m = (_sigmoid(gate_ref[0, :, 0:d].astype(F32)) * ya
         + _sigmoid(gate_ref[0, :, d:2 * d].astype(F32)) * yb
         + _sigmoid(gate_ref[0, :, 2 * d:3 * d].astype(F32)) * yc)
    y = _dot(m.astype(BF16), wout_ref[...])
    x_new = x_ref[0] + m_ref[0, 2:3, :] * _rms(y, gpost_ref[...])
    o_ref[0] = x_new
    h_ref[0] = (_rms(x_new, gffn_ref[...]) * (1.0 + m_ref[0, 4:5, :]) + m_ref[0, 3:4, :]).astype(BF16)


def _merge(o_f, o_b, p_hg, att, c_x, c_c, p_rest, xs, mods, g_hg, g_post, g_ffn, w_oa, w_ob, w_oc, w_out,
           row_off, n_ctx_rows):
    b, s, d = xs.shape
    nct = n_ctx_rows // ROW_TILE
    n_tiles = s // ROW_TILE - row_off
    has_ctx = c_c is not None
    ctx_row = b

    def stream(width, col=0):
        return pl.BlockSpec((1, ROW_TILE, width), lambda bi, i: (bi, i + row_off, col))

    def full(shape):
        return pl.BlockSpec(shape, lambda bi, i: (0,) * len(shape))

    in_specs = [stream(HG_WIDTH), stream(HG_WIDTH), stream(HG_WIDTH, 4), stream(AT_WIDTH),
                pl.BlockSpec((1, ROW_TILE, HY_WIDTH),
                             lambda bi, i: (bi, jnp.maximum(i + row_off - nct, 0), 0))]
    args = [o_f, o_b, p_hg, att, c_x]
    if has_ctx:
        in_specs.append(pl.BlockSpec((1, ROW_TILE, HY_WIDTH),
                                     lambda bi, i: (bi, jnp.minimum(i + row_off, nct - 1), 0)))
        args.append(c_c)
    in_specs += [stream(3 * d), stream(d),
                 pl.BlockSpec((1, 6, d), lambda bi, i: (jnp.where(i + row_off < nct, ctx_row, bi), 0, 0)),
                 full((1, HG_DIM)), full((1, d)), full((1, d)),
                 full((HG_WIDTH, d)), full((AT_WIDTH, d)), full((HY_WIDTH, d)), full((d, d))]
    args += [p_rest, xs, mods, g_hg.reshape(1, HG_DIM), g_post.reshape(1, d), g_ffn.reshape(1, d),
             w_oa, w_ob, w_oc, w_out]
    ospec = pl.BlockSpec((1, ROW_TILE, d), lambda bi, i: (bi, i, 0))
    return pl.pallas_call(
        functools.partial(_merge_kernel, n_ctx_tiles=nct, row_off=row_off, has_ctx=has_ctx),
        grid=(b, n_tiles),
        in_specs=in_specs,
        out_specs=[ospec, ospec],
        out_shape=[jax.ShapeDtypeStruct((b, n_tiles * ROW_TILE, d), F32),
                   jax.ShapeDtypeStruct((b, n_tiles * ROW_TILE, d), BF16)],
        compiler_params=_cparams(2, 48),
        name="merge",
    )(*args)


FFN_COLS = 256


def _ffn_kernel(*refs, first_tiles, last_tiles, has_next):
    if has_next:
        (u_ref, up_ref, un_ref, w_ref, b_ref, x_ref, m_ref, g_ref, wd_ref, mn_ref, gn_ref,
         o_ref, h_ref) = refs
    else:
        u_ref, up_ref, un_ref, w_ref, b_ref, x_ref, m_ref, g_ref, wd_ref, o_ref = refs
    i = pl.program_id(1)
    is_first = functools.reduce(jnp.logical_or, [i == t for t in first_tiles])
    is_last = functools.reduce(jnp.logical_or, [i == t for t in last_tiles])
    d_ff = wd_ref.shape[0]
    acc = jnp.zeros(o_ref.shape[1:], F32)
    for j in range(d_ff // FFN_COLS):
        halves = []
        for base in (0, d_ff):
            cols = slice(base + j * FFN_COLS, base + (j + 1) * FFN_COLS)
            prev_row, next_row = _halo_rows(up_ref, un_ref, is_first, is_last, cols)
            halves.append(_conv3(u_ref[0, :, cols], None, prev_row, next_row,
                                 w_ref[:, cols], b_ref[:, cols]))
        act = halves[0] * _sigmoid(halves[0]) * halves[1]
        acc = acc + _dot(act.astype(BF16), wd_ref[j * FFN_COLS:(j + 1) * FFN_COLS, :])
    x_new = x_ref[0] + m_ref[0, 5:6, :] * _rms(acc, g_ref[...])
    o_ref[0] = x_new
    if has_next:
        h_ref[0] = (_rms(x_new, gn_ref[...]) * (1.0 + mn_ref[0, 1:2, :]) + mn_ref[0, 0:1, :]).astype(BF16)


def _ffn_tail(pu, xs, mods, conv_w, conv_b, g_post, w_down, n_ctx_rows, mods_next=None, g_next=None):
    b, s, d = xs.shape
    d_ff = w_down.shape[0]
    nt = s // ROW_TILE
    nct = n_ctx_rows // ROW_TILE
    first_tiles = tuple(sorted({0, nct}))
    last_tiles = tuple(sorted({nct - 1, nt - 1} - {-1}))
    ctx_row = b
    has_next = mods_next is not None
    full = lambda shape: pl.BlockSpec(shape, lambda bi, i: (0,) * len(shape))
    mspec = pl.BlockSpec((1, 6, d), lambda bi, i: (jnp.where(i < nct, ctx_row, bi), 0, 0))
    ospec = pl.BlockSpec((1, ROW_TILE, d), lambda bi, i: (bi, i, 0))
    in_specs = _halo_specs(2 * d_ff, 0, 0, s) + [
        full((3, 2 * d_ff)), full((1, 2 * d_ff)),
        ospec, mspec, full((1, d)), full((d_ff, d))]
    args = [pu, pu, pu, conv_w, conv_b.reshape(1, 2 * d_ff), xs, mods, g_post.reshape(1, d), w_down]
    out_specs = [ospec]
    out_shape = [jax.ShapeDtypeStruct((b, s, d), F32)]
    if has_next:
        in_specs += [mspec, full((1, d))]
        args += [mods_next, g_next.reshape(1, d)]
        out_specs.append(ospec)
        out_shape.append(jax.ShapeDtypeStruct((b, s, d), BF16))
    return pl.pallas_call(
        functools.partial(_ffn_kernel, first_tiles=first_tiles, last_tiles=last_tiles, has_next=has_next),
        grid=(b, nt),
        in_specs=in_specs,
        out_specs=out_specs,
        out_shape=out_shape,
        compiler_params=_cparams(2, 48),
        name="ffn_tail",
    )(*args)


def _deinterleave():
    return np.concatenate([np.arange(0, AT_DIM, 2), np.arange(1, AT_DIM, 2)])


def _q_head_order():
    return [h for j in range(AT_GROUP) for h in (j, AT_GROUP + j)]


def _largest_tile(n, cap):
    best = LANE
    for t in range(LANE, cap + 1, LANE):
        if n % t == 0:
            best = t
    return best


def kernel(x, c, ctx, c_ctx, w_ada, b_ada, g_pre_mix, g_post_mix, g_pre_ffn, g_post_ffn, w_in, hg_lower_bounds, hg_norm, q_norm, k_norm, hy_conv_w, hy_conv_b, hy_w1, hy_b1, hy_wi, hy_bi, hy_freq, hy_w_last, hy_bias, w_oa, w_ob, w_oc, w_out, w_up, ffn_conv_w, ffn_conv_b, w_down):
    bsz, n_lat, d = x.shape
    n_ctx = ctx.shape[1]
    depth = w_ada.shape[0]
    d_ff = w_down.shape[1]
    assert AT_KV_HEADS == 2 and AT_GROUP * LANE == AT_WIDTH and AT_KV_WIDTH == LANE
    assert n_ctx % ROW_TILE == 0 and n_lat % ROW_TILE == 0 and n_lat % GRID_W == 0
    assert (bsz * (n_ctx + n_lat)) % MM_ROWS == 0 and (bsz * n_lat) % MM_ROWS == 0

    lbp = jax.nn.softmax(hg_lower_bounds.astype(F32), axis=0)
    lower = jnp.cumsum(lbp, axis=0) - lbp[0]

    rp = -(-(bsz + 1) // 8) * 8
    src = jnp.concatenate([c, c_ctx[None, :], jnp.zeros((rp - bsz - 1, d), F32)], axis=0)
    mods_all = _ada(src, w_ada, b_ada).reshape(depth, rp, 6, d)

    o_q = 5 * HG_WIDTH
    o_k = o_q + AT_WIDTH
    o_v = o_k + AT_KV_WIDTH
    o_hy = o_v + AT_KV_WIDTH
    o_gate = o_hy + 3 * HY_WIDTH
    deint = _deinterleave()
    q_cols = np.concatenate([o_q + h * AT_DIM + deint for h in _q_head_order()])
    k_cols = np.concatenate([o_k + g * AT_DIM + deint for g in range(AT_KV_HEADS)])
    rest_cols = np.concatenate([np.arange(o_gate, o_gate + 3 * d), np.arange(o_hy, o_hy + 3 * HY_WIDTH),
                                q_cols, k_cols, np.arange(o_v, o_v + AT_KV_WIDTH)])
    col_hy = (3 * d) // (3 * HY_WIDTH)
    col_q = (3 * d + 3 * HY_WIDTH) // AT_WIDTH
    col_k = (3 * d + 3 * HY_WIDTH + AT_WIDTH) // AT_KV_WIDTH
    col_v = col_k + 1
    assert (3 * d) % (3 * HY_WIDTH) == 0 and (3 * d + 3 * HY_WIDTH) % AT_WIDTH == 0
    ob_rows = np.concatenate([np.arange(h * AT_DIM, (h + 1) * AT_DIM) for h in _q_head_order()])

    rope_tabs = _rope_tables(n_ctx, n_lat)
    dft_lat = _dft_tables(n_lat)
    dft_ctx = _dft_tables(n_ctx)
    nct = n_ctx // ROW_TILE

    xs = jnp.concatenate([ctx, x], axis=1)
    s_all = n_ctx + n_lat
    h = _modulate(xs, mods_all[0], g_pre_mix[0], 0, 1, nct)
    for l in range(depth):
        need_ctx = l < depth - 1
        mods = mods_all[l]
        w_hg = w_in[l][:, :5 * HG_WIDTH].astype(BF16)
        w_rest = w_in[l][:, rest_cols].astype(BF16)

        h = h.reshape(bsz * s_all, d)
        p_hg = _matmul(h, w_hg, F32, _largest_tile(5 * HG_WIDTH, 1280), "proj_hgrn").reshape(bsz, s_all, -1)
        p_rest = _matmul(h, w_rest, BF16, _largest_tile(w_rest.shape[1], 1792), "proj_rest").reshape(bsz, s_all, -1)

        o_f, o_b = _hgrn(p_hg, lower[l], n_ctx)

        gq = jnp.tile(q_norm[l][deint], AT_HEADS)[None, :]
        gk = jnp.tile(k_norm[l][deint], AT_KV_HEADS)[None, :]
        row_off = 0 if need_ctx else nct
        att = _attention(p_rest, col_q, col_k, col_v, rope_tabs, gq, gk, row_off, n_ctx)

        filt_args = (hy_w1[l], hy_b1[l], hy_wi[l], hy_bi[l], hy_freq[l], hy_w_last[l])
        coef = _hyena_coef(dft_lat, _hyena_filter_sums(n_lat, *filt_args))
        c_x = _hyena_conv(*_hyena_pre(p_rest, col_hy, nct, n_lat, hy_conv_w[l], hy_conv_b[l], hy_bias[l]),
                          dft_lat, coef)
        c_c = None
        if need_ctx:
            coef_c = _hyena_coef(dft_ctx, _hyena_filter_sums(n_ctx, *filt_args))
            c_c = _hyena_conv(*_hyena_pre(p_rest, col_hy, 0, n_ctx, hy_conv_w[l], hy_conv_b[l], hy_bias[l]),
                              dft_ctx, coef_c)

        xs, h2 = _merge(o_f, o_b, p_hg, att, c_x, c_c, p_rest, xs, mods, hg_norm[l], g_post_mix[l],
                        g_pre_ffn[l], w_oa[l].astype(BF16), w_ob[l][ob_rows].astype(BF16),
                        w_oc[l].astype(BF16), w_out[l].astype(BF16), row_off, n_ctx)
        n_ctx_now = n_ctx if need_ctx else 0
        s_now = xs.shape[1]
        pu = _matmul(h2.reshape(bsz * s_now, d), w_up[l].astype(BF16), BF16,
                     _largest_tile(2 * d_ff, 1408), "ffn_up").reshape(bsz, s_now, -1)
        if need_ctx:
            xs, h = _ffn_tail(pu, xs, mods, ffn_conv_w[l], ffn_conv_b[l], g_post_ffn[l],
                              w_down[l].astype(BF16), n_ctx_now, mods_all[l + 1], g_pre_mix[l + 1])
        else:
            xs, = _ffn_tail(pu, xs, mods, ffn_conv_w[l], ffn_conv_b[l], g_post_ffn[l],
                            w_down[l].astype(BF16), n_ctx_now)
    return xs
```

```python
import functools
import math

import jax
import jax.numpy as jnp
import numpy as np
from jax import lax
from jax.experimental import pallas as pl
from jax.experimental.pallas import tpu as pltpu

F32 = jnp.float32
BF16 = jnp.bfloat16

NORM_EPS = 1e-6
GRID_W = 64
HG_HEADS = 4
HG_DIM = 128
HG_WIDTH = HG_HEADS * HG_DIM
HG_EXP_CLIP = 30.0
AT_HEADS = 8
AT_KV_HEADS = 2
AT_DIM = 64
AT_GROUP = AT_HEADS // AT_KV_HEADS
AT_WIDTH = AT_HEADS * AT_DIM
AT_KV_WIDTH = AT_KV_HEADS * AT_DIM
ROPE_THETA = 10000.0
HY_WIDTH = 512
HY_EMB_DIM = 33
HY_BANDS = (HY_EMB_DIM - 1) // 2
HY_FILTER_WIDTH = 64
HY_INNER = 2
HY_FAST_DECAY = 0.3
HY_SLOW_DECAY = 1.5
HY_TARGET = 1e-2

LANE = 128
BF16_SUBLANES = 16
ROW_TILE = 256
HG_CHUNK = 128
HG_LEVELS = tuple(HG_CHUNK >> (j + 1) for j in range(int(math.log2(HG_CHUNK))))
MM_ROWS = 1024
DFT_ROWS = 512
FFN_ROWS = 512
VMEM_CAP = 56 * 1024 * 1024


def _cparams(n_axes, vmem_mb):
    return pltpu.CompilerParams(
        dimension_semantics=("arbitrary",) * n_axes,
        vmem_limit_bytes=min(int(vmem_mb) * 1024 * 1024, VMEM_CAP))


def _dot(a, b):
    return jnp.dot(a, b, preferred_element_type=F32)


def _dot_nt(a, b):
    return lax.dot_general(a, b, (((1,), (1,)), ((), ())), preferred_element_type=F32)


def _split_bf16(a):
    hi = a.astype(BF16)
    lo = (a - hi.astype(F32)).astype(BF16)
    return hi, lo


def _dot3(a, b):
    ah, al = _split_bf16(a)
    bh, bl = _split_bf16(b)
    return _dot(ah, bh) + (_dot(ah, bl) + _dot(al, bh))


def _rms(x, g):
    return x * lax.rsqrt(jnp.mean(x * x, axis=-1, keepdims=True) + NORM_EPS) * g


def _sigmoid(x):
    return 0.5 * jnp.tanh(0.5 * x) + 0.5


def _ada_kernel(src_ref, w_ref, b_ref, o_ref):
    s = src_ref[...]
    s = s * _sigmoid(s)
    o_ref[0] = _dot3(s, w_ref[0]) + b_ref[0]


def _ada(src, w_ada, b_ada):
    depth, d, d6 = w_ada.shape
    rp = src.shape[0]
    tn = d
    return pl.pallas_call(
        _ada_kernel,
        grid=(depth, d6 // tn),
        in_specs=[pl.BlockSpec((rp, d), lambda l, j: (0, 0)),
                  pl.BlockSpec((1, d, tn), lambda l, j: (l, 0, j)),
                  pl.BlockSpec((1, 1, tn), lambda l, j: (l, 0, j))],
        out_specs=pl.BlockSpec((1, rp, tn), lambda l, j: (l, 0, j)),
        out_shape=jax.ShapeDtypeStruct((depth, rp, d6), F32),
        compiler_params=_cparams(2, 32),
        name="ada",
    )(src, w_ada, b_ada.reshape(depth, 1, d6))


def _mod_kernel(x_ref, m_ref, g_ref, o_ref, *, k_shift, k_scale):
    y = _rms(x_ref[0], g_ref[...])
    shift = m_ref[0, k_shift:k_shift + 1, :]
    scale = m_ref[0, k_scale:k_scale + 1, :]
    o_ref[0] = (y * (1.0 + scale) + shift).astype(o_ref.dtype)


def _modulate(xs, mods, g, k_shift, k_scale, n_ctx_tiles):
    b, s, d = xs.shape
    nt = s // ROW_TILE
    ctx_row = b

    def mrow(bi, i):
        return (jnp.where(i < n_ctx_tiles, ctx_row, bi), 0, 0)

    return pl.pallas_call(
        functools.partial(_mod_kernel, k_shift=k_shift, k_scale=k_scale),
        grid=(b, nt),
        in_specs=[pl.BlockSpec((1, ROW_TILE, d), lambda bi, i: (bi, i, 0)),
                  pl.BlockSpec((1, 6, d), mrow),
                  pl.BlockSpec((1, d), lambda bi, i: (0, 0))],
        out_specs=pl.BlockSpec((1, ROW_TILE, d), lambda bi, i: (bi, i, 0)),
        out_shape=jax.ShapeDtypeStruct((b, s, d), BF16),
        compiler_params=_cparams(2, 16),
        name="modulate",
    )(xs, mods, g.reshape(1, d))


def _mm_kernel(a_ref, b_ref, o_ref):
    o_ref[...] = _dot(a_ref[...], b_ref[...]).astype(o_ref.dtype)


def _matmul(a, w, out_dtype, tn, name):
    m, k = a.shape
    n = w.shape[1]
    tm = MM_ROWS
    assert m % tm == 0 and n % tn == 0
    return pl.pallas_call(
        _mm_kernel,
        grid=(m // tm, n // tn),
        in_specs=[pl.BlockSpec((tm, k), lambda i, j: (i, 0)),
                  pl.BlockSpec((k, tn), lambda i, j: (0, j))],
        out_specs=pl.BlockSpec((tm, tn), lambda i, j: (i, j)),
        out_shape=jax.ShapeDtypeStruct((m, n), out_dtype),
        compiler_params=_cparams(2, 48),
        name=name,
    )(a, w)


def _mm3_kernel(a_ref, b_ref, o_ref):
    o_ref[...] = _dot3(a_ref[...], b_ref[...])


def _matmul_f32(a, w, tm, name):
    m, k = a.shape
    n = w.shape[1]
    return pl.pallas_call(
        _mm3_kernel,
        grid=(m // tm,),
        in_specs=[pl.BlockSpec((tm, k), lambda i: (i, 0)),
                  pl.BlockSpec((k, n), lambda i: (0, 0))],
        out_specs=pl.BlockSpec((tm, n), lambda i: (i, 0)),
        out_shape=jax.ShapeDtypeStruct((m, n), F32),
        compiler_params=_cparams(1, 48),
        name=name,
    )(a, w)


def _hg_scan_matrix(reverse):
    t_n = HG_CHUNK
    t = np.arange(t_n)[:, None]
    u = np.arange(t_n)[None, :]
    rows = [(u >= t) if reverse else (u <= t)]
    for w in HG_LEVELS:
        base = (t // (2 * w)) * (2 * w)
        mid = base + w
        upper = (t - base) >= w
        if reverse:
            m = np.where(upper, (u >= mid) & (u < t), (u >= t) & (u < mid))
        else:
            m = np.where(upper, (u >= mid) & (u <= t), (u > t) & (u < mid))
        rows.append(m)
    m = np.concatenate(rows, axis=0).astype(np.float32)
    return np.concatenate([m, m], axis=1)


def _hgrn_kernel(qf_ref, zf_ref, vf_ref, qb_ref, zb_ref, vb_ref, lb_ref, pf_ref, pb_ref,
                 of_ref, ob_ref, s_ref):
    @pl.when(pl.program_id(1) == 0)
    def _():
        s_ref[...] = jnp.zeros_like(s_ref)

    t_n = HG_CHUNK
    ti = lax.broadcasted_iota(jnp.int32, (t_n, t_n), 0)
    si = lax.broadcasted_iota(jnp.int32, (t_n, t_n), 1)
    tx = ti ^ si
    dirs = ((qf_ref, zf_ref, vf_ref, pf_ref, of_ref), (qb_ref, zb_ref, vb_ref, pb_ref, ob_ref))
    masks = []
    for reverse in (False, True):
        later = (ti < si) if reverse else (ti > si)
        masks.append([later & (tx >= w) & (tx < 2 * w) for w in HG_LEVELS])
    units = [(d, h) for d in range(2) for h in range(HG_HEADS)]
    cols = lambda h: slice(h * HG_DIM, (h + 1) * HG_DIM)

    kk, kb, qb, cat, x = {}, {}, {}, {}, {}
    a = {u: jnp.zeros((t_n, t_n), F32) for u in units}

    def gates(d, h):
        z = dirs[d][1][0, :, cols(h)]
        lb = lb_ref[d:d + 1, cols(h)]
        e = jnp.exp(-jnp.abs(z))
        ope = 1.0 + e
        log_sig = jnp.minimum(z, 0.0) - jnp.log(ope)
        e_clip = jnp.exp(jnp.minimum(-z, HG_EXP_CLIP))
        lf = log_sig + jnp.log(1.0 + lb * e_clip)
        kk[d, h] = (1.0 - lb) * (jnp.where(z >= 0.0, e, 1.0) / ope)
        kb[d, h] = kk[d, h].astype(BF16)
        qb[d, h] = dirs[d][0][0, :, cols(h)].astype(BF16)
        hi, lo = _split_bf16(lf)
        cat[d, h] = jnp.concatenate([hi, lo], axis=0)

    def exponents(d):
        xd = _dot(dirs[d][3][...], jnp.concatenate([cat[d, h] for h in range(HG_HEADS)], axis=1))
        for h in range(HG_HEADS):
            x[d, h] = xd[:, cols(h)]

    def level(d, j):
        for h in range(HG_HEADS):
            ew = jnp.exp(x[d, h][(j + 1) * t_n:(j + 2) * t_n]).astype(BF16)
            pw = _dot_nt(qb[d, h] * ew, kb[d, h] * ew)
            a[d, h] = jnp.where(masks[d][j], pw, a[d, h])

    def finish(d, h):
        q = dirs[d][0][0, :, cols(h)]
        v = dirs[d][2][0, :, cols(h)]
        g = x[d, h][0:t_n]
        g_last = g[0:1] if d == 1 else g[t_n - 1:t_n]
        st = s_ref[d * HG_HEADS + h]
        dqk = jnp.sum(q * kk[d, h], axis=1, keepdims=True)
        o = (_dot(a[d, h].astype(BF16), v.astype(BF16)) + dqk * v
             + _dot_nt((q * jnp.exp(g)).astype(BF16), st.astype(BF16)))
        kd = (kk[d, h] * jnp.exp(g_last - g)).astype(BF16)
        dirs[d][4][0, :, cols(h)] = o
        s_ref[d * HG_HEADS + h] = st * jnp.exp(g_last) + _dot(v.T.astype(BF16), kd)

    for d in range(2):
        for h in range(HG_HEADS):
            gates(d, h)
        exponents(d)
    for j in range(len(HG_LEVELS)):
        for d in range(2):
            level(d, j)
    for d, h in units:
        finish(d, h)


def _hgrn(p_hg, lb, n_ctx_rows):
    b, s, _ = p_hg.shape
    nb = s // HG_CHUNK
    nc = n_ctx_rows // HG_CHUNK

    def bidx(n):
        return jnp.where(n < nc, nc - 1 - n, nb - 1 - (n - nc))

    blk = (1, HG_CHUNK, HG_WIDTH)
    pf = jnp.asarray(_hg_scan_matrix(False)).astype(BF16)
    pb = jnp.asarray(_hg_scan_matrix(True)).astype(BF16)
    pshape = pf.shape
    return pl.pallas_call(
        _hgrn_kernel,
        grid=(b, nb),
        in_specs=[pl.BlockSpec(blk, lambda bi, n: (bi, n, 0)),
                  pl.BlockSpec(blk, lambda bi, n: (bi, n, 1)),
                  pl.BlockSpec(blk, lambda bi, n: (bi, n, 3)),
                  pl.BlockSpec(blk, lambda bi, n: (bi, bidx(n), 0)),
                  pl.BlockSpec(blk, lambda bi, n: (bi, bidx(n), 2)),
                  pl.BlockSpec(blk, lambda bi, n: (bi, bidx(n), 3)),
                  pl.BlockSpec((2, HG_WIDTH), lambda bi, n: (0, 0)),
                  pl.BlockSpec(pshape, lambda bi, n: (0, 0)),
                  pl.BlockSpec(pshape, lambda bi, n: (0, 0))],
        out_specs=[pl.BlockSpec(blk, lambda bi, n: (bi, n, 0)),
                   pl.BlockSpec(blk, lambda bi, n: (bi, bidx(n), 0))],
        out_shape=[jax.ShapeDtypeStruct((b, s, HG_WIDTH), F32)] * 2,
        scratch_shapes=[pltpu.VMEM((2 * HG_HEADS, HG_DIM, HG_DIM), F32)],
        compiler_params=_cparams(2, 40),
        name="hgrn",
    )(p_hg, p_hg, p_hg, p_hg, p_hg, p_hg, lb, pf, pb)


def _rope(x, cos, sin_signed, first_half):
    n = x.shape[-1]
    half = AT_DIM // 2
    partner = jnp.where(first_half, pltpu.roll(x, n - half, axis=1), pltpu.roll(x, half, axis=1))
    return x * cos + partner * sin_signed


def _head_norm(x, gain, group_mean):
    ms = _dot((x * x).astype(BF16), group_mean)
    return x * lax.rsqrt(ms + NORM_EPS) * gain


def _attn_kernel(q_ref, k_ref, v_ref, cq_ref, sq_ref, ck_ref, sk_ref, gq_ref, gk_ref, mq_ref, mk_ref,
                 o_ref, k_scr, v_scr, *, q_off, n_ctx_tiles, n_ctx_rows):
    i = pl.program_id(1)

    @pl.when(i == 0)
    def _():
        kr = k_ref[0].astype(F32)
        lane = lax.broadcasted_iota(jnp.int32, kr.shape, 1)
        kn = _head_norm(kr, gk_ref[...], mk_ref[...])
        k_scr[...] = _rope(kn, ck_ref[...], sk_ref[...], (lane % AT_DIM) < AT_DIM // 2).astype(BF16)
        v = v_ref[0]
        one = jnp.ones_like(v)
        v_scr[0] = jnp.where(lane < AT_DIM, v, one)
        v_scr[1] = jnp.where(lane < AT_DIM, one, v)

    qr = q_ref[0].astype(F32)
    lane = lax.broadcasted_iota(jnp.int32, qr.shape, 1)
    qn = _head_norm(qr, gq_ref[...], mq_ref[...])
    qn = _rope(qn, cq_ref[...], sq_ref[...], (lane % AT_DIM) < AT_DIM // 2)
    qn = qn * (AT_DIM ** -0.5 * math.log2(math.e))
    lane_t = lax.broadcasted_iota(jnp.int32, (ROW_TILE, LANE), 1)
    kv0 = lane_t < AT_DIM

    def attend(n_keys):
        keys = k_scr[0:n_keys, :]
        heads = [(j, g) for j in range(AT_WIDTH // LANE) for g in range(AT_KV_HEADS)]

        def scores(j, g):
            qt = qn[:, j * LANE:(j + 1) * LANE]
            return _dot_nt(jnp.where(kv0 if g == 0 else ~kv0, qt, 0.0).astype(BF16), keys)

        s_next = scores(*heads[0])
        outs = {}
        for n, (j, g) in enumerate(heads):
            s = s_next
            if n + 1 < len(heads):
                s_next = scores(*heads[n + 1])
            p = jnp.exp2(s - jnp.max(s, axis=1, keepdims=True))
            outs[g] = _dot(p.astype(BF16), v_scr[g, 0:n_keys, :])
            if g == AT_KV_HEADS - 1:
                num = jnp.where(kv0, outs[0], outs[1])
                den = pltpu.roll(jnp.where(kv0, outs[1], outs[0]), AT_DIM, axis=1)
                o_ref[0, :, j * LANE:(j + 1) * LANE] = (num / den).astype(o_ref.dtype)

    n_all = k_scr.shape[0]
    if q_off < n_ctx_tiles:
        @pl.when(i + q_off < n_ctx_tiles)
        def _():
            attend(n_ctx_rows)

        @pl.when(i + q_off >= n_ctx_tiles)
        def _():
            attend(n_all)
    else:
        attend(n_all)


def _attention(p_rest, col_q, col_k, col_v, tabs, gq, gk, q_off, n_ctx_rows):
    b, s, _ = p_rest.shape
    nt = s // ROW_TILE
    cq, sq, ck, sk = tabs
    mq = jnp.asarray(np.kron(np.eye(AT_HEADS), np.full((AT_DIM, AT_DIM), 1.0 / AT_DIM)), BF16)
    mk = jnp.asarray(np.kron(np.eye(AT_KV_HEADS), np.full((AT_DIM, AT_DIM), 1.0 / AT_DIM)), BF16)
    kern = functools.partial(_attn_kernel, q_off=q_off, n_ctx_tiles=n_ctx_rows // ROW_TILE,
                             n_ctx_rows=n_ctx_rows)
    return pl.pallas_call(
        kern,
        grid=(b, nt - q_off),
        in_specs=[pl.BlockSpec((1, ROW_TILE, AT_WIDTH), lambda bi, i: (bi, i + q_off, col_q)),
                  pl.BlockSpec((1, s, AT_KV_WIDTH), lambda bi, i: (bi, 0, col_k)),
                  pl.BlockSpec((1, s, AT_KV_WIDTH), lambda bi, i: (bi, 0, col_v)),
                  pl.BlockSpec((ROW_TILE, AT_WIDTH), lambda bi, i: (i + q_off, 0)),
                  pl.BlockSpec((ROW_TILE, AT_WIDTH), lambda bi, i: (i + q_off, 0)),
                  pl.BlockSpec((s, AT_KV_WIDTH), lambda bi, i: (0, 0)),
                  pl.BlockSpec((s, AT_KV_WIDTH), lambda bi, i: (0, 0)),
                  pl.BlockSpec((1, AT_WIDTH), lambda bi, i: (0, 0)),
                  pl.BlockSpec((1, AT_KV_WIDTH), lambda bi, i: (0, 0)),
                  pl.BlockSpec((AT_WIDTH, AT_WIDTH), lambda bi, i: (0, 0)),
                  pl.BlockSpec((AT_KV_WIDTH, AT_KV_WIDTH), lambda bi, i: (0, 0))],
        out_specs=pl.BlockSpec((1, ROW_TILE, AT_WIDTH), lambda bi, i: (bi, i + q_off, 0)),
        out_shape=jax.ShapeDtypeStruct((b, s, AT_WIDTH), BF16),
        scratch_shapes=[pltpu.VMEM((s, AT_KV_WIDTH), BF16),
                        pltpu.VMEM((AT_KV_HEADS, s, AT_KV_WIDTH), BF16)],
        compiler_params=_cparams(2, 48),
        name="attention",
    )(p_rest, p_rest, p_rest, cq, sq, ck, sk, gq, gk, mq, mk)


def _rope_tables(n_ctx_rows, n_lat_rows):
    rows = n_lat_rows // GRID_W
    row = jnp.repeat(jnp.arange(rows), GRID_W).astype(F32)
    col = jnp.tile(jnp.arange(GRID_W), rows).astype(F32)
    n_freq = AT_DIM // 4
    inv = ROPE_THETA ** (-jnp.arange(n_freq, dtype=F32) / n_freq)
    ang = jnp.concatenate([row[:, None] * inv, col[:, None] * inv], axis=-1)
    cos = jnp.concatenate([jnp.cos(ang), jnp.cos(ang)], axis=-1)
    sin = jnp.concatenate([-jnp.sin(ang), jnp.sin(ang)], axis=-1)
    cos = jnp.concatenate([jnp.ones((n_ctx_rows, AT_DIM), F32), cos], axis=0)
    sin = jnp.concatenate([jnp.zeros((n_ctx_rows, AT_DIM), F32), sin], axis=0)
    return (jnp.tile(cos, (1, AT_HEADS)), jnp.tile(sin, (1, AT_HEADS)),
            jnp.tile(cos, (1, AT_KV_HEADS)), jnp.tile(sin, (1, AT_KV_HEADS)))


def _shift_matrices(n):
    i = np.arange(n)
    down = i[:, None] - 1 == i[None, :]
    up = i[:, None] + 1 == i[None, :]
    return jnp.asarray(np.stack([down, up]).astype(np.float32), BF16)


def _conv3(xb, shift_ref, prev_row, next_row, w, bias):
    n, c = xb.shape
    sub = 8
    r = lax.broadcasted_iota(jnp.int32, (sub, c), 0)
    x = xb.astype(F32)
    if shift_ref is None:
        xm = pltpu.roll(x, 1, axis=0)
        xp = pltpu.roll(x, n - 1, axis=0)
    else:
        xm = _dot(shift_ref[0], xb)
        xp = _dot(shift_ref[1], xb)
    xm = jnp.concatenate([jnp.where(r == 0, prev_row, xm[0:sub]), xm[sub:]], axis=0)
    xp = jnp.concatenate([xp[:n - sub], jnp.where(r == sub - 1, next_row, xp[n - sub:])], axis=0)
    return xm * w[0:1] + x * w[1:2] + xp * w[2:3] + bias


def _halo_specs(width, col, row_off, n_rows, rows=ROW_TILE):
    per = rows // BF16_SUBLANES
    last = n_rows // BF16_SUBLANES - 1
    return [
        pl.BlockSpec((1, rows, width), lambda bi, i: (bi, i + row_off, col)),
        pl.BlockSpec((1, BF16_SUBLANES, width),
                     lambda bi, i: (bi, jnp.maximum((i + row_off) * per - 1, 0), col)),
        pl.BlockSpec((1, BF16_SUBLANES, width),
                     lambda bi, i: (bi, jnp.minimum((i + row_off + 1) * per, last), col)),
    ]


def _halo_rows(prev_ref, next_ref, is_first, is_last, cols=None):
    sl = slice(None) if cols is None else cols
    prev_row = prev_ref[0, BF16_SUBLANES - 1:BF16_SUBLANES, sl].astype(F32)
    next_row = next_ref[0, 0:1, sl].astype(F32)
    prev_row = jnp.where(is_first, 0.0, prev_row)
    next_row = jnp.where(is_last, 0.0, next_row)
    return prev_row, next_row


def _hypre_kernel(z_ref, zp_ref, zn_ref, sh_ref, w_ref, b_ref, db_ref, u_ref, ud_ref, x0_ref):
    i = pl.program_id(1)
    prev_row, next_row = _halo_rows(zp_ref, zn_ref, i == 0, i == pl.num_programs(1) - 1)
    zc = _conv3(z_ref[0], sh_ref, prev_row, next_row, w_ref[...], b_ref[...])
    x0 = zc[:, :HY_WIDTH]
    x1 = zc[:, HY_WIDTH:2 * HY_WIDTH]
    v = zc[:, 2 * HY_WIDTH:]
    u = v * x1
    u_ref[0] = u.astype(BF16)
    ud_ref[0] = (u * db_ref[...]).astype(BF16)
    x0_ref[0] = x0.astype(BF16)


def _hyena_pre(p_rest, col, row_off, n_rows, conv_w, conv_b, d_bias):
    b, s, _ = p_rest.shape
    width = 3 * HY_WIDTH
    out = jax.ShapeDtypeStruct((b, n_rows, HY_WIDTH), BF16)
    ospec = pl.BlockSpec((1, ROW_TILE, HY_WIDTH), lambda bi, i: (bi, i, 0))
    return pl.pallas_call(
        _hypre_kernel,
        grid=(b, n_rows // ROW_TILE),
        in_specs=_halo_specs(width, col, row_off, s) + [
            pl.BlockSpec((2, ROW_TILE, ROW_TILE), lambda bi, i: (0, 0, 0)),
            pl.BlockSpec((3, width), lambda bi, i: (0, 0)),
            pl.BlockSpec((1, width), lambda bi, i: (0, 0)),
            pl.BlockSpec((1, HY_WIDTH), lambda bi, i: (0, 0))],
        out_specs=[ospec, ospec, ospec],
        out_shape=[out, out, out],
        compiler_params=_cparams(2, 32),
        name="hyena_pre",
    )(p_rest, p_rest, p_rest, _shift_matrices(ROW_TILE), conv_w, conv_b.reshape(1, width),
      d_bias.reshape(1, HY_WIDTH))


def _hyfilt_kernel(z_ref, t_ref, dl_ref, w1_ref, b1_ref, wi_ref, bi_ref, fr_ref, wl_ref, o_ref):
    fr = fr_ref[...]
    h = jnp.sin(fr * (_dot3(z_ref[...], w1_ref[...]) + b1_ref[...]))
    for j in range(HY_INNER):
        h = jnp.sin(fr * (_dot3(h, wi_ref[j]) + bi_ref[j]))
    h = _dot3(h, wl_ref[...])
    decay = jnp.exp(-t_ref[...] * dl_ref[...])
    hf = h[:, :HY_WIDTH] * decay
    hb = h[:, HY_WIDTH:] * decay
    o_ref[...] = jnp.concatenate([hf + hb, hf - hb], axis=1)


def _pad2(a, rows, cols):
    return jnp.pad(a, ((0, rows - a.shape[0]), (0, cols - a.shape[1])))


def _hyena_filter_sums(n, w1, b1, wi, bi, freq, w_last):
    t = jnp.linspace(0.0, 1.0, n, dtype=F32)[:, None]
    w = 2.0 * math.pi * jnp.arange(n, dtype=F32)[:, None] / n
    f = jnp.linspace(1e-4, HY_BANDS - 1, HY_BANDS, dtype=F32)[None, :]
    z = jnp.concatenate([t, jnp.cos(f * w), -jnp.sin(f * w)], axis=-1)
    max_decay = math.log(HY_TARGET) / HY_FAST_DECAY
    min_decay = math.log(HY_TARGET) / HY_SLOW_DECAY
    deltas = jnp.abs(jnp.linspace(min_decay, max_decay, HY_WIDTH, dtype=F32))[None, :]
    zp = _pad2(z, n, LANE)
    w1p = _pad2(w1, LANE, LANE)
    b1p = _pad2(b1[None, :], 1, LANE)
    wip = jnp.stack([_pad2(wi[j], LANE, LANE) for j in range(HY_INNER)])
    bip = jnp.stack([_pad2(bi[j][None, :], 1, LANE) for j in range(HY_INNER)])
    frp = _pad2(freq[None, :], 1, LANE)
    wlp = _pad2(w_last, LANE, 2 * HY_WIDTH)
    tr = min(n, ROW_TILE)
    full = lambda shape: pl.BlockSpec(shape, lambda i: (0,) * len(shape))
    return pl.pallas_call(
        _hyfilt_kernel,
        grid=(n // tr,),
        in_specs=[pl.BlockSpec((tr, LANE), lambda i: (i, 0)),
                  pl.BlockSpec((tr, 1), lambda i: (i, 0)),
                  full((1, HY_WIDTH)), full((LANE, LANE)), full((1, LANE)),
                  full((HY_INNER, LANE, LANE)), full((HY_INNER, 1, LANE)), full((1, LANE)),
                  full((LANE, 2 * HY_WIDTH))],
        out_specs=pl.BlockSpec((tr, 2 * HY_WIDTH), lambda i: (i, 0)),
        out_shape=jax.ShapeDtypeStruct((n, 2 * HY_WIDTH), F32),
        compiler_params=_cparams(1, 32),
        name="hyena_filter",
    )(zp, t, deltas, w1p, b1p, wip, bip, frp, wlp)


def _dft_tables(n):
    f = jnp.arange(n, dtype=jnp.int32)[:, None]
    t = jnp.arange(n, dtype=jnp.int32)[None, :]
    ang = ((f * t) % (2 * n)).astype(F32) * (math.pi / n)
    nyq = jnp.where(t % 2 == 0, 1.0, -1.0).astype(F32)
    return jnp.stack([jnp.cos(ang), jnp.where(f == 0, nyq, jnp.sin(ang))])


def _hyfwd_kernel(u_ref, f_ref, co_ref, o_ref):
    u = u_ref[0]
    ure = _dot(f_ref[0], u)
    uim = _dot(f_ref[1], u)
    o_ref[0, 0] = (ure * co_ref[0] - uim * co_ref[1]).astype(BF16)
    o_ref[0, 1] = (ure * co_ref[2] + uim * co_ref[3]).astype(BF16)


def _hyinv_kernel(y_ref, ft_ref, ud_ref, x0_ref, o_ref):
    y = _dot(ft_ref[0], y_ref[0, 0]) + _dot(ft_ref[1], y_ref[0, 1])
    o_ref[0] = ((y + ud_ref[0].astype(F32)) * x0_ref[0].astype(F32)).astype(BF16)


def _hyena_conv(u, ud, x0, tables, coef):
    b, n, c = u.shape
    tf = min(n, DFT_ROWS)
    f_bf = tables.astype(BF16)
    ft_bf = jnp.swapaxes(tables, 1, 2).astype(BF16)
    spec = pl.pallas_call(
        _hyfwd_kernel,
        grid=(n // tf, b),
        in_specs=[pl.BlockSpec((1, n, c), lambda j, bi: (bi, 0, 0)),
                  pl.BlockSpec((2, tf, n), lambda j, bi: (0, j, 0)),
                  pl.BlockSpec((4, tf, c), lambda j, bi: (0, j, 0))],
        out_specs=pl.BlockSpec((1, 2, tf, c), lambda j, bi: (bi, 0, j, 0)),
        out_shape=jax.ShapeDtypeStruct((b, 2, n, c), BF16),
        compiler_params=_cparams(2, 40),
        name="hyena_dft",
    )(u, f_bf, coef)
    return pl.pallas_call(
        _hyinv_kernel,
        grid=(b, n // tf),
        in_specs=[pl.BlockSpec((1, 2, n, c), lambda bi, j: (bi, 0, 0, 0)),
                  pl.BlockSpec((2, tf, n), lambda bi, j: (0, j, 0)),
                  pl.BlockSpec((1, tf, c), lambda bi, j: (bi, j, 0)),
                  pl.BlockSpec((1, tf, c), lambda bi, j: (bi, j, 0))],
        out_specs=pl.BlockSpec((1, tf, c), lambda bi, j: (bi, j, 0)),
        out_shape=jax.ShapeDtypeStruct((b, n, c), BF16),
        compiler_params=_cparams(2, 40),
        name="hyena_idft",
    )(spec, ft_bf, ud, x0)


def _hyena_coef(tables, hsum_hdiff):
    n = tables.shape[1]
    c = HY_WIDTH
    r = _matmul_f32(tables.reshape(2 * n, n), hsum_hdiff, min(n, ROW_TILE), "hyena_kernel_dft")
    k_re = r[:n, :c]
    k_im = r[n:, c:]
    k_nyq = r[n:n + 1, :c]
    first = (jnp.arange(n) == 0)[:, None]
    scale = jnp.where(first, 1.0 / (2 * n), 2.0 / (2 * n)).astype(F32)
    zero = jnp.zeros_like(k_im)
    return jnp.stack([k_re * scale,
                      jnp.where(first, zero, k_im * scale),
                      jnp.where(first, zero, k_im * scale),
                      jnp.where(first, k_nyq, k_re) * scale])


def _merge_kernel(*refs, n_ctx_tiles, row_off, has_ctx):
    if has_ctx:
        (of_ref, ob_ref, zg_ref, att_ref, cx_ref, cc_ref, gate_ref, x_ref, m_ref, ghg_ref, gpost_ref,
         gffn_ref, woa_ref, wob_ref, woc_ref, wout_ref, o_ref, h_ref) = refs
    else:
        (of_ref, ob_ref, zg_ref, att_ref, cx_ref, gate_ref, x_ref, m_ref, ghg_ref, gpost_ref,
         gffn_ref, woa_ref, wob_ref, woc_ref, wout_ref, o_ref, h_ref) = refs
    o = of_ref[0] + ob_ref[0]
    ghg = ghg_ref[...]
    a = jnp.concatenate([_rms(o[:, h * HG_DIM:(h + 1) * HG_DIM], ghg) for h in range(HG_HEADS)], axis=1)
    zg = zg_ref[0]
    a = a * (zg * _sigmoid(zg))
    c = cx_ref[0]
    if has_ctx:
        c = jnp.where(pl.program_id(1) + row_off < n_ctx_tiles, cc_ref[0], c)
    d = x_ref.shape[-1]
    ya = _dot(a.astype(BF16), woa_ref[...])
    yb = _dot(att_ref[0], wob_ref[...])
    yc = _dot(c, woc_ref[...])
    m = (_sigmoid(gate_ref[0, :, 0:d].astype(F32)) * ya
         + _sigmoid(gate_ref[0, :, d:2 * d].astype(F32)) * yb
         + _sigmoid(gate_ref[0, :, 2 * d:3 * d].astype(F32)) * yc)
    y = _dot(m.astype(BF16), wout_ref[...])
    x_new = x_ref[0] + m_ref[0, 2:3, :] * _rms(y, gpost_ref[...])
    o_ref[0] = x_new
    h_ref[0] = (_rms(x_new, gffn_ref[...]) * (1.0 + m_ref[0, 4:5, :]) + m_ref[0, 3:4, :]).astype(BF16)


def _merge(o_f, o_b, p_hg, att, c_x, c_c, p_rest, xs, mods, g_hg, g_post, g_ffn, w_oa, w_ob, w_oc, w_out,
           row_off, n_ctx_rows):
    b, s, d = xs.shape
    nct = n_ctx_rows // ROW_TILE
    n_tiles = s // ROW_TILE - row_off
    has_ctx = c_c is not None
    ctx_row = b

    def stream(width, col=0):
        return pl.BlockSpec((1, ROW_TILE, width), lambda bi, i: (bi, i + row_off, col))

    def full(shape):
        return pl.BlockSpec(shape, lambda bi, i: (0,) * len(shape))

    in_specs = [stream(HG_WIDTH), stream(HG_WIDTH), stream(HG_WIDTH, 4), stream(AT_WIDTH),
                pl.BlockSpec((1, ROW_TILE, HY_WIDTH),
                             lambda bi, i: (bi, jnp.maximum(i + row_off - nct, 0), 0))]
    args = [o_f, o_b, p_hg, att, c_x]
    if has_ctx:
        in_specs.append(pl.BlockSpec((1, ROW_TILE, HY_WIDTH),
                                     lambda bi, i: (bi, jnp.minimum(i + row_off, nct - 1), 0)))
        args.append(c_c)
    in_specs += [stream(3 * d), stream(d),
                 pl.BlockSpec((1, 6, d), lambda bi, i: (jnp.where(i + row_off < nct, ctx_row, bi), 0, 0)),
                 full((1, HG_DIM)), full((1, d)), full((1, d)),
                 full((HG_WIDTH, d)), full((AT_WIDTH, d)), full((HY_WIDTH, d)), full((d, d))]
    args += [p_rest, xs, mods, g_hg.reshape(1, HG_DIM), g_post.reshape(1, d), g_ffn.reshape(1, d),
             w_oa, w_ob, w_oc, w_out]
    ospec = pl.BlockSpec((1, ROW_TILE, d), lambda bi, i: (bi, i, 0))
    return pl.pallas_call(
        functools.partial(_merge_kernel, n_ctx_tiles=nct, row_off=row_off, has_ctx=has_ctx),
        grid=(b, n_tiles),
        in_specs=in_specs,
        out_specs=[ospec, ospec],
        out_shape=[jax.ShapeDtypeStruct((b, n_tiles * ROW_TILE, d), F32),
                   jax.ShapeDtypeStruct((b, n_tiles * ROW_TILE, d), BF16)],
        compiler_params=_cparams(2, 48),
        name="merge",
    )(*args)


FFN_COLS = 256
FFN_DOWN_GROUPS = 2


def _ffn_kernel(*refs, first_tiles, last_tiles, has_next):
    if has_next:
        (h_ref, hp_ref, hn_ref, wu_ref, w_ref, b_ref, x_ref, m_ref, g_ref, wd_ref, mn_ref, gn_ref,
         o_ref, hx_ref, act_ref) = refs
    else:
        h_ref, hp_ref, hn_ref, wu_ref, w_ref, b_ref, x_ref, m_ref, g_ref, wd_ref, o_ref, act_ref = refs
    i = pl.program_id(1)
    is_first = functools.reduce(jnp.logical_or, [i == t for t in first_tiles])
    is_last = functools.reduce(jnp.logical_or, [i == t for t in last_tiles])
    d_ff = wd_ref.shape[0]
    rows = h_ref.shape[1]
    halo = BF16_SUBLANES
    ext = rows + 2 * halo
    hp = jnp.where(is_first, jnp.zeros_like(hp_ref[0]), hp_ref[0])
    hn = jnp.where(is_last, jnp.zeros_like(hn_ref[0]), hn_ref[0])
    h_ext = jnp.concatenate([hp, h_ref[0], hn], axis=0)

    def up(j):
        return [_dot(h_ext, wu_ref[:, base + j * FFN_COLS:base + (j + 1) * FFN_COLS]) for base in (0, d_ff)]

    def conv(u, cols):
        w = w_ref[:, cols]
        full = pltpu.roll(u, 1, axis=0) * w[0:1] + u * w[1:2] + pltpu.roll(u, ext - 1, axis=0) * w[2:3]
        return full[halo:halo + rows] + b_ref[:, cols]

    n_chunks = d_ff // FFN_COLS
    per_group = -(-n_chunks // FFN_DOWN_GROUPS)
    acc = None
    u_next = up(0)
    for j in range(n_chunks):
        u = u_next
        if j + 1 < n_chunks:
            u_next = up(j + 1)
        a = conv(u[0], slice(j * FFN_COLS, (j + 1) * FFN_COLS))
        g = conv(u[1], slice(d_ff + j * FFN_COLS, d_ff + (j + 1) * FFN_COLS))
        act_ref[:, j * FFN_COLS:(j + 1) * FFN_COLS] = (a * _sigmoid(a) * g).astype(BF16)
        if (j + 1) % per_group == 0 or j + 1 == n_chunks:
            lo = (j // per_group) * per_group * FFN_COLS
            part = _dot(act_ref[:, lo:(j + 1) * FFN_COLS], wd_ref[lo:(j + 1) * FFN_COLS, :])
            acc = part if acc is None else acc + part
    x_new = x_ref[0] + m_ref[0, 5:6, :] * _rms(acc, g_ref[...])
    o_ref[0] = x_new
    if has_next:
        hx_ref[0] = (_rms(x_new, gn_ref[...]) * (1.0 + mn_ref[0, 1:2, :]) + mn_ref[0, 0:1, :]).astype(BF16)


def _ffn(h, xs, mods, w_up, conv_w, conv_b, g_post, w_down, n_ctx_rows, mods_next=None, g_next=None):
    b, s, d = xs.shape
    d_ff = w_down.shape[0]
    rows = FFN_ROWS if (n_ctx_rows % FFN_ROWS == 0 and s % FFN_ROWS == 0) else ROW_TILE
    nt = s // rows
    nct = n_ctx_rows // rows
    first_tiles = tuple(sorted({0, nct}))
    last_tiles = tuple(sorted({nct - 1, nt - 1} - {-1}))
    ctx_row = b
    has_next = mods_next is not None
    full = lambda shape: pl.BlockSpec(shape, lambda bi, i: (0,) * len(shape))
    mspec = pl.BlockSpec((1, 6, d), lambda bi, i: (jnp.where(i < nct, ctx_row, bi), 0, 0))
    ospec = pl.BlockSpec((1, rows, d), lambda bi, i: (bi, i, 0))
    resident = lambda shape: pl.BlockSpec(shape, lambda bi, i: (0,) * len(shape), pipeline_mode=pl.Buffered(1))
    in_specs = _halo_specs(d, 0, 0, s, rows) + [
        resident((d, 2 * d_ff)), full((3, 2 * d_ff)), full((1, 2 * d_ff)),
        ospec, mspec, full((1, d)), resident((d_ff, d))]
    args = [h, h, h, w_up, conv_w, conv_b.reshape(1, 2 * d_ff), xs, mods, g_post.reshape(1, d), w_down]
    out_specs = [ospec]
    out_shape = [jax.ShapeDtypeStruct((b, s, d), F32)]
    if has_next:
        in_specs += [mspec, full((1, d))]
        args += [mods_next, g_next.reshape(1, d)]
        out_specs.append(ospec)
        out_shape.append(jax.ShapeDtypeStruct((b, s, d), BF16))
    return pl.pallas_call(
        functools.partial(_ffn_kernel, first_tiles=first_tiles, last_tiles=last_tiles, has_next=has_next),
        grid=(b, nt),
        in_specs=in_specs,
        out_specs=out_specs,
        out_shape=out_shape,
        scratch_shapes=[pltpu.VMEM((rows, d_ff), BF16)],
        compiler_params=_cparams(2, 56),
        name="ffn",
    )(*args)


def _deinterleave():
    return np.concatenate([np.arange(0, AT_DIM, 2), np.arange(1, AT_DIM, 2)])


def _q_head_order():
    return [h for j in range(AT_GROUP) for h in (j, AT_GROUP + j)]


def _largest_tile(n, cap):
    best = LANE
    for t in range(LANE, cap + 1, LANE):
        if n % t == 0:
            best = t
    return best


def kernel(x, c, ctx, c_ctx, w_ada, b_ada, g_pre_mix, g_post_mix, g_pre_ffn, g_post_ffn, w_in, hg_lower_bounds, hg_norm, q_norm, k_norm, hy_conv_w, hy_conv_b, hy_w1, hy_b1, hy_wi, hy_bi, hy_freq, hy_w_last, hy_bias, w_oa, w_ob, w_oc, w_out, w_up, ffn_conv_w, ffn_conv_b, w_down):
    bsz, n_lat, d = x.shape
    n_ctx = ctx.shape[1]
    depth = w_ada.shape[0]
    d_ff = w_down.shape[1]
    assert AT_KV_HEADS == 2 and AT_GROUP * LANE == AT_WIDTH and AT_KV_WIDTH == LANE
    assert n_ctx % ROW_TILE == 0 and n_lat % ROW_TILE == 0 and n_lat % GRID_W == 0
    assert (bsz * (n_ctx + n_lat)) % MM_ROWS == 0 and (bsz * n_lat) % MM_ROWS == 0

    lbp = jax.nn.softmax(hg_lower_bounds.astype(F32), axis=0)
    lower = jnp.cumsum(lbp, axis=0) - lbp[0]

    rp = -(-(bsz + 1) // 8) * 8
    src = jnp.concatenate([c, c_ctx[None, :], jnp.zeros((rp - bsz - 1, d), F32)], axis=0)
    mods_all = _ada(src, w_ada, b_ada).reshape(depth, rp, 6, d)

    o_q = 5 * HG_WIDTH
    o_k = o_q + AT_WIDTH
    o_v = o_k + AT_KV_WIDTH
    o_hy = o_v + AT_KV_WIDTH
    o_gate = o_hy + 3 * HY_WIDTH
    deint = _deinterleave()
    q_cols = np.concatenate([o_q + h * AT_DIM + deint for h in _q_head_order()])
    k_cols = np.concatenate([o_k + g * AT_DIM + deint for g in range(AT_KV_HEADS)])
    rest_cols = np.concatenate([np.arange(o_gate, o_gate + 3 * d), np.arange(o_hy, o_hy + 3 * HY_WIDTH),
                                q_cols, k_cols, np.arange(o_v, o_v + AT_KV_WIDTH)])
    col_hy = (3 * d) // (3 * HY_WIDTH)
    col_q = (3 * d + 3 * HY_WIDTH) // AT_WIDTH
    col_k = (3 * d + 3 * HY_WIDTH + AT_WIDTH) // AT_KV_WIDTH
    col_v = col_k + 1
    assert (3 * d) % (3 * HY_WIDTH) == 0 and (3 * d + 3 * HY_WIDTH) % AT_WIDTH == 0
    ob_rows = np.concatenate([np.arange(h * AT_DIM, (h + 1) * AT_DIM) for h in _q_head_order()])

    rope_tabs = _rope_tables(n_ctx, n_lat)
    dft_lat = _dft_tables(n_lat)
    dft_ctx = _dft_tables(n_ctx)
    nct = n_ctx // ROW_TILE

    xs = jnp.concatenate([ctx, x], axis=1)
    s_all = n_ctx + n_lat
    h = _modulate(xs, mods_all[0], g_pre_mix[0], 0, 1, nct)
    for l in range(depth):
        need_ctx = l < depth - 1
        mods = mods_all[l]
        w_hg = w_in[l][:, :5 * HG_WIDTH].astype(BF16)
        w_rest = w_in[l][:, rest_cols].astype(BF16)

        h = h.reshape(bsz * s_all, d)
        p_hg = _matmul(h, w_hg, F32, _largest_tile(5 * HG_WIDTH, 1280), "proj_hgrn").reshape(bsz, s_all, -1)
        p_rest = _matmul(h, w_rest, BF16, _largest_tile(w_rest.shape[1], 1792), "proj_rest").reshape(bsz, s_all, -1)

        o_f, o_b = _hgrn(p_hg, lower[l], n_ctx)

        gq = jnp.tile(q_norm[l][deint], AT_HEADS)[None, :]
        gk = jnp.tile(k_norm[l][deint], AT_KV_HEADS)[None, :]
        row_off = 0 if need_ctx else nct
        att = _attention(p_rest, col_q, col_k, col_v, rope_tabs, gq, gk, row_off, n_ctx)

        filt_args = (hy_w1[l], hy_b1[l], hy_wi[l], hy_bi[l], hy_freq[l], hy_w_last[l])
        coef = _hyena_coef(dft_lat, _hyena_filter_sums(n_lat, *filt_args))
        c_x = _hyena_conv(*_hyena_pre(p_rest, col_hy, nct, n_lat, hy_conv_w[l], hy_conv_b[l], hy_bias[l]),
                          dft_lat, coef)
        c_c = None
        if need_ctx:
            coef_c = _hyena_coef(dft_ctx, _hyena_filter_sums(n_ctx, *filt_args))
            c_c = _hyena_conv(*_hyena_pre(p_rest, col_hy, 0, n_ctx, hy_conv_w[l], hy_conv_b[l], hy_bias[l]),
                              dft_ctx, coef_c)

        xs, h2 = _merge(o_f, o_b, p_hg, att, c_x, c_c, p_rest, xs, mods, hg_norm[l], g_post_mix[l],
                        g_pre_ffn[l], w_oa[l].astype(BF16), w_ob[l][ob_rows].astype(BF16),
                        w_oc[l].astype(BF16), w_out[l].astype(BF16), row_off, n_ctx)
        n_ctx_now = n_ctx if need_ctx else 0
        ffn_args = (h2, xs, mods, w_up[l].astype(BF16), ffn_conv_w[l], ffn_conv_b[l], g_post_ffn[l],
                    w_down[l].astype(BF16), n_ctx_now)
        if need_ctx:
            xs, h = _ffn(*ffn_args, mods_all[l + 1], g_pre_mix[l + 1])
        else:
            xs, = _ffn(*ffn_args)
    return xs
```

```python
import functools
import math

import jax
import jax.numpy as jnp
import numpy as np
from jax import lax
from jax.experimental import pallas as pl
from jax.experimental.pallas import tpu as pltpu

F32 = jnp.float32
BF16 = jnp.bfloat16

NORM_EPS = 1e-6
GRID_W = 64
HG_HEADS = 4
HG_DIM = 128
HG_WIDTH = HG_HEADS * HG_DIM
HG_EXP_CLIP = 30.0
AT_HEADS = 8
AT_KV_HEADS = 2
AT_DIM = 64
AT_GROUP = AT_HEADS // AT_KV_HEADS
AT_WIDTH = AT_HEADS * AT_DIM
AT_KV_WIDTH = AT_KV_HEADS * AT_DIM
ROPE_THETA = 10000.0
HY_WIDTH = 512
HY_EMB_DIM = 33
HY_BANDS = (HY_EMB_DIM - 1) // 2
HY_FILTER_WIDTH = 64
HY_INNER = 2
HY_FAST_DECAY = 0.3
HY_SLOW_DECAY = 1.5
HY_TARGET = 1e-2

LANE = 128
BF16_SUBLANES = 16
ROW_TILE = 256
HG_CHUNK = 128
HG_BLOCK = 256
HG_LEVELS = tuple(HG_CHUNK >> (j + 1) for j in range(int(math.log2(HG_CHUNK))))
MM_ROWS = 1024
DFT_ROWS = 512
AT_LOOKAHEAD = 2
FFN_ROWS = 512
VMEM_CAP = 56 * 1024 * 1024


def _cparams(n_axes, vmem_mb):
    return pltpu.CompilerParams(
        dimension_semantics=("arbitrary",) * n_axes,
        vmem_limit_bytes=min(int(vmem_mb) * 1024 * 1024, VMEM_CAP))


def _dot(a, b):
    return jnp.dot(a, b, preferred_element_type=F32)


def _dot_nt(a, b):
    return lax.dot_general(a, b, (((1,), (1,)), ((), ())), preferred_element_type=F32)


def _split_bf16(a):
    hi = a.astype(BF16)
    lo = (a - hi.astype(F32)).astype(BF16)
    return hi, lo


def _dot3(a, b):
    ah, al = _split_bf16(a)
    bh, bl = _split_bf16(b)
    return _dot(ah, bh) + (_dot(ah, bl) + _dot(al, bh))


def _rms(x, g):
    return x * lax.rsqrt(jnp.mean(x * x, axis=-1, keepdims=True) + NORM_EPS) * g


def _sigmoid(x):
    return 0.5 * jnp.tanh(0.5 * x) + 0.5


def _ada_kernel(src_ref, w_ref, b_ref, o_ref):
    s = src_ref[...]
    s = s * _sigmoid(s)
    o_ref[0] = _dot3(s, w_ref[0]) + b_ref[0]


def _ada(src, w_ada, b_ada):
    depth, d, d6 = w_ada.shape
    rp = src.shape[0]
    tn = d
    return pl.pallas_call(
        _ada_kernel,
        grid=(depth, d6 // tn),
        in_specs=[pl.BlockSpec((rp, d), lambda l, j: (0, 0)),
                  pl.BlockSpec((1, d, tn), lambda l, j: (l, 0, j)),
                  pl.BlockSpec((1, 1, tn), lambda l, j: (l, 0, j))],
        out_specs=pl.BlockSpec((1, rp, tn), lambda l, j: (l, 0, j)),
        out_shape=jax.ShapeDtypeStruct((depth, rp, d6), F32),
        compiler_params=_cparams(2, 32),
        name="ada",
    )(src, w_ada, b_ada.reshape(depth, 1, d6))


def _mod_kernel(c_ref, x_ref, m_ref, g_ref, xs_ref, o_ref, *, k_shift, k_scale, n_ctx_tiles):
    x = jnp.where(pl.program_id(1) < n_ctx_tiles, c_ref[0], x_ref[0])
    xs_ref[0] = x
    shift = m_ref[0, k_shift:k_shift + 1, :]
    scale = m_ref[0, k_scale:k_scale + 1, :]
    o_ref[0] = (_rms(x, g_ref[...]) * (1.0 + scale) + shift).astype(o_ref.dtype)


def _join_modulate(ctx, x, mods, g, k_shift, k_scale):
    b, n_lat, d = x.shape
    n_ctx_tiles = ctx.shape[1] // ROW_TILE
    nt = n_ctx_tiles + n_lat // ROW_TILE
    ctx_row = b
    ospec = pl.BlockSpec((1, ROW_TILE, d), lambda bi, i: (bi, i, 0))
    return pl.pallas_call(
        functools.partial(_mod_kernel, k_shift=k_shift, k_scale=k_scale, n_ctx_tiles=n_ctx_tiles),
        grid=(b, nt),
        in_specs=[pl.BlockSpec((1, ROW_TILE, d), lambda bi, i: (bi, jnp.minimum(i, n_ctx_tiles - 1), 0)),
                  pl.BlockSpec((1, ROW_TILE, d), lambda bi, i: (bi, jnp.maximum(i - n_ctx_tiles, 0), 0)),
                  pl.BlockSpec((1, 6, d), lambda bi, i: (jnp.where(i < n_ctx_tiles, ctx_row, bi), 0, 0)),
                  pl.BlockSpec((1, d), lambda bi, i: (0, 0))],
        out_specs=[ospec, ospec],
        out_shape=[jax.ShapeDtypeStruct((b, nt * ROW_TILE, d), F32),
                   jax.ShapeDtypeStruct((b, nt * ROW_TILE, d), BF16)],
        compiler_params=_cparams(2, 16),
        name="modulate",
    )(ctx, x, mods, g.reshape(1, d))


def _mm_kernel(a_ref, b_ref, o_ref):
    o_ref[...] = _dot(a_ref[...], b_ref[...]).astype(o_ref.dtype)


def _matmul(a, w, out_dtype, tn, name):
    m, k = a.shape
    n = w.shape[1]
    tm = MM_ROWS
    assert m % tm == 0 and n % tn == 0
    return pl.pallas_call(
        _mm_kernel,
        grid=(m // tm, n // tn),
        in_specs=[pl.BlockSpec((tm, k), lambda i, j: (i, 0)),
                  pl.BlockSpec((k, tn), lambda i, j: (0, j))],
        out_specs=pl.BlockSpec((tm, tn), lambda i, j: (i, j)),
        out_shape=jax.ShapeDtypeStruct((m, n), out_dtype),
        compiler_params=_cparams(2, 48),
        name=name,
    )(a, w)


def _mm3_kernel(a_ref, b_ref, o_ref):
    o_ref[...] = _dot3(a_ref[...], b_ref[...])


def _matmul_f32(a, w, tm, name):
    m, k = a.shape
    n = w.shape[1]
    return pl.pallas_call(
        _mm3_kernel,
        grid=(m // tm,),
        in_specs=[pl.BlockSpec((tm, k), lambda i: (i, 0)),
                  pl.BlockSpec((k, n), lambda i: (0, 0))],
        out_specs=pl.BlockSpec((tm, n), lambda i: (i, 0)),
        out_shape=jax.ShapeDtypeStruct((m, n), F32),
        compiler_params=_cparams(1, 48),
        name=name,
    )(a, w)


def _hg_scan_matrix(reverse):
    t_n = HG_CHUNK
    t = np.arange(t_n)[:, None]
    u = np.arange(t_n)[None, :]
    rows = [(u >= t) if reverse else (u <= t)]
    for w in HG_LEVELS:
        base = (t // (2 * w)) * (2 * w)
        mid = base + w
        upper = (t - base) >= w
        if reverse:
            m = np.where(upper, (u >= mid) & (u < t), (u >= t) & (u < mid))
        else:
            m = np.where(upper, (u >= mid) & (u <= t), (u > t) & (u < mid))
        rows.append(m)
    m = np.concatenate(rows, axis=0).astype(np.float32)
    return np.concatenate([m, m], axis=1)


def _hgrn_kernel(qf_ref, zf_ref, vf_ref, qb_ref, zb_ref, vb_ref, lb_ref, pf_ref, pb_ref,
                 of_ref, ob_ref, s_ref):
    @pl.when(pl.program_id(1) == 0)
    def _():
        s_ref[...] = jnp.zeros_like(s_ref)

    t_n = HG_CHUNK
    ti = lax.broadcasted_iota(jnp.int32, (t_n, t_n), 0)
    si = lax.broadcasted_iota(jnp.int32, (t_n, t_n), 1)
    tx = ti ^ si
    dirs = ((qf_ref, zf_ref, vf_ref, pf_ref, of_ref), (qb_ref, zb_ref, vb_ref, pb_ref, ob_ref))
    masks = []
    for reverse in (False, True):
        later = (ti < si) if reverse else (ti > si)
        masks.append([later & (tx >= w) & (tx < 2 * w) for w in HG_LEVELS])
    n_sub = qf_ref.shape[1] // t_n
    heads = range(HG_HEADS)
    units = [(d, c, h) for d in range(2) for c in range(n_sub) for h in heads]
    cols = lambda h: slice(h * HG_DIM, (h + 1) * HG_DIM)
    rows = lambda c: slice(c * t_n, (c + 1) * t_n)

    kk, kb, qb, cat, x = {}, {}, {}, {}, {}
    a = {u: jnp.zeros((t_n, t_n), F32) for u in units}

    def gates(d, c, h):
        z = dirs[d][1][0, rows(c), cols(h)]
        lb = lb_ref[d:d + 1, cols(h)]
        e = jnp.exp(-jnp.abs(z))
        log_num = jnp.log(jnp.where(z >= 0.0, 1.0 + lb * e, e + lb))
        log_clip = z + jnp.log(1.0 + lb * math.exp(HG_EXP_CLIP))
        lf = jnp.where(z < -HG_EXP_CLIP, log_clip, log_num) - jnp.log(1.0 + e)
        half = 0.5 * (1.0 - lb)
        kk[d, c, h] = half - half * jnp.tanh(0.5 * z)
        kb[d, c, h] = kk[d, c, h].astype(BF16)
        qb[d, c, h] = dirs[d][0][0, rows(c), cols(h)].astype(BF16)
        hi, lo = _split_bf16(lf)
        cat[d, c, h] = jnp.concatenate([hi, lo], axis=0)

    def exponents(d):
        rhs = jnp.concatenate([cat[d, c, h] for c in range(n_sub) for h in heads], axis=1)
        xd = _dot(dirs[d][3][...], rhs)
        for c in range(n_sub):
            for h in heads:
                x[d, c, h] = xd[:, cols(c * HG_HEADS + h)]

    def level(d, j):
        for c in range(n_sub):
            for h in heads:
                u = (d, c, h)
                ew = jnp.exp(x[u][(j + 1) * t_n:(j + 2) * t_n]).astype(BF16)
                pw = _dot_nt(qb[u] * ew, kb[u] * ew)
                a[u] = jnp.where(masks[d][j], pw, a[u])

    def finish(d, c, h):
        u = (d, c, h)
        q = dirs[d][0][0, rows(c), cols(h)]
        v = dirs[d][2][0, rows(c), cols(h)]
        g = x[u][0:t_n]
        g_last = g[0:1] if d == 1 else g[t_n - 1:t_n]
        st = s_ref[d * HG_HEADS + h]
        dqk = jnp.sum(q * kk[u], axis=1, keepdims=True)
        o = (_dot(a[u].astype(BF16), v.astype(BF16)) + dqk * v
             + _dot_nt((q * jnp.exp(g)).astype(BF16), st.astype(BF16)))
        kd = (kk[u] * jnp.exp(g_last - g)).astype(BF16)
        dirs[d][4][0, rows(c), cols(h)] = o
        s_ref[d * HG_HEADS + h] = st * jnp.exp(g_last) + _dot(v.T.astype(BF16), kd)

    for d in range(2):
        for c in range(n_sub):
            for h in heads:
                gates(d, c, h)
        exponents(d)
    for j in range(len(HG_LEVELS)):
        for d in range(2):
            level(d, j)
    for d in range(2):
        for c in (range(n_sub) if d == 0 else reversed(range(n_sub))):
            for h in heads:
                finish(d, c, h)


def _hgrn(p_hg, lb, n_ctx_rows):
    b, s, _ = p_hg.shape
    assert s % HG_BLOCK == 0 and n_ctx_rows % HG_BLOCK == 0
    nb = s // HG_BLOCK
    nc = n_ctx_rows // HG_BLOCK

    def bidx(n):
        return jnp.where(n < nc, nc - 1 - n, nb - 1 - (n - nc))

    blk = (1, HG_BLOCK, HG_WIDTH)
    pf = jnp.asarray(_hg_scan_matrix(False)).astype(BF16)
    pb = jnp.asarray(_hg_scan_matrix(True)).astype(BF16)
    pshape = pf.shape
    return pl.pallas_call(
        _hgrn_kernel,
        grid=(b, nb),
        in_specs=[pl.BlockSpec(blk, lambda bi, n: (bi, n, 0)),
                  pl.BlockSpec(blk, lambda bi, n: (bi, n, 1)),
                  pl.BlockSpec(blk, lambda bi, n: (bi, n, 3)),
                  pl.BlockSpec(blk, lambda bi, n: (bi, bidx(n), 0)),
                  pl.BlockSpec(blk, lambda bi, n: (bi, bidx(n), 2)),
                  pl.BlockSpec(blk, lambda bi, n: (bi, bidx(n), 3)),
                  pl.BlockSpec((2, HG_WIDTH), lambda bi, n: (0, 0)),
                  pl.BlockSpec(pshape, lambda bi, n: (0, 0)),
                  pl.BlockSpec(pshape, lambda bi, n: (0, 0))],
        out_specs=[pl.BlockSpec(blk, lambda bi, n: (bi, n, 0)),
                   pl.BlockSpec(blk, lambda bi, n: (bi, bidx(n), 0))],
        out_shape=[jax.ShapeDtypeStruct((b, s, HG_WIDTH), F32)] * 2,
        scratch_shapes=[pltpu.VMEM((2 * HG_HEADS, HG_DIM, HG_DIM), F32)],
        compiler_params=_cparams(2, 40),
        name="hgrn",
    )(p_hg, p_hg, p_hg, p_hg, p_hg, p_hg, lb, pf, pb)


def _rope(x, cos, sin_signed, first_half):
    n = x.shape[-1]
    half = AT_DIM // 2
    partner = jnp.where(first_half, pltpu.roll(x, n - half, axis=1), pltpu.roll(x, half, axis=1))
    return x * cos + partner * sin_signed


def _head_norm(x, gain, group_mean):
    ms = _dot((x * x).astype(BF16), group_mean)
    return x * lax.rsqrt(ms + NORM_EPS) * gain


def _attn_kernel(q_ref, k_ref, v_ref, cq_ref, sq_ref, ck_ref, sk_ref, gq_ref, gk_ref, mq_ref, mk_ref,
                 o_ref, k_scr, v_scr, *, q_off, n_ctx_tiles, n_ctx_rows):
    i = pl.program_id(1)

    @pl.when(i == 0)
    def _():
        kr = k_ref[0].astype(F32)
        lane = lax.broadcasted_iota(jnp.int32, kr.shape, 1)
        kn = _head_norm(kr, gk_ref[...], mk_ref[...])
        k_scr[...] = _rope(kn, ck_ref[...], sk_ref[...], (lane % AT_DIM) < AT_DIM // 2).astype(BF16)
        v = v_ref[0]
        one = jnp.ones_like(v)
        v_scr[0] = jnp.where(lane < AT_DIM, v, one)
        v_scr[1] = jnp.where(lane < AT_DIM, one, v)

    qr = q_ref[0].astype(F32)
    lane = lax.broadcasted_iota(jnp.int32, qr.shape, 1)
    qn = _head_norm(qr, gq_ref[...], mq_ref[...])
    qn = _rope(qn, cq_ref[...], sq_ref[...], (lane % AT_DIM) < AT_DIM // 2)
    qn = (qn * (AT_DIM ** -0.5 * math.log2(math.e))).astype(BF16)
    lane_t = lax.broadcasted_iota(jnp.int32, (ROW_TILE, LANE), 1)
    kv0 = lane_t < AT_DIM

    def attend(n_keys):
        keys = k_scr[0:n_keys, :]
        heads = [(j, g) for j in range(AT_WIDTH // LANE) for g in range(AT_KV_HEADS)]

        def scores(j, g):
            qt = qn[:, j * LANE:(j + 1) * LANE]
            return _dot_nt(jnp.where(kv0 if g == 0 else ~kv0, qt, jnp.zeros_like(qt)), keys)

        pending = [scores(*heads[n]) for n in range(AT_LOOKAHEAD)]
        outs = {}
        for n, (j, g) in enumerate(heads):
            s = pending.pop(0)
            if n + AT_LOOKAHEAD < len(heads):
                pending.append(scores(*heads[n + AT_LOOKAHEAD]))
            p = jnp.exp2(s - jnp.max(s, axis=1, keepdims=True))
            outs[g] = _dot(p.astype(BF16), v_scr[g, 0:n_keys, :])
            if g == AT_KV_HEADS - 1:
                num = jnp.where(kv0, outs[0], outs[1])
                den = pltpu.roll(jnp.where(kv0, outs[1], outs[0]), AT_DIM, axis=1)
                o_ref[0, :, j * LANE:(j + 1) * LANE] = (num / den).astype(o_ref.dtype)

    n_all = k_scr.shape[0]
    if q_off < n_ctx_tiles:
        @pl.when(i + q_off < n_ctx_tiles)
        def _():
            attend(n_ctx_rows)

        @pl.when(i + q_off >= n_ctx_tiles)
        def _():
            attend(n_all)
    else:
        attend(n_all)


def _attention(p_rest, col_q, col_k, col_v, tabs, gq, gk, q_off, n_ctx_rows):
    b, s, _ = p_rest.shape
    nt = s // ROW_TILE
    cq, sq, ck, sk = tabs
    mq = jnp.asarray(np.kron(np.eye(AT_HEADS), np.full((AT_DIM, AT_DIM), 1.0 / AT_DIM)), BF16)
    mk = jnp.asarray(np.kron(np.eye(AT_KV_HEADS), np.full((AT_DIM, AT_DIM), 1.0 / AT_DIM)), BF16)
    kern = functools.partial(_attn_kernel, q_off=q_off, n_ctx_tiles=n_ctx_rows // ROW_TILE,
                             n_ctx_rows=n_ctx_rows)
    return pl.pallas_call(
        kern,
        grid=(b, nt - q_off),
        in_specs=[pl.BlockSpec((1, ROW_TILE, AT_WIDTH), lambda bi, i: (bi, i + q_off, col_q)),
                  pl.BlockSpec((1, s, AT_KV_WIDTH), lambda bi, i: (bi, 0, col_k)),
                  pl.BlockSpec((1, s, AT_KV_WIDTH), lambda bi, i: (bi, 0, col_v)),
                  pl.BlockSpec((ROW_TILE, AT_WIDTH), lambda bi, i: (i + q_off, 0)),
                  pl.BlockSpec((ROW_TILE, AT_WIDTH), lambda bi, i: (i + q_off, 0)),
                  pl.BlockSpec((s, AT_KV_WIDTH), lambda bi, i: (0, 0)),
                  pl.BlockSpec((s, AT_KV_WIDTH), lambda bi, i: (0, 0)),
                  pl.BlockSpec((1, AT_WIDTH), lambda bi, i: (0, 0)),
                  pl.BlockSpec((1, AT_KV_WIDTH), lambda bi, i: (0, 0)),
                  pl.BlockSpec((AT_WIDTH, AT_WIDTH), lambda bi, i: (0, 0)),
                  pl.BlockSpec((AT_KV_WIDTH, AT_KV_WIDTH), lambda bi, i: (0, 0))],
        out_specs=pl.BlockSpec((1, ROW_TILE, AT_WIDTH), lambda bi, i: (bi, i + q_off, 0)),
        out_shape=jax.ShapeDtypeStruct((b, s, AT_WIDTH), BF16),
        scratch_shapes=[pltpu.VMEM((s, AT_KV_WIDTH), BF16),
                        pltpu.VMEM((AT_KV_HEADS, s, AT_KV_WIDTH), BF16)],
        compiler_params=_cparams(2, 48),
        name="attention",
    )(p_rest, p_rest, p_rest, cq, sq, ck, sk, gq, gk, mq, mk)


def _rope_tables(n_ctx_rows, n_lat_rows):
    rows = n_lat_rows // GRID_W
    row = jnp.repeat(jnp.arange(rows), GRID_W).astype(F32)
    col = jnp.tile(jnp.arange(GRID_W), rows).astype(F32)
    n_freq = AT_DIM // 4
    inv = ROPE_THETA ** (-jnp.arange(n_freq, dtype=F32) / n_freq)
    ang = jnp.concatenate([row[:, None] * inv, col[:, None] * inv], axis=-1)
    cos = jnp.concatenate([jnp.cos(ang), jnp.cos(ang)], axis=-1)
    sin = jnp.concatenate([-jnp.sin(ang), jnp.sin(ang)], axis=-1)
    cos = jnp.concatenate([jnp.ones((n_ctx_rows, AT_DIM), F32), cos], axis=0)
    sin = jnp.concatenate([jnp.zeros((n_ctx_rows, AT_DIM), F32), sin], axis=0)
    return (jnp.tile(cos, (1, AT_HEADS)), jnp.tile(sin, (1, AT_HEADS)),
            jnp.tile(cos, (1, AT_KV_HEADS)), jnp.tile(sin, (1, AT_KV_HEADS)))


def _shift_matrices(n):
    i = np.arange(n)
    down = i[:, None] - 1 == i[None, :]
    up = i[:, None] + 1 == i[None, :]
    return jnp.asarray(np.stack([down, up]).astype(np.float32), BF16)


def _conv3(xb, shift_ref, prev_row, next_row, w, bias):
    n, c = xb.shape
    sub = 8
    r = lax.broadcasted_iota(jnp.int32, (sub, c), 0)
    x = xb.astype(F32)
    if shift_ref is None:
        xm = pltpu.roll(x, 1, axis=0)
        xp = pltpu.roll(x, n - 1, axis=0)
    else:
        xm = _dot(shift_ref[0], xb)
        xp = _dot(shift_ref[1], xb)
    xm = jnp.concatenate([jnp.where(r == 0, prev_row, xm[0:sub]), xm[sub:]], axis=0)
    xp = jnp.concatenate([xp[:n - sub], jnp.where(r == sub - 1, next_row, xp[n - sub:])], axis=0)
    return xm * w[0:1] + x * w[1:2] + xp * w[2:3] + bias


def _halo_specs(width, col, row_off, n_rows, rows=ROW_TILE):
    per = rows // BF16_SUBLANES
    last = n_rows // BF16_SUBLANES - 1
    return [
        pl.BlockSpec((1, rows, width), lambda bi, i: (bi, i + row_off, col)),
        pl.BlockSpec((1, BF16_SUBLANES, width),
                     lambda bi, i: (bi, jnp.maximum((i + row_off) * per - 1, 0), col)),
        pl.BlockSpec((1, BF16_SUBLANES, width),
                     lambda bi, i: (bi, jnp.minimum((i + row_off + 1) * per, last), col)),
    ]


def _halo_rows(prev_ref, next_ref, is_first, is_last, cols=None):
    sl = slice(None) if cols is None else cols
    prev_row = prev_ref[0, BF16_SUBLANES - 1:BF16_SUBLANES, sl].astype(F32)
    next_row = next_ref[0, 0:1, sl].astype(F32)
    prev_row = jnp.where(is_first, 0.0, prev_row)
    next_row = jnp.where(is_last, 0.0, next_row)
    return prev_row, next_row


def _hypre_kernel(z_ref, zp_ref, zn_ref, sh_ref, w_ref, b_ref, db_ref, u_ref, ud_ref, x0_ref):
    i = pl.program_id(1)
    prev_row, next_row = _halo_rows(zp_ref, zn_ref, i == 0, i == pl.num_programs(1) - 1)
    zc = _conv3(z_ref[0], sh_ref, prev_row, next_row, w_ref[...], b_ref[...])
    x0 = zc[:, :HY_WIDTH]
    x1 = zc[:, HY_WIDTH:2 * HY_WIDTH]
    v = zc[:, 2 * HY_WIDTH:]
    u = v * x1
    u_ref[0] = u.astype(BF16)
    ud_ref[0] = (u * db_ref[...]).astype(BF16)
    x0_ref[0] = x0.astype(BF16)


def _hyena_pre(p_rest, col, row_off, n_rows, conv_w, conv_b, d_bias):
    b, s, _ = p_rest.shape
    width = 3 * HY_WIDTH
    out = jax.ShapeDtypeStruct((b, n_rows, HY_WIDTH), BF16)
    ospec = pl.BlockSpec((1, ROW_TILE, HY_WIDTH), lambda bi, i: (bi, i, 0))
    return pl.pallas_call(
        _hypre_kernel,
        grid=(b, n_rows // ROW_TILE),
        in_specs=_halo_specs(width, col, row_off, s) + [
            pl.BlockSpec((2, ROW_TILE, ROW_TILE), lambda bi, i: (0, 0, 0)),
            pl.BlockSpec((3, width), lambda bi, i: (0, 0)),
            pl.BlockSpec((1, width), lambda bi, i: (0, 0)),
            pl.BlockSpec((1, HY_WIDTH), lambda bi, i: (0, 0))],
        out_specs=[ospec, ospec, ospec],
        out_shape=[out, out, out],
        compiler_params=_cparams(2, 32),
        name="hyena_pre",
    )(p_rest, p_rest, p_rest, _shift_matrices(ROW_TILE), conv_w, conv_b.reshape(1, width),
      d_bias.reshape(1, HY_WIDTH))


def _hyfilt_kernel(z_ref, t_ref, dl_ref, w1_ref, b1_ref, wi_ref, bi_ref, fr_ref, wl_ref, o_ref):
    fr = fr_ref[...]
    h = jnp.sin(fr * (_dot3(z_ref[...], w1_ref[...]) + b1_ref[...]))
    for j in range(HY_INNER):
        h = jnp.sin(fr * (_dot3(h, wi_ref[j]) + bi_ref[j]))
    h = _dot3(h, wl_ref[...])
    decay = jnp.exp(-t_ref[...] * dl_ref[...])
    hf = h[:, :HY_WIDTH] * decay
    hb = h[:, HY_WIDTH:] * decay
    o_ref[...] = jnp.concatenate([hf + hb, hf - hb], axis=1)


def _pad2(a, rows, cols):
    return jnp.pad(a, ((0, rows - a.shape[0]), (0, cols - a.shape[1])))


def _hyena_filter_sums(n, w1, b1, wi, bi, freq, w_last):
    t = jnp.linspace(0.0, 1.0, n, dtype=F32)[:, None]
    w = 2.0 * math.pi * jnp.arange(n, dtype=F32)[:, None] / n
    f = jnp.linspace(1e-4, HY_BANDS - 1, HY_BANDS, dtype=F32)[None, :]
    z = jnp.concatenate([t, jnp.cos(f * w), -jnp.sin(f * w)], axis=-1)
    max_decay = math.log(HY_TARGET) / HY_FAST_DECAY
    min_decay = math.log(HY_TARGET) / HY_SLOW_DECAY
    deltas = jnp.abs(jnp.linspace(min_decay, max_decay, HY_WIDTH, dtype=F32))[None, :]
    zp = _pad2(z, n, LANE)
    w1p = _pad2(w1, LANE, LANE)
    b1p = _pad2(b1[None, :], 1, LANE)
    wip = jnp.stack([_pad2(wi[j], LANE, LANE) for j in range(HY_INNER)])
    bip = jnp.stack([_pad2(bi[j][None, :], 1, LANE) for j in range(HY_INNER)])
    frp = _pad2(freq[None, :], 1, LANE)
    wlp = _pad2(w_last, LANE, 2 * HY_WIDTH)
    tr = min(n, ROW_TILE)
    full = lambda shape: pl.BlockSpec(shape, lambda i: (0,) * len(shape))
    return pl.pallas_call(
        _hyfilt_kernel,
        grid=(n // tr,),
        in_specs=[pl.BlockSpec((tr, LANE), lambda i: (i, 0)),
                  pl.BlockSpec((tr, 1), lambda i: (i, 0)),
                  full((1, HY_WIDTH)), full((LANE, LANE)), full((1, LANE)),
                  full((HY_INNER, LANE, LANE)), full((HY_INNER, 1, LANE)), full((1, LANE)),
                  full((LANE, 2 * HY_WIDTH))],
        out_specs=pl.BlockSpec((tr, 2 * HY_WIDTH), lambda i: (i, 0)),
        out_shape=jax.ShapeDtypeStruct((n, 2 * HY_WIDTH), F32),
        compiler_params=_cparams(1, 32),
        name="hyena_filter",
    )(zp, t, deltas, w1p, b1p, wip, bip, frp, wlp)


def _dft_tables(n):
    f = jnp.arange(n, dtype=jnp.int32)[:, None]
    t = jnp.arange(n, dtype=jnp.int32)[None, :]
    ang = ((f * t) % (2 * n)).astype(F32) * (math.pi / n)
    nyq = jnp.where(t % 2 == 0, 1.0, -1.0).astype(F32)
    return jnp.stack([jnp.cos(ang), jnp.where(f == 0, nyq, jnp.sin(ang))])


def _hyfwd_kernel(u_ref, f_ref, co_ref, o_ref):
    u = u_ref[0]
    ure = _dot(f_ref[0], u)
    uim = _dot(f_ref[1], u)
    o_ref[0, 0] = (ure * co_ref[0] - uim * co_ref[1]).astype(BF16)
    o_ref[0, 1] = (ure * co_ref[2] + uim * co_ref[3]).astype(BF16)


def _hyinv_kernel(y_ref, ft_ref, ud_ref, x0_ref, o_ref):
    y = _dot(ft_ref[0], y_ref[0, 0]) + _dot(ft_ref[1], y_ref[0, 1])
    o_ref[0] = ((y + ud_ref[0].astype(F32)) * x0_ref[0].astype(F32)).astype(BF16)


def _hyena_conv(u, ud, x0, tables, coef):
    b, n, c = u.shape
    tf = min(n, DFT_ROWS)
    f_bf = tables.astype(BF16)
    ft_bf = jnp.swapaxes(tables, 1, 2).astype(BF16)
    spec = pl.pallas_call(
        _hyfwd_kernel,
        grid=(n // tf, b),
        in_specs=[pl.BlockSpec((1, n, c), lambda j, bi: (bi, 0, 0)),
                  pl.BlockSpec((2, tf, n), lambda j, bi: (0, j, 0)),
                  pl.BlockSpec((4, tf, c), lambda j, bi: (0, j, 0))],
        out_specs=pl.BlockSpec((1, 2, tf, c), lambda j, bi: (bi, 0, j, 0)),
        out_shape=jax.ShapeDtypeStruct((b, 2, n, c), BF16),
        compiler_params=_cparams(2, 40),
        name="hyena_dft",
    )(u, f_bf, coef)
    return pl.pallas_call(
        _hyinv_kernel,
        grid=(b, n // tf),
        in_specs=[pl.BlockSpec((1, 2, n, c), lambda bi, j: (bi, 0, 0, 0)),
                  pl.BlockSpec((2, tf, n), lambda bi, j: (0, j, 0)),
                  pl.BlockSpec((1, tf, c), lambda bi, j: (bi, j, 0)),
                  pl.BlockSpec((1, tf, c), lambda bi, j: (bi, j, 0))],
        out_specs=pl.BlockSpec((1, tf, c), lambda bi, j: (bi, j, 0)),
        out_shape=jax.ShapeDtypeStruct((b, n, c), BF16),
        compiler_params=_cparams(2, 40),
        name="hyena_idft",
    )(spec, ft_bf, ud, x0)


def _hyena_coef(tables, hsum_hdiff):
    n = tables.shape[1]
    c = HY_WIDTH
    r = _matmul_f32(tables.reshape(2 * n, n), hsum_hdiff, min(n, ROW_TILE), "hyena_kernel_dft")
    k_re = r[:n, :c]
    k_im = r[n:, c:]
    k_nyq = r[n:n + 1, :c]
    first = (jnp.arange(n) == 0)[:, None]
    scale = jnp.where(first, 1.0 / (2 * n), 2.0 / (2 * n)).astype(F32)
    zero = jnp.zeros_like(k_im)
    return jnp.stack([k_re * scale,
                      jnp.where(first, zero, k_im * scale),
                      jnp.where(first, zero, k_im * scale),
                      jnp.where(first, k_nyq, k_re) * scale])


def _merge_kernel(*refs, n_ctx_tiles, row_off, has_ctx):
    if has_ctx:
        (of_ref, ob_ref, zg_ref, att_ref, cx_ref, cc_ref, gate_ref, x_ref, m_ref, ghg_ref, gpost_ref,
         gffn_ref, woa_ref, wob_ref, woc_ref, wout_ref, o_ref, h_ref) = refs
    else:
        (of_ref, ob_ref, zg_ref, att_ref, cx_ref, gate_ref, x_ref, m_ref, ghg_ref, gpost_ref,
         gffn_ref, woa_ref, wob_ref, woc_ref, wout_ref, o_ref, h_ref) = refs
    o = of_ref[0] + ob_ref[0]
    ghg = ghg_ref[...]
    a = jnp.concatenate([_rms(o[:, h * HG_DIM:(h + 1) * HG_DIM], ghg) for h in range(HG_HEADS)], axis=1)
    zg = zg_ref[0]
    a = a * (zg * _sigmoid(zg))
    c = cx_ref[0]
    if has_ctx:
        c = jnp.where(pl.program_id(1) + row_off < n_ctx_tiles, cc_ref[0], c)
    d = x_ref.shape[-1]
    ya = _dot(a.astype(BF16), woa_ref[...])
    yb = _dot(att_ref[0], wob_ref[...])
    yc = _dot(c, woc_ref[...])
    m = (_sigmoid(gate_ref[0, :, 0:d].astype(F32)) * ya
         + _sigmoid(gate_ref[0, :, d:2 * d].astype(F32)) * yb
         + _sigmoid(gate_ref[0, :, 2 * d:3 * d].astype(F32)) * yc)
    y = _dot(m.astype(BF16), wout_ref[...])
    x_new = x_ref[0] + m_ref[0, 2:3, :] * _rms(y, gpost_ref[...])
    o_ref[0] = x_new
    h_ref[0] = (_rms(x_new, gffn_ref[...]) * (1.0 + m_ref[0, 4:5, :]) + m_ref[0, 3:4, :]).astype(BF16)


def _merge(o_f, o_b, p_hg, att, c_x, c_c, p_rest, xs, mods, g_hg, g_post, g_ffn, w_oa, w_ob, w_oc, w_out,
           row_off, n_ctx_rows):
    b, s, d = xs.shape
    nct = n_ctx_rows // ROW_TILE
    n_tiles = s // ROW_TILE - row_off
    has_ctx = c_c is not None
    ctx_row = b

    def stream(width, col=0):
        return pl.BlockSpec((1, ROW_TILE, width), lambda bi, i: (bi, i + row_off, col))

    def full(shape):
        return pl.BlockSpec(shape, lambda bi, i: (0,) * len(shape))

    in_specs = [stream(HG_WIDTH), stream(HG_WIDTH), stream(HG_WIDTH, 4), stream(AT_WIDTH),
                pl.BlockSpec((1, ROW_TILE, HY_WIDTH),
                             lambda bi, i: (bi, jnp.maximum(i + row_off - nct, 0), 0))]
    args = [o_f, o_b, p_hg, att, c_x]
    if has_ctx:
        in_specs.append(pl.BlockSpec((1, ROW_TILE, HY_WIDTH),
                                     lambda bi, i: (bi, jnp.minimum(i + row_off, nct - 1), 0)))
        args.append(c_c)
    in_specs += [stream(3 * d), stream(d),
                 pl.BlockSpec((1, 6, d), lambda bi, i: (jnp.where(i + row_off < nct, ctx_row, bi), 0, 0)),
                 full((1, HG_DIM)), full((1, d)), full((1, d)),
                 full((HG_WIDTH, d)), full((AT_WIDTH, d)), full((HY_WIDTH, d)), full((d, d))]
    args += [p_rest, xs, mods, g_hg.reshape(1, HG_DIM), g_post.reshape(1, d), g_ffn.reshape(1, d),
             w_oa, w_ob, w_oc, w_out]
    ospec = pl.BlockSpec((1, ROW_TILE, d), lambda bi, i: (bi, i, 0))
    return pl.pallas_call(
        functools.partial(_merge_kernel, n_ctx_tiles=nct, row_off=row_off, has_ctx=has_ctx),
        grid=(b, n_tiles),
        in_specs=in_specs,
        out_specs=[ospec, ospec],
        out_shape=[jax.ShapeDtypeStruct((b, n_tiles * ROW_TILE, d), F32),
                   jax.ShapeDtypeStruct((b, n_tiles * ROW_TILE, d), BF16)],
        compiler_params=_cparams(2, 48),
        name="merge",
    )(*args)


FFN_COLS = 256
FFN_DOWN_GROUPS = 2


def _ffn_kernel(*refs, first_tiles, last_tiles, has_next):
    if has_next:
        (h_ref, hp_ref, hn_ref, wu_ref, w_ref, b_ref, x_ref, m_ref, g_ref, wd_ref, mn_ref, gn_ref,
         o_ref, hx_ref, act_ref) = refs
    else:
        h_ref, hp_ref, hn_ref, wu_ref, w_ref, b_ref, x_ref, m_ref, g_ref, wd_ref, o_ref, act_ref = refs
    i = pl.program_id(1)
    is_first = functools.reduce(jnp.logical_or, [i == t for t in first_tiles])
    is_last = functools.reduce(jnp.logical_or, [i == t for t in last_tiles])
    d_ff = wd_ref.shape[0]
    rows = h_ref.shape[1]
    halo = BF16_SUBLANES
    ext = rows + 2 * halo
    hp = jnp.where(is_first, jnp.zeros_like(hp_ref[0]), hp_ref[0])
    hn = jnp.where(is_last, jnp.zeros_like(hn_ref[0]), hn_ref[0])
    h_ext = jnp.concatenate([hp, h_ref[0], hn], axis=0)

    def up(j):
        return [_dot(h_ext, wu_ref[:, base + j * FFN_COLS:base + (j + 1) * FFN_COLS]) for base in (0, d_ff)]

    def conv(u, cols):
        w = w_ref[:, cols]
        full = pltpu.roll(u, 1, axis=0) * w[0:1] + u * w[1:2] + pltpu.roll(u, ext - 1, axis=0) * w[2:3]
        return full[halo:halo + rows] + b_ref[:, cols]

    n_chunks = d_ff // FFN_COLS
    per_group = -(-n_chunks // FFN_DOWN_GROUPS)
    acc = None
    u_next = up(0)
    for j in range(n_chunks):
        u = u_next
        if j + 1 < n_chunks:
            u_next = up(j + 1)
        a = conv(u[0], slice(j * FFN_COLS, (j + 1) * FFN_COLS))
        g = conv(u[1], slice(d_ff + j * FFN_COLS, d_ff + (j + 1) * FFN_COLS))
        act_ref[:, j * FFN_COLS:(j + 1) * FFN_COLS] = (a * _sigmoid(a) * g).astype(BF16)
        if (j + 1) % per_group == 0 or j + 1 == n_chunks:
            lo = (j // per_group) * per_group * FFN_COLS
            part = _dot(act_ref[:, lo:(j + 1) * FFN_COLS], wd_ref[lo:(j + 1) * FFN_COLS, :])
            acc = part if acc is None else acc + part
    x_new = x_ref[0] + m_ref[0, 5:6, :] * _rms(acc, g_ref[...])
    o_ref[0] = x_new
    if has_next:
        hx_ref[0] = (_rms(x_new, gn_ref[...]) * (1.0 + mn_ref[0, 1:2, :]) + mn_ref[0, 0:1, :]).astype(BF16)


def _ffn(h, xs, mods, w_up, conv_w, conv_b, g_post, w_down, n_ctx_rows, mods_next=None, g_next=None):
    b, s, d = xs.shape
    d_ff = w_down.shape[0]
    rows = FFN_ROWS if (n_ctx_rows % FFN_ROWS == 0 and s % FFN_ROWS == 0) else ROW_TILE
    nt = s // rows
    nct = n_ctx_rows // rows
    first_tiles = tuple(sorted({0, nct}))
    last_tiles = tuple(sorted({nct - 1, nt - 1} - {-1}))
    ctx_row = b
    has_next = mods_next is not None
    full = lambda shape: pl.BlockSpec(shape, lambda bi, i: (0,) * len(shape))
    mspec = pl.BlockSpec((1, 6, d), lambda bi, i: (jnp.where(i < nct, ctx_row, bi), 0, 0))
    ospec = pl.BlockSpec((1, rows, d), lambda bi, i: (bi, i, 0))
    resident = lambda shape: pl.BlockSpec(shape, lambda bi, i: (0,) * len(shape), pipeline_mode=pl.Buffered(1))
    in_specs = _halo_specs(d, 0, 0, s, rows) + [
        resident((d, 2 * d_ff)), full((3, 2 * d_ff)), full((1, 2 * d_ff)),
        ospec, mspec, full((1, d)), resident((d_ff, d))]
    args = [h, h, h, w_up, conv_w, conv_b.reshape(1, 2 * d_ff), xs, mods, g_post.reshape(1, d), w_down]
    out_specs = [ospec]
    out_shape = [jax.ShapeDtypeStruct((b, s, d), F32)]
    if has_next:
        in_specs += [mspec, full((1, d))]
        args += [mods_next, g_next.reshape(1, d)]
        out_specs.append(ospec)
        out_shape.append(jax.ShapeDtypeStruct((b, s, d), BF16))
    return pl.pallas_call(
        functools.partial(_ffn_kernel, first_tiles=first_tiles, last_tiles=last_tiles, has_next=has_next),
        grid=(b, nt),
        in_specs=in_specs,
        out_specs=out_specs,
        out_shape=out_shape,
        scratch_shapes=[pltpu.VMEM((rows, d_ff), BF16)],
        compiler_params=_cparams(2, 56),
        name="ffn",
    )(*args)


def _deinterleave():
    return np.concatenate([np.arange(0, AT_DIM, 2), np.arange(1, AT_DIM, 2)])


def _q_head_order():
    return [h for j in range(AT_GROUP) for h in (j, AT_GROUP + j)]


def _largest_tile(n, cap):
    best = LANE
    for t in range(LANE, cap + 1, LANE):
        if n % t == 0:
            best = t
    return best


def kernel(x, c, ctx, c_ctx, w_ada, b_ada, g_pre_mix, g_post_mix, g_pre_ffn, g_post_ffn, w_in, hg_lower_bounds, hg_norm, q_norm, k_norm, hy_conv_w, hy_conv_b, hy_w1, hy_b1, hy_wi, hy_bi, hy_freq, hy_w_last, hy_bias, w_oa, w_ob, w_oc, w_out, w_up, ffn_conv_w, ffn_conv_b, w_down):
    bsz, n_lat, d = x.shape
    n_ctx = ctx.shape[1]
    depth = w_ada.shape[0]
    d_ff = w_down.shape[1]
    assert AT_KV_HEADS == 2 and AT_GROUP * LANE == AT_WIDTH and AT_KV_WIDTH == LANE
    assert n_ctx % ROW_TILE == 0 and n_lat % ROW_TILE == 0 and n_lat % GRID_W == 0
    assert (bsz * (n_ctx + n_lat)) % MM_ROWS == 0 and (bsz * n_lat) % MM_ROWS == 0

    lbp = jax.nn.softmax(hg_lower_bounds.astype(F32), axis=0)
    lower = jnp.cumsum(lbp, axis=0) - lbp[0]

    rp = -(-(bsz + 1) // 8) * 8
    src = jnp.concatenate([c, c_ctx[None, :], jnp.zeros((rp - bsz - 1, d), F32)], axis=0)
    mods_all = _ada(src, w_ada, b_ada).reshape(depth, rp, 6, d)

    o_q = 5 * HG_WIDTH
    o_k = o_q + AT_WIDTH
    o_v = o_k + AT_KV_WIDTH
    o_hy = o_v + AT_KV_WIDTH
    o_gate = o_hy + 3 * HY_WIDTH
    deint = _deinterleave()
    q_cols = np.concatenate([o_q + h * AT_DIM + deint for h in _q_head_order()])
    k_cols = np.concatenate([o_k + g * AT_DIM + deint for g in range(AT_KV_HEADS)])
    rest_cols = np.concatenate([np.arange(o_gate, o_gate + 3 * d), np.arange(o_hy, o_hy + 3 * HY_WIDTH),
                                q_cols, k_cols, np.arange(o_v, o_v + AT_KV_WIDTH)])
    col_hy = (3 * d) // (3 * HY_WIDTH)
    col_q = (3 * d + 3 * HY_WIDTH) // AT_WIDTH
    col_k = (3 * d + 3 * HY_WIDTH + AT_WIDTH) // AT_KV_WIDTH
    col_v = col_k + 1
    assert (3 * d) % (3 * HY_WIDTH) == 0 and (3 * d + 3 * HY_WIDTH) % AT_WIDTH == 0
    ob_rows = np.concatenate([np.arange(h * AT_DIM, (h + 1) * AT_DIM) for h in _q_head_order()])

    rope_tabs = _rope_tables(n_ctx, n_lat)
    dft_lat = _dft_tables(n_lat)
    dft_ctx = _dft_tables(n_ctx)
    nct = n_ctx // ROW_TILE

    s_all = n_ctx + n_lat
    xs, h = _join_modulate(ctx, x, mods_all[0], g_pre_mix[0], 0, 1)
    for l in range(depth):
        need_ctx = l < depth - 1
        mods = mods_all[l]
        w_hg = w_in[l][:, :5 * HG_WIDTH].astype(BF16)
        w_rest = w_in[l][:, rest_cols].astype(BF16)

        h = h.reshape(bsz * s_all, d)
        p_hg = _matmul(h, w_hg, F32, _largest_tile(5 * HG_WIDTH, 1280), "proj_hgrn").reshape(bsz, s_all, -1)
        p_rest = _matmul(h, w_rest, BF16, _largest_tile(w_rest.shape[1], 1792), "proj_rest").reshape(bsz, s_all, -1)

        o_f, o_b = _hgrn(p_hg, lower[l], n_ctx)

        gq = jnp.tile(q_norm[l][deint], AT_HEADS)[None, :]
        gk = jnp.tile(k_norm[l][deint], AT_KV_HEADS)[None, :]
        row_off = 0 if need_ctx else nct
        att = _attention(p_rest, col_q, col_k, col_v, rope_tabs, gq, gk, row_off, n_ctx)

        filt_args = (hy_w1[l], hy_b1[l], hy_wi[l], hy_bi[l], hy_freq[l], hy_w_last[l])
        coef = _hyena_coef(dft_lat, _hyena_filter_sums(n_lat, *filt_args))
        c_x = _hyena_conv(*_hyena_pre(p_rest, col_hy, nct, n_lat, hy_conv_w[l], hy_conv_b[l], hy_bias[l]),
                          dft_lat, coef)
        c_c = None
        if need_ctx:
            coef_c = _hyena_coef(dft_ctx, _hyena_filter_sums(n_ctx, *filt_args))
            c_c = _hyena_conv(*_hyena_pre(p_rest, col_hy, 0, n_ctx, hy_conv_w[l], hy_conv_b[l], hy_bias[l]),
                              dft_ctx, coef_c)

        xs, h2 = _merge(o_f, o_b, p_hg, att, c_x, c_c, p_rest, xs, mods, hg_norm[l], g_post_mix[l],
                        g_pre_ffn[l], w_oa[l].astype(BF16), w_ob[l][ob_rows].astype(BF16),
                        w_oc[l].astype(BF16), w_out[l].astype(BF16), row_off, n_ctx)
        n_ctx_now = n_ctx if need_ctx else 0
        ffn_args = (h2, xs, mods, w_up[l].astype(BF16), ffn_conv_w[l], ffn_conv_b[l], g_post_ffn[l],
                    w_down[l].astype(BF16), n_ctx_now)
        if need_ctx:
            xs, h = _ffn(*ffn_args, mods_all[l + 1], g_pre_mix[l + 1])
        else:
            xs, = _ffn(*ffn_args)
    return xs
```

```python
import functools
import math

import jax
import jax.numpy as jnp
import numpy as np
from jax import lax
from jax.experimental import pallas as pl
from jax.experimental.pallas import tpu as pltpu

F32 = jnp.float32
BF16 = jnp.bfloat16

NORM_EPS = 1e-6
GRID_W = 64
HG_HEADS = 4
HG_DIM = 128
HG_WIDTH = HG_HEADS * HG_DIM
HG_EXP_CLIP = 30.0
AT_HEADS = 8
AT_KV_HEADS = 2
AT_DIM = 64
AT_GROUP = AT_HEADS // AT_KV_HEADS
AT_WIDTH = AT_HEADS * AT_DIM
AT_KV_WIDTH = AT_KV_HEADS * AT_DIM
ROPE_THETA = 10000.0
HY_WIDTH = 512
HY_EMB_DIM = 33
HY_BANDS = (HY_EMB_DIM - 1) // 2
HY_FILTER_WIDTH = 64
HY_INNER = 2
HY_FAST_DECAY = 0.3
HY_SLOW_DECAY = 1.5
HY_TARGET = 1e-2

LANE = 128
BF16_SUBLANES = 16
ROW_TILE = 256
MERGE_SAMPLES = 2
HG_CHUNK = 128
HG_BLOCK = 256
HG_LEVELS = tuple(HG_CHUNK >> (j + 1) for j in range(int(math.log2(HG_CHUNK))))
MM_ROWS = 1024
DFT_ROWS = 512
AT_LOOKAHEAD = 2
FFN_ROWS = 512
VMEM_CAP = 56 * 1024 * 1024


def _cparams(n_axes, vmem_mb):
    return pltpu.CompilerParams(
        dimension_semantics=("arbitrary",) * n_axes,
        vmem_limit_bytes=min(int(vmem_mb) * 1024 * 1024, VMEM_CAP))


def _dot(a, b):
    return jnp.dot(a, b, preferred_element_type=F32)


def _dot_nt(a, b):
    return lax.dot_general(a, b, (((1,), (1,)), ((), ())), preferred_element_type=F32)


def _split_bf16(a):
    hi = a.astype(BF16)
    lo = (a - hi.astype(F32)).astype(BF16)
    return hi, lo


def _dot3(a, b):
    ah, al = _split_bf16(a)
    bh, bl = _split_bf16(b)
    return _dot(ah, bh) + (_dot(ah, bl) + _dot(al, bh))


def _rms(x, g):
    return x * lax.rsqrt(jnp.mean(x * x, axis=-1, keepdims=True) + NORM_EPS) * g


def _sigmoid(x):
    return 0.5 * jnp.tanh(0.5 * x) + 0.5


def _ada_kernel(src_ref, w_ref, b_ref, o_ref):
    s = src_ref[...]
    s = s * _sigmoid(s)
    o_ref[0] = _dot3(s, w_ref[0]) + b_ref[0]


def _ada(src, w_ada, b_ada):
    depth, d, d6 = w_ada.shape
    rp = src.shape[0]
    tn = d
    return pl.pallas_call(
        _ada_kernel,
        grid=(depth, d6 // tn),
        in_specs=[pl.BlockSpec((rp, d), lambda l, j: (0, 0)),
                  pl.BlockSpec((1, d, tn), lambda l, j: (l, 0, j)),
                  pl.BlockSpec((1, 1, tn), lambda l, j: (l, 0, j))],
        out_specs=pl.BlockSpec((1, rp, tn), lambda l, j: (l, 0, j)),
        out_shape=jax.ShapeDtypeStruct((depth, rp, d6), F32),
        compiler_params=_cparams(2, 32),
        name="ada",
    )(src, w_ada, b_ada.reshape(depth, 1, d6))


def _mod_kernel(c_ref, x_ref, m_ref, g_ref, xs_ref, o_ref, *, k_shift, k_scale, n_ctx_tiles):
    x = jnp.where(pl.program_id(1) < n_ctx_tiles, c_ref[0], x_ref[0])
    xs_ref[0] = x
    shift = m_ref[0, k_shift:k_shift + 1, :]
    scale = m_ref[0, k_scale:k_scale + 1, :]
    o_ref[0] = (_rms(x, g_ref[...]) * (1.0 + scale) + shift).astype(o_ref.dtype)


def _join_modulate(ctx, x, mods, g, k_shift, k_scale):
    b, n_lat, d = x.shape
    n_ctx_tiles = ctx.shape[1] // ROW_TILE
    nt = n_ctx_tiles + n_lat // ROW_TILE
    ctx_row = b
    ospec = pl.BlockSpec((1, ROW_TILE, d), lambda bi, i: (bi, i, 0))
    return pl.pallas_call(
        functools.partial(_mod_kernel, k_shift=k_shift, k_scale=k_scale, n_ctx_tiles=n_ctx_tiles),
        grid=(b, nt),
        in_specs=[pl.BlockSpec((1, ROW_TILE, d), lambda bi, i: (bi, jnp.minimum(i, n_ctx_tiles - 1), 0)),
                  pl.BlockSpec((1, ROW_TILE, d), lambda bi, i: (bi, jnp.maximum(i - n_ctx_tiles, 0), 0)),
                  pl.BlockSpec((1, 6, d), lambda bi, i: (jnp.where(i < n_ctx_tiles, ctx_row, bi), 0, 0)),
                  pl.BlockSpec((1, d), lambda bi, i: (0, 0))],
        out_specs=[ospec, ospec],
        out_shape=[jax.ShapeDtypeStruct((b, nt * ROW_TILE, d), F32),
                   jax.ShapeDtypeStruct((b, nt * ROW_TILE, d), BF16)],
        compiler_params=_cparams(2, 16),
        name="modulate",
    )(ctx, x, mods, g.reshape(1, d))


def _mm_kernel(a_ref, b_ref, o_ref):
    o_ref[...] = _dot(a_ref[...], b_ref[...]).astype(o_ref.dtype)


def _matmul(a, w, out_dtype, tn, name):
    m, k = a.shape
    n = w.shape[1]
    tm = MM_ROWS
    assert m % tm == 0 and n % tn == 0
    return pl.pallas_call(
        _mm_kernel,
        grid=(m // tm, n // tn),
        in_specs=[pl.BlockSpec((tm, k), lambda i, j: (i, 0)),
                  pl.BlockSpec((k, tn), lambda i, j: (0, j))],
        out_specs=pl.BlockSpec((tm, tn), lambda i, j: (i, j)),
        out_shape=jax.ShapeDtypeStruct((m, n), out_dtype),
        compiler_params=_cparams(2, 48),
        name=name,
    )(a, w)


def _mm3_kernel(a_ref, b_ref, o_ref):
    o_ref[...] = _dot3(a_ref[...], b_ref[...])


def _matmul_f32(a, w, tm, name):
    m, k = a.shape
    n = w.shape[1]
    return pl.pallas_call(
        _mm3_kernel,
        grid=(m // tm,),
        in_specs=[pl.BlockSpec((tm, k), lambda i: (i, 0)),
                  pl.BlockSpec((k, n), lambda i: (0, 0))],
        out_specs=pl.BlockSpec((tm, n), lambda i: (i, 0)),
        out_shape=jax.ShapeDtypeStruct((m, n), F32),
        compiler_params=_cparams(1, 48),
        name=name,
    )(a, w)


def _hg_scan_matrix(reverse):
    t_n = HG_CHUNK
    t = np.arange(t_n)[:, None]
    u = np.arange(t_n)[None, :]
    rows = [(u >= t) if reverse else (u <= t)]
    for w in HG_LEVELS:
        base = (t // (2 * w)) * (2 * w)
        mid = base + w
        upper = (t - base) >= w
        if reverse:
            m = np.where(upper, (u >= mid) & (u < t), (u >= t) & (u < mid))
        else:
            m = np.where(upper, (u >= mid) & (u <= t), (u > t) & (u < mid))
        rows.append(m)
    m = np.concatenate(rows, axis=0).astype(np.float32)
    return np.concatenate([m, m], axis=1)


def _hgrn_kernel(qf_ref, zf_ref, vf_ref, qb_ref, zb_ref, vb_ref, lb_ref, pf_ref, pb_ref,
                 of_ref, ob_ref, s_ref):
    @pl.when(pl.program_id(1) == 0)
    def _():
        s_ref[...] = jnp.zeros_like(s_ref)

    t_n = HG_CHUNK
    ti = lax.broadcasted_iota(jnp.int32, (t_n, t_n), 0)
    si = lax.broadcasted_iota(jnp.int32, (t_n, t_n), 1)
    tx = ti ^ si
    dirs = ((qf_ref, zf_ref, vf_ref, pf_ref, of_ref), (qb_ref, zb_ref, vb_ref, pb_ref, ob_ref))
    masks = []
    for reverse in (False, True):
        later = (ti < si) if reverse else (ti > si)
        masks.append([later & (tx >= w) & (tx < 2 * w) for w in HG_LEVELS])
    n_sub = qf_ref.shape[1] // t_n
    heads = range(HG_HEADS)
    units = [(d, c, h) for d in range(2) for c in range(n_sub) for h in heads]
    cols = lambda h: slice(h * HG_DIM, (h + 1) * HG_DIM)
    rows = lambda c: slice(c * t_n, (c + 1) * t_n)

    kk, kb, qb, cat, x, ex = {}, {}, {}, {}, {}, {}
    a = {u: jnp.zeros((t_n, t_n), F32) for u in units}

    def gates(d, c, h):
        z = dirs[d][1][0, rows(c), cols(h)]
        lb = lb_ref[d:d + 1, cols(h)]
        e = jnp.exp(-jnp.abs(z))
        log_num = jnp.log(jnp.where(z >= 0.0, 1.0 + lb * e, e + lb))
        log_clip = z + jnp.log(1.0 + lb * math.exp(HG_EXP_CLIP))
        lf = jnp.where(z < -HG_EXP_CLIP, log_clip, log_num) - jnp.log(1.0 + e)
        half = 0.5 * (1.0 - lb)
        kk[d, c, h] = half - half * jnp.tanh(0.5 * z)
        kb[d, c, h] = kk[d, c, h].astype(BF16)
        qb[d, c, h] = dirs[d][0][0, rows(c), cols(h)].astype(BF16)
        hi, lo = _split_bf16(lf)
        cat[d, c, h] = jnp.concatenate([hi, lo], axis=0)

    def exponents(d):
        rhs = jnp.concatenate([cat[d, c, h] for c in range(n_sub) for h in heads], axis=1)
        xd = _dot(dirs[d][3][...], rhs)
        ed = jnp.exp(xd[t_n:]).astype(BF16)
        for c in range(n_sub):
            for h in heads:
                x[d, c, h] = xd[0:t_n, cols(c * HG_HEADS + h)]
                ex[d, c, h] = ed[:, cols(c * HG_HEADS + h)]

    def level(d, c, j):
        for h in heads:
            u = (d, c, h)
            ew = ex[u][j * t_n:(j + 1) * t_n]
            pw = _dot_nt(qb[u] * ew, kb[u] * ew)
            a[u] = jnp.where(masks[d][j], pw, a[u])

    def finish(d, c, h):
        u = (d, c, h)
        q = dirs[d][0][0, rows(c), cols(h)]
        v = dirs[d][2][0, rows(c), cols(h)]
        g = x[u]
        g_last = g[0:1] if d == 1 else g[t_n - 1:t_n]
        st = s_ref[d * HG_HEADS + h]
        dqk = jnp.sum(q * kk[u], axis=1, keepdims=True)
        o = (_dot(a[u].astype(BF16), v.astype(BF16)) + dqk * v
             + _dot_nt((q * jnp.exp(g)).astype(BF16), st.astype(BF16)))
        kd = (kk[u] * jnp.exp(g_last - g)).astype(BF16)
        dirs[d][4][0, rows(c), cols(h)] = o
        s_ref[d * HG_HEADS + h] = st * jnp.exp(g_last) + _dot(v.T.astype(BF16), kd)

    for d in range(2):
        for c in range(n_sub):
            for h in heads:
                gates(d, c, h)
        exponents(d)
        for j in range(len(HG_LEVELS)):
            for c in range(n_sub):
                level(d, c, j)
        for c in (range(n_sub) if d == 0 else reversed(range(n_sub))):
            for h in heads:
                finish(d, c, h)


def _hgrn(p_hg, lb, n_ctx_rows):
    b, s, _ = p_hg.shape
    assert s % HG_BLOCK == 0 and n_ctx_rows % HG_BLOCK == 0
    nb = s // HG_BLOCK
    nc = n_ctx_rows // HG_BLOCK

    def bidx(n):
        return jnp.where(n < nc, nc - 1 - n, nb - 1 - (n - nc))

    blk = (1, HG_BLOCK, HG_WIDTH)
    pf = jnp.asarray(_hg_scan_matrix(False)).astype(BF16)
    pb = jnp.asarray(_hg_scan_matrix(True)).astype(BF16)
    pshape = pf.shape
    return pl.pallas_call(
        _hgrn_kernel,
        grid=(b, nb),
        in_specs=[pl.BlockSpec(blk, lambda bi, n: (bi, n, 0)),
                  pl.BlockSpec(blk, lambda bi, n: (bi, n, 1)),
                  pl.BlockSpec(blk, lambda bi, n: (bi, n, 3)),
                  pl.BlockSpec(blk, lambda bi, n: (bi, bidx(n), 0)),
                  pl.BlockSpec(blk, lambda bi, n: (bi, bidx(n), 2)),
                  pl.BlockSpec(blk, lambda bi, n: (bi, bidx(n), 3)),
                  pl.BlockSpec((2, HG_WIDTH), lambda bi, n: (0, 0)),
                  pl.BlockSpec(pshape, lambda bi, n: (0, 0)),
                  pl.BlockSpec(pshape, lambda bi, n: (0, 0))],
        out_specs=[pl.BlockSpec(blk, lambda bi, n: (bi, n, 0)),
                   pl.BlockSpec(blk, lambda bi, n: (bi, bidx(n), 0))],
        out_shape=[jax.ShapeDtypeStruct((b, s, HG_WIDTH), F32)] * 2,
        scratch_shapes=[pltpu.VMEM((2 * HG_HEADS, HG_DIM, HG_DIM), F32)],
        compiler_params=_cparams(2, 40),
        name="hgrn",
    )(p_hg, p_hg, p_hg, p_hg, p_hg, p_hg, lb, pf, pb)


def _rope(x, cos, sin_signed, first_half):
    n = x.shape[-1]
    half = AT_DIM // 2
    partner = jnp.where(first_half, pltpu.roll(x, n - half, axis=1), pltpu.roll(x, half, axis=1))
    return x * cos + partner * sin_signed


def _head_norm(x, gain, group_mean):
    ms = _dot((x * x).astype(BF16), group_mean)
    return x * lax.rsqrt(ms + NORM_EPS) * gain


def _attn_kernel(q_ref, k_ref, v_ref, cq_ref, sq_ref, ck_ref, sk_ref, gq_ref, gk_ref, mq_ref, mk_ref,
                 o_ref, k_scr, v_scr, *, q_off, n_ctx_tiles, n_ctx_rows):
    i = pl.program_id(1)

    @pl.when(i == 0)
    def _():
        kr = k_ref[0].astype(F32)
        lane = lax.broadcasted_iota(jnp.int32, kr.shape, 1)
        kn = _head_norm(kr, gk_ref[...], mk_ref[...])
        k_scr[...] = _rope(kn, ck_ref[...], sk_ref[...], (lane % AT_DIM) < AT_DIM // 2).astype(BF16)
        v = v_ref[0]
        one = jnp.ones_like(v)
        v_scr[0] = jnp.where(lane < AT_DIM, v, one)
        v_scr[1] = jnp.where(lane < AT_DIM, one, v)

    qr = q_ref[0].astype(F32)
    lane = lax.broadcasted_iota(jnp.int32, qr.shape, 1)
    qn = _head_norm(qr, gq_ref[...], mq_ref[...])
    qn = _rope(qn, cq_ref[...], sq_ref[...], (lane % AT_DIM) < AT_DIM // 2)
    qn = (qn * (AT_DIM ** -0.5 * math.log2(math.e))).astype(BF16)
    lane_t = lax.broadcasted_iota(jnp.int32, (ROW_TILE, LANE), 1)
    kv0 = lane_t < AT_DIM

    def attend(n_keys):
        keys = k_scr[0:n_keys, :]
        heads = [(j, g) for j in range(AT_WIDTH // LANE) for g in range(AT_KV_HEADS)]

        def scores(j, g):
            qt = qn[:, j * LANE:(j + 1) * LANE]
            return _dot_nt(jnp.where(kv0 if g == 0 else ~kv0, qt, jnp.zeros_like(qt)), keys)

        pending = [scores(*heads[n]) for n in range(AT_LOOKAHEAD)]
        outs = {}
        for n, (j, g) in enumerate(heads):
            s = pending.pop(0)
            if n + AT_LOOKAHEAD < len(heads):
                pending.append(scores(*heads[n + AT_LOOKAHEAD]))
            p = jnp.exp2(s - jnp.max(s, axis=1, keepdims=True))
            outs[g] = _dot(p.astype(BF16), v_scr[g, 0:n_keys, :])
            if g == AT_KV_HEADS - 1:
                num = jnp.where(kv0, outs[0], outs[1])
                den = pltpu.roll(jnp.where(kv0, outs[1], outs[0]), AT_DIM, axis=1)
                o_ref[0, :, j * LANE:(j + 1) * LANE] = (num / den).astype(o_ref.dtype)

    n_all = k_scr.shape[0]
    if q_off < n_ctx_tiles:
        @pl.when(i + q_off < n_ctx_tiles)
        def _():
            attend(n_ctx_rows)

        @pl.when(i + q_off >= n_ctx_tiles)
        def _():
            attend(n_all)
    else:
        attend(n_all)


def _attention(p_rest, col_q, col_k, col_v, tabs, gq, gk, q_off, n_ctx_rows):
    b, s, _ = p_rest.shape
    nt = s // ROW_TILE
    cq, sq, ck, sk = tabs
    mq = jnp.asarray(np.kron(np.eye(AT_HEADS), np.full((AT_DIM, AT_DIM), 1.0 / AT_DIM)), BF16)
    mk = jnp.asarray(np.kron(np.eye(AT_KV_HEADS), np.full((AT_DIM, AT_DIM), 1.0 / AT_DIM)), BF16)
    kern = functools.partial(_attn_kernel, q_off=q_off, n_ctx_tiles=n_ctx_rows // ROW_TILE,
                             n_ctx_rows=n_ctx_rows)
    return pl.pallas_call(
        kern,
        grid=(b, nt - q_off),
        in_specs=[pl.BlockSpec((1, ROW_TILE, AT_WIDTH), lambda bi, i: (bi, i + q_off, col_q)),
                  pl.BlockSpec((1, s, AT_KV_WIDTH), lambda bi, i: (bi, 0, col_k)),
                  pl.BlockSpec((1, s, AT_KV_WIDTH), lambda bi, i: (bi, 0, col_v)),
                  pl.BlockSpec((ROW_TILE, AT_WIDTH), lambda bi, i: (i + q_off, 0)),
                  pl.BlockSpec((ROW_TILE, AT_WIDTH), lambda bi, i: (i + q_off, 0)),
                  pl.BlockSpec((s, AT_KV_WIDTH), lambda bi, i: (0, 0)),
                  pl.BlockSpec((s, AT_KV_WIDTH), lambda bi, i: (0, 0)),
                  pl.BlockSpec((1, AT_WIDTH), lambda bi, i: (0, 0)),
                  pl.BlockSpec((1, AT_KV_WIDTH), lambda bi, i: (0, 0)),
                  pl.BlockSpec((AT_WIDTH, AT_WIDTH), lambda bi, i: (0, 0)),
                  pl.BlockSpec((AT_KV_WIDTH, AT_KV_WIDTH), lambda bi, i: (0, 0))],
        out_specs=pl.BlockSpec((1, ROW_TILE, AT_WIDTH), lambda bi, i: (bi, i + q_off, 0)),
        out_shape=jax.ShapeDtypeStruct((b, s, AT_WIDTH), BF16),
        scratch_shapes=[pltpu.VMEM((s, AT_KV_WIDTH), BF16),
                        pltpu.VMEM((AT_KV_HEADS, s, AT_KV_WIDTH), BF16)],
        compiler_params=_cparams(2, 48),
        name="attention",
    )(p_rest, p_rest, p_rest, cq, sq, ck, sk, gq, gk, mq, mk)


def _rope_tables(n_ctx_rows, n_lat_rows):
    rows = n_lat_rows // GRID_W
    row = jnp.repeat(jnp.arange(rows), GRID_W).astype(F32)
    col = jnp.tile(jnp.arange(GRID_W), rows).astype(F32)
    n_freq = AT_DIM // 4
    inv = ROPE_THETA ** (-jnp.arange(n_freq, dtype=F32) / n_freq)
    ang = jnp.concatenate([row[:, None] * inv, col[:, None] * inv], axis=-1)
    cos = jnp.concatenate([jnp.cos(ang), jnp.cos(ang)], axis=-1)
    sin = jnp.concatenate([-jnp.sin(ang), jnp.sin(ang)], axis=-1)
    cos = jnp.concatenate([jnp.ones((n_ctx_rows, AT_DIM), F32), cos], axis=0)
    sin = jnp.concatenate([jnp.zeros((n_ctx_rows, AT_DIM), F32), sin], axis=0)
    return (jnp.tile(cos, (1, AT_HEADS)), jnp.tile(sin, (1, AT_HEADS)),
            jnp.tile(cos, (1, AT_KV_HEADS)), jnp.tile(sin, (1, AT_KV_HEADS)))


def _shift_matrices(n):
    i = np.arange(n)
    down = i[:, None] - 1 == i[None, :]
    up = i[:, None] + 1 == i[None, :]
    return jnp.asarray(np.stack([down, up]).astype(np.float32), BF16)


def _conv3(xb, shift_ref, prev_row, next_row, w, bias):
    n, c = xb.shape
    sub = 8
    r = lax.broadcasted_iota(jnp.int32, (sub, c), 0)
    x = xb.astype(F32)
    if shift_ref is None:
        xm = pltpu.roll(x, 1, axis=0)
        xp = pltpu.roll(x, n - 1, axis=0)
    else:
        xm = _dot(shift_ref[0], xb)
        xp = _dot(shift_ref[1], xb)
    xm = jnp.concatenate([jnp.where(r == 0, prev_row, xm[0:sub]), xm[sub:]], axis=0)
    xp = jnp.concatenate([xp[:n - sub], jnp.where(r == sub - 1, next_row, xp[n - sub:])], axis=0)
    return xm * w[0:1] + x * w[1:2] + xp * w[2:3] + bias


def _halo_specs(width, col, row_off, n_rows, rows=ROW_TILE, samples=1):
    per = rows // BF16_SUBLANES
    last = n_rows // BF16_SUBLANES - 1
    return [
        pl.BlockSpec((samples, rows, width), lambda bi, i: (bi, i + row_off, col)),
        pl.BlockSpec((samples, BF16_SUBLANES, width),
                     lambda bi, i: (bi, jnp.maximum((i + row_off) * per - 1, 0), col)),
        pl.BlockSpec((samples, BF16_SUBLANES, width),
                     lambda bi, i: (bi, jnp.minimum((i + row_off + 1) * per, last), col)),
    ]


def _halo_rows(prev_ref, next_ref, is_first, is_last, b):
    prev_row = prev_ref[b, BF16_SUBLANES - 1:BF16_SUBLANES, :].astype(F32)
    next_row = next_ref[b, 0:1, :].astype(F32)
    prev_row = jnp.where(is_first, 0.0, prev_row)
    next_row = jnp.where(is_last, 0.0, next_row)
    return prev_row, next_row


HY_PRE_SAMPLES = 2


def _hypre_kernel(z_ref, zp_ref, zn_ref, sh_ref, w_ref, b_ref, db_ref, u_ref, ud_ref, x0_ref):
    i = pl.program_id(1)
    for b in range(z_ref.shape[0]):
        prev_row, next_row = _halo_rows(zp_ref, zn_ref, i == 0, i == pl.num_programs(1) - 1, b)
        zc = _conv3(z_ref[b], sh_ref, prev_row, next_row, w_ref[...], b_ref[...])
        x0 = zc[:, :HY_WIDTH]
        x1 = zc[:, HY_WIDTH:2 * HY_WIDTH]
        v = zc[:, 2 * HY_WIDTH:]
        u = v * x1
        u_ref[b] = u.astype(BF16)
        ud_ref[b] = (u * db_ref[...]).astype(BF16)
        x0_ref[b] = x0.astype(BF16)


def _hyena_pre(p_rest, col, row_off, n_rows, conv_w, conv_b, d_bias):
    b, s, _ = p_rest.shape
    width = 3 * HY_WIDTH
    nb = HY_PRE_SAMPLES if b % HY_PRE_SAMPLES == 0 else 1
    out = jax.ShapeDtypeStruct((b, n_rows, HY_WIDTH), BF16)
    ospec = pl.BlockSpec((nb, ROW_TILE, HY_WIDTH), lambda bi, i: (bi, i, 0))
    return pl.pallas_call(
        _hypre_kernel,
        grid=(b // nb, n_rows // ROW_TILE),
        in_specs=_halo_specs(width, col, row_off, s, samples=nb) + [
            pl.BlockSpec((2, ROW_TILE, ROW_TILE), lambda bi, i: (0, 0, 0)),
            pl.BlockSpec((3, width), lambda bi, i: (0, 0)),
            pl.BlockSpec((1, width), lambda bi, i: (0, 0)),
            pl.BlockSpec((1, HY_WIDTH), lambda bi, i: (0, 0))],
        out_specs=[ospec, ospec, ospec],
        out_shape=[out, out, out],
        compiler_params=_cparams(2, 32),
        name="hyena_pre",
    )(p_rest, p_rest, p_rest, _shift_matrices(ROW_TILE), conv_w, conv_b.reshape(1, width),
      d_bias.reshape(1, HY_WIDTH))


def _hyfilt_kernel(z_ref, t_ref, dl_ref, w1_ref, b1_ref, wi_ref, bi_ref, fr_ref, wl_ref, o_ref):
    fr = fr_ref[...]
    h = jnp.sin(fr * (_dot3(z_ref[...], w1_ref[...]) + b1_ref[...]))
    for j in range(HY_INNER):
        h = jnp.sin(fr * (_dot3(h, wi_ref[j]) + bi_ref[j]))
    h = _dot3(h, wl_ref[...])
    decay = jnp.exp(-t_ref[...] * dl_ref[...])
    hf = h[:, :HY_WIDTH] * decay
    hb = h[:, HY_WIDTH:] * decay
    o_ref[...] = jnp.concatenate([hf + hb, hf - hb], axis=1)


def _pad2(a, rows, cols):
    return jnp.pad(a, ((0, rows - a.shape[0]), (0, cols - a.shape[1])))


def _hyena_filter_sums(n, w1, b1, wi, bi, freq, w_last):
    t = jnp.linspace(0.0, 1.0, n, dtype=F32)[:, None]
    w = 2.0 * math.pi * jnp.arange(n, dtype=F32)[:, None] / n
    f = jnp.linspace(1e-4, HY_BANDS - 1, HY_BANDS, dtype=F32)[None, :]
    z = jnp.concatenate([t, jnp.cos(f * w), -jnp.sin(f * w)], axis=-1)
    max_decay = math.log(HY_TARGET) / HY_FAST_DECAY
    min_decay = math.log(HY_TARGET) / HY_SLOW_DECAY
    deltas = jnp.abs(jnp.linspace(min_decay, max_decay, HY_WIDTH, dtype=F32))[None, :]
    zp = _pad2(z, n, LANE)
    w1p = _pad2(w1, LANE, LANE)
    b1p = _pad2(b1[None, :], 1, LANE)
    wip = jnp.stack([_pad2(wi[j], LANE, LANE) for j in range(HY_INNER)])
    bip = jnp.stack([_pad2(bi[j][None, :], 1, LANE) for j in range(HY_INNER)])
    frp = _pad2(freq[None, :], 1, LANE)
    wlp = _pad2(w_last, LANE, 2 * HY_WIDTH)
    tr = min(n, ROW_TILE)
    full = lambda shape: pl.BlockSpec(shape, lambda i: (0,) * len(shape))
    return pl.pallas_call(
        _hyfilt_kernel,
        grid=(n // tr,),
        in_specs=[pl.BlockSpec((tr, LANE), lambda i: (i, 0)),
                  pl.BlockSpec((tr, 1), lambda i: (i, 0)),
                  full((1, HY_WIDTH)), full((LANE, LANE)), full((1, LANE)),
                  full((HY_INNER, LANE, LANE)), full((HY_INNER, 1, LANE)), full((1, LANE)),
                  full((LANE, 2 * HY_WIDTH))],
        out_specs=pl.BlockSpec((tr, 2 * HY_WIDTH), lambda i: (i, 0)),
        out_shape=jax.ShapeDtypeStruct((n, 2 * HY_WIDTH), F32),
        compiler_params=_cparams(1, 32),
        name="hyena_filter",
    )(zp, t, deltas, w1p, b1p, wip, bip, frp, wlp)


def _dft_tables(n):
    f = jnp.arange(n, dtype=jnp.int32)[:, None]
    t = jnp.arange(n, dtype=jnp.int32)[None, :]
    ang = ((f * t) % (2 * n)).astype(F32) * (math.pi / n)
    nyq = jnp.where(t % 2 == 0, 1.0, -1.0).astype(F32)
    return jnp.stack([jnp.cos(ang), jnp.where(f == 0, nyq, jnp.sin(ang))])


def _hyfwd_kernel(u_ref, f_ref, co_ref, o_ref):
    u = u_ref[0]
    ure = _dot(f_ref[0], u)
    uim = _dot(f_ref[1], u)
    o_ref[0, 0] = (ure * co_ref[0] - uim * co_ref[1]).astype(BF16)
    o_ref[0, 1] = (ure * co_ref[2] + uim * co_ref[3]).astype(BF16)


def _hyinv_kernel(y_ref, ft_ref, ud_ref, x0_ref, o_ref):
    y = _dot(ft_ref[0], y_ref[0, 0]) + _dot(ft_ref[1], y_ref[0, 1])
    o_ref[0] = ((y + ud_ref[0].astype(F32)) * x0_ref[0].astype(F32)).astype(BF16)


def _hyena_conv(u, ud, x0, tables, coef):
    b, n, c = u.shape
    tf = min(n, DFT_ROWS)
    f_bf = tables.astype(BF16)
    ft_bf = jnp.swapaxes(tables, 1, 2).astype(BF16)
    spec = pl.pallas_call(
        _hyfwd_kernel,
        grid=(n // tf, b),
        in_specs=[pl.BlockSpec((1, n, c), lambda j, bi: (bi, 0, 0)),
                  pl.BlockSpec((2, tf, n), lambda j, bi: (0, j, 0)),
                  pl.BlockSpec((4, tf, c), lambda j, bi: (0, j, 0))],
        out_specs=pl.BlockSpec((1, 2, tf, c), lambda j, bi: (bi, 0, j, 0)),
        out_shape=jax.ShapeDtypeStruct((b, 2, n, c), BF16),
        compiler_params=_cparams(2, 40),
        name="hyena_dft",
    )(u, f_bf, coef)
    return pl.pallas_call(
        _hyinv_kernel,
        grid=(b, n // tf),
        in_specs=[pl.BlockSpec((1, 2, n, c), lambda bi, j: (bi, 0, 0, 0)),
                  pl.BlockSpec((2, tf, n), lambda bi, j: (0, j, 0)),
                  pl.BlockSpec((1, tf, c), lambda bi, j: (bi, j, 0)),
                  pl.BlockSpec((1, tf, c), lambda bi, j: (bi, j, 0))],
        out_specs=pl.BlockSpec((1, tf, c), lambda bi, j: (bi, j, 0)),
        out_shape=jax.ShapeDtypeStruct((b, n, c), BF16),
        compiler_params=_cparams(2, 40),
        name="hyena_idft",
    )(spec, ft_bf, ud, x0)


def _hyena_coef(tables, hsum_hdiff):
    n = tables.shape[1]
    c = HY_WIDTH
    r = _matmul_f32(tables.reshape(2 * n, n), hsum_hdiff, min(n, ROW_TILE), "hyena_kernel_dft")
    k_re = r[:n, :c]
    k_im = r[n:, c:]
    k_nyq = r[n:n + 1, :c]
    first = (jnp.arange(n) == 0)[:, None]
    scale = jnp.where(first, 1.0 / (2 * n), 2.0 / (2 * n)).astype(F32)
    zero = jnp.zeros_like(k_im)
    return jnp.stack([k_re * scale,
                      jnp.where(first, zero, k_im * scale),
                      jnp.where(first, zero, k_im * scale),
                      jnp.where(first, k_nyq, k_re) * scale])


def _merge_kernel(*refs, n_ctx_tiles, row_off, has_ctx):
    if has_ctx:
        (of_ref, ob_ref, zg_ref, att_ref, cx_ref, cc_ref, gate_ref, x_ref, m_ref, ghg_ref, gpost_ref,
         gffn_ref, woa_ref, wob_ref, woc_ref, wout_ref, o_ref, h_ref) = refs
    else:
        (of_ref, ob_ref, zg_ref, att_ref, cx_ref, gate_ref, x_ref, m_ref, ghg_ref, gpost_ref,
         gffn_ref, woa_ref, wob_ref, woc_ref, wout_ref, o_ref, h_ref) = refs
    nb, rows, d = x_ref.shape
    flat = lambda ref: ref[...].reshape(nb * rows, ref.shape[-1])
    o = flat(of_ref) + flat(ob_ref)
    ghg = ghg_ref[...]
    a = jnp.concatenate([_rms(o[:, h * HG_DIM:(h + 1) * HG_DIM], ghg) for h in range(HG_HEADS)], axis=1)
    zg = flat(zg_ref)
    a = a * (zg * _sigmoid(zg))
    c = flat(cx_ref)
    if has_ctx:
        c = jnp.where(pl.program_id(1) + row_off < n_ctx_tiles, flat(cc_ref), c)
    ya = _dot(a.astype(BF16), woa_ref[...])
    yb = _dot(flat(att_ref), wob_ref[...])
    yc = _dot(c, woc_ref[...])
    gates = flat(gate_ref)
    m = (_sigmoid(gates[:, 0:d].astype(F32)) * ya
         + _sigmoid(gates[:, d:2 * d].astype(F32)) * yb
         + _sigmoid(gates[:, 2 * d:3 * d].astype(F32)) * yc)
    y = _rms(_dot(m.astype(BF16), wout_ref[...]), gpost_ref[...])
    for b in range(nb):
        x_new = x_ref[b] + m_ref[b, 2:3, :] * y[b * rows:(b + 1) * rows]
        o_ref[b] = x_new
        h_ref[b] = (_rms(x_new, gffn_ref[...]) * (1.0 + m_ref[b, 4:5, :]) + m_ref[b, 3:4, :]).astype(BF16)


def _merge(o_f, o_b, p_hg, att, c_x, c_c, p_rest, xs, mods, g_hg, g_post, g_ffn, w_oa, w_ob, w_oc, w_out,
           row_off, n_ctx_rows):
    b, s, d = xs.shape
    nct = n_ctx_rows // ROW_TILE
    n_tiles = s // ROW_TILE - row_off
    has_ctx = c_c is not None
    nb = MERGE_SAMPLES if b % MERGE_SAMPLES == 0 else 1
    ctx_blk = b // nb

    def stream(width, col=0):
        return pl.BlockSpec((nb, ROW_TILE, width), lambda bi, i: (bi, i + row_off, col))

    def full(shape):
        return pl.BlockSpec(shape, lambda bi, i: (0,) * len(shape))

    in_specs = [stream(HG_WIDTH), stream(HG_WIDTH), stream(HG_WIDTH, 4), stream(AT_WIDTH),
                pl.BlockSpec((nb, ROW_TILE, HY_WIDTH),
                             lambda bi, i: (bi, jnp.maximum(i + row_off - nct, 0), 0))]
    args = [o_f, o_b, p_hg, att, c_x]
    if has_ctx:
        in_specs.append(pl.BlockSpec((nb, ROW_TILE, HY_WIDTH),
                                     lambda bi, i: (bi, jnp.minimum(i + row_off, nct - 1), 0)))
        args.append(c_c)
    in_specs += [stream(3 * d), stream(d),
                 pl.BlockSpec((nb, 6, d), lambda bi, i: (jnp.where(i + row_off < nct, ctx_blk, bi), 0, 0)),
                 full((1, HG_DIM)), full((1, d)), full((1, d)),
                 full((HG_WIDTH, d)), full((AT_WIDTH, d)), full((HY_WIDTH, d)), full((d, d))]
    args += [p_rest, xs, mods, g_hg.reshape(1, HG_DIM), g_post.reshape(1, d), g_ffn.reshape(1, d),
             w_oa, w_ob, w_oc, w_out]
    ospec = pl.BlockSpec((nb, ROW_TILE, d), lambda bi, i: (bi, i, 0))
    return pl.pallas_call(
        functools.partial(_merge_kernel, n_ctx_tiles=nct, row_off=row_off, has_ctx=has_ctx),
        grid=(b // nb, n_tiles),
        in_specs=in_specs,
        out_specs=[ospec, ospec],
        out_shape=[jax.ShapeDtypeStruct((b, n_tiles * ROW_TILE, d), F32),
                   jax.ShapeDtypeStruct((b, n_tiles * ROW_TILE, d), BF16)],
        compiler_params=_cparams(2, 48),
        name="merge",
    )(*args)


FFN_COLS = 256
FFN_DOWN_GROUPS = 2


def _ffn_kernel(*refs, first_tiles, last_tiles, has_next):
    if has_next:
        (h_ref, hp_ref, hn_ref, wu_ref, w_ref, b_ref, x_ref, m_ref, g_ref, wd_ref, mn_ref, gn_ref,
         o_ref, hx_ref, act_ref) = refs
    else:
        h_ref, hp_ref, hn_ref, wu_ref, w_ref, b_ref, x_ref, m_ref, g_ref, wd_ref, o_ref, act_ref = refs
    i = pl.program_id(1)
    is_first = functools.reduce(jnp.logical_or, [i == t for t in first_tiles])
    is_last = functools.reduce(jnp.logical_or, [i == t for t in last_tiles])
    d_ff = wd_ref.shape[0]
    rows = h_ref.shape[1]
    halo = BF16_SUBLANES
    ext = rows + 2 * halo
    hp = jnp.where(is_first, jnp.zeros_like(hp_ref[0]), hp_ref[0])
    hn = jnp.where(is_last, jnp.zeros_like(hn_ref[0]), hn_ref[0])
    h_ext = jnp.concatenate([hp, h_ref[0], hn], axis=0)

    def up(j):
        return [_dot(h_ext, wu_ref[:, base + j * FFN_COLS:base + (j + 1) * FFN_COLS]) for base in (0, d_ff)]

    def conv(u, cols):
        w = w_ref[:, cols]
        full = pltpu.roll(u, 1, axis=0) * w[0:1] + u * w[1:2] + pltpu.roll(u, ext - 1, axis=0) * w[2:3]
        return full[halo:halo + rows] + b_ref[:, cols]

    n_chunks = d_ff // FFN_COLS
    per_group = -(-n_chunks // FFN_DOWN_GROUPS)
    acc = None
    u_next = up(0)
    for j in range(n_chunks):
        u = u_next
        if j + 1 < n_chunks:
            u_next = up(j + 1)
        a = conv(u[0], slice(j * FFN_COLS, (j + 1) * FFN_COLS))
        g = conv(u[1], slice(d_ff + j * FFN_COLS, d_ff + (j + 1) * FFN_COLS))
        act_ref[:, j * FFN_COLS:(j + 1) * FFN_COLS] = (a * _sigmoid(a) * g).astype(BF16)
        if (j + 1) % per_group == 0 or j + 1 == n_chunks:
            lo = (j // per_group) * per_group * FFN_COLS
            part = _dot(act_ref[:, lo:(j + 1) * FFN_COLS], wd_ref[lo:(j + 1) * FFN_COLS, :])
            acc = part if acc is None else acc + part
    x_new = x_ref[0] + m_ref[0, 5:6, :] * _rms(acc, g_ref[...])
    o_ref[0] = x_new
    if has_next:
        hx_ref[0] = (_rms(x_new, gn_ref[...]) * (1.0 + mn_ref[0, 1:2, :]) + mn_ref[0, 0:1, :]).astype(BF16)


def _ffn(h, xs, mods, w_up, conv_w, conv_b, g_post, w_down, n_ctx_rows, mods_next=None, g_next=None):
    b, s, d = xs.shape
    d_ff = w_down.shape[0]
    rows = FFN_ROWS if (n_ctx_rows % FFN_ROWS == 0 and s % FFN_ROWS == 0) else ROW_TILE
    nt = s // rows
    nct = n_ctx_rows // rows
    first_tiles = tuple(sorted({0, nct}))
    last_tiles = tuple(sorted({nct - 1, nt - 1} - {-1}))
    ctx_row = b
    has_next = mods_next is not None
    full = lambda shape: pl.BlockSpec(shape, lambda bi, i: (0,) * len(shape))
    mspec = pl.BlockSpec((1, 6, d), lambda bi, i: (jnp.where(i < nct, ctx_row, bi), 0, 0))
    ospec = pl.BlockSpec((1, rows, d), lambda bi, i: (bi, i, 0))
    resident = lambda shape: pl.BlockSpec(shape, lambda bi, i: (0,) * len(shape), pipeline_mode=pl.Buffered(1))
    in_specs = _halo_specs(d, 0, 0, s, rows) + [
        resident((d, 2 * d_ff)), full((3, 2 * d_ff)), full((1, 2 * d_ff)),
        ospec, mspec, full((1, d)), resident((d_ff, d))]
    args = [h, h, h, w_up, conv_w, conv_b.reshape(1, 2 * d_ff), xs, mods, g_post.reshape(1, d), w_down]
    out_specs = [ospec]
    out_shape = [jax.ShapeDtypeStruct((b, s, d), F32)]
    if has_next:
        in_specs += [mspec, full((1, d))]
        args += [mods_next, g_next.reshape(1, d)]
        out_specs.append(ospec)
        out_shape.append(jax.ShapeDtypeStruct((b, s, d), BF16))
    return pl.pallas_call(
        functools.partial(_ffn_kernel, first_tiles=first_tiles, last_tiles=last_tiles, has_next=has_next),
        grid=(b, nt),
        in_specs=in_specs,
        out_specs=out_specs,
        out_shape=out_shape,
        scratch_shapes=[pltpu.VMEM((rows, d_ff), BF16)],
        compiler_params=_cparams(2, 56),
        name="ffn",
    )(*args)


def _deinterleave():
    return np.concatenate([np.arange(0, AT_DIM, 2), np.arange(1, AT_DIM, 2)])


def _q_head_order():
    return [h for j in range(AT_GROUP) for h in (j, AT_GROUP + j)]


def _largest_tile(n, cap):
    best = LANE
    for t in range(LANE, cap + 1, LANE):
        if n % t == 0:
            best = t
    return best


def kernel(x, c, ctx, c_ctx, w_ada, b_ada, g_pre_mix, g_post_mix, g_pre_ffn, g_post_ffn, w_in, hg_lower_bounds, hg_norm, q_norm, k_norm, hy_conv_w, hy_conv_b, hy_w1, hy_b1, hy_wi, hy_bi, hy_freq, hy_w_last, hy_bias, w_oa, w_ob, w_oc, w_out, w_up, ffn_conv_w, ffn_conv_b, w_down):
    bsz, n_lat, d = x.shape
    n_ctx = ctx.shape[1]
    depth = w_ada.shape[0]
    d_ff = w_down.shape[1]
    assert AT_KV_HEADS == 2 and AT_GROUP * LANE == AT_WIDTH and AT_KV_WIDTH == LANE
    assert n_ctx % ROW_TILE == 0 and n_lat % ROW_TILE == 0 and n_lat % GRID_W == 0
    assert (bsz * (n_ctx + n_lat)) % MM_ROWS == 0 and (bsz * n_lat) % MM_ROWS == 0

    lbp = jax.nn.softmax(hg_lower_bounds.astype(F32), axis=0)
    lower = jnp.cumsum(lbp, axis=0) - lbp[0]

    rp = -(-(bsz + MERGE_SAMPLES) // 8) * 8
    src = jnp.concatenate([c, jnp.tile(c_ctx[None, :], (MERGE_SAMPLES, 1)),
                           jnp.zeros((rp - bsz - MERGE_SAMPLES, d), F32)], axis=0)
    mods_all = _ada(src, w_ada, b_ada).reshape(depth, rp, 6, d)

    o_q = 5 * HG_WIDTH
    o_k = o_q + AT_WIDTH
    o_v = o_k + AT_KV_WIDTH
    o_hy = o_v + AT_KV_WIDTH
    o_gate = o_hy + 3 * HY_WIDTH
    deint = _deinterleave()
    q_cols = np.concatenate([o_q + h * AT_DIM + deint for h in _q_head_order()])
    k_cols = np.concatenate([o_k + g * AT_DIM + deint for g in range(AT_KV_HEADS)])
    rest_cols = np.concatenate([np.arange(o_gate, o_gate + 3 * d), np.arange(o_hy, o_hy + 3 * HY_WIDTH),
                                q_cols, k_cols, np.arange(o_v, o_v + AT_KV_WIDTH)])
    col_hy = (3 * d) // (3 * HY_WIDTH)
    col_q = (3 * d + 3 * HY_WIDTH) // AT_WIDTH
    col_k = (3 * d + 3 * HY_WIDTH + AT_WIDTH) // AT_KV_WIDTH
    col_v = col_k + 1
    assert (3 * d) % (3 * HY_WIDTH) == 0 and (3 * d + 3 * HY_WIDTH) % AT_WIDTH == 0
    ob_rows = np.concatenate([np.arange(h * AT_DIM, (h + 1) * AT_DIM) for h in _q_head_order()])

    rope_tabs = _rope_tables(n_ctx, n_lat)
    dft_lat = _dft_tables(n_lat)
    dft_ctx = _dft_tables(n_ctx)
    nct = n_ctx // ROW_TILE

    s_all = n_ctx + n_lat
    xs, h = _join_modulate(ctx, x, mods_all[0], g_pre_mix[0], 0, 1)
    for l in range(depth):
        need_ctx = l < depth - 1
        mods = mods_all[l]
        w_hg = w_in[l][:, :5 * HG_WIDTH].astype(BF16)
        w_rest = w_in[l][:, rest_cols].astype(BF16)

        h = h.reshape(bsz * s_all, d)
        p_hg = _matmul(h, w_hg, F32, _largest_tile(5 * HG_WIDTH, 1280), "proj_hgrn").reshape(bsz, s_all, -1)
        p_rest = _matmul(h, w_rest, BF16, _largest_tile(w_rest.shape[1], 1792), "proj_rest").reshape(bsz, s_all, -1)

        o_f, o_b = _hgrn(p_hg, lower[l], n_ctx)

        gq = jnp.tile(q_norm[l][deint], AT_HEADS)[None, :]
        gk = jnp.tile(k_norm[l][deint], AT_KV_HEADS)[None, :]
        row_off = 0 if need_ctx else nct
        att = _attention(p_rest, col_q, col_k, col_v, rope_tabs, gq, gk, row_off, n_ctx)

        filt_args = (hy_w1[l], hy_b1[l], hy_wi[l], hy_bi[l], hy_freq[l], hy_w_last[l])
        coef = _hyena_coef(dft_lat, _hyena_filter_sums(n_lat, *filt_args))
        c_x = _hyena_conv(*_hyena_pre(p_rest, col_hy, nct, n_lat, hy_conv_w[l], hy_conv_b[l], hy_bias[l]),
                          dft_lat, coef)
        c_c = None
        if need_ctx:
            coef_c = _hyena_coef(dft_ctx, _hyena_filter_sums(n_ctx, *filt_args))
            c_c = _hyena_conv(*_hyena_pre(p_rest, col_hy, 0, n_ctx, hy_conv_w[l], hy_conv_b[l], hy_bias[l]),
                              dft_ctx, coef_c)

        xs, h2 = _merge(o_f, o_b, p_hg, att, c_x, c_c, p_rest, xs, mods, hg_norm[l], g_post_mix[l],
                        g_pre_ffn[l], w_oa[l].astype(BF16), w_ob[l][ob_rows].astype(BF16),
                        w_oc[l].astype(BF16), w_out[l].astype(BF16), row_off, n_ctx)
        n_ctx_now = n_ctx if need_ctx else 0
        ffn_args = (h2, xs, mods, w_up[l].astype(BF16), ffn_conv_w[l], ffn_conv_b[l], g_post_ffn[l],
                    w_down[l].astype(BF16), n_ctx_now)
        if need_ctx:
            xs, h = _ffn(*ffn_args, mods_all[l + 1], g_pre_mix[l + 1])
        else:
            xs, = _ffn(*ffn_args)
    return xs
```

```python
import functools
import math

import jax
import jax.numpy as jnp
import numpy as np
from jax import lax
from jax.experimental import pallas as pl
from jax.experimental.pallas import tpu as pltpu

F32 = jnp.float32
BF16 = jnp.bfloat16

NORM_EPS = 1e-6
GRID_W = 64
HG_HEADS = 4
HG_DIM = 128
HG_WIDTH = HG_HEADS * HG_DIM
HG_EXP_CLIP = 30.0
AT_HEADS = 8
AT_KV_HEADS = 2
AT_DIM = 64
AT_GROUP = AT_HEADS // AT_KV_HEADS
AT_WIDTH = AT_HEADS * AT_DIM
AT_KV_WIDTH = AT_KV_HEADS * AT_DIM
ROPE_THETA = 10000.0
HY_WIDTH = 512
HY_EMB_DIM = 33
HY_BANDS = (HY_EMB_DIM - 1) // 2
HY_FILTER_WIDTH = 64
HY_INNER = 2
HY_FAST_DECAY = 0.3
HY_SLOW_DECAY = 1.5
HY_TARGET = 1e-2

LANE = 128
BF16_SUBLANES = 16
ROW_TILE = 256
MERGE_SAMPLES = 2
HG_CHUNK = 128
HG_BLOCK = 256
HG_LEVELS = tuple(HG_CHUNK >> (j + 1) for j in range(int(math.log2(HG_CHUNK))))
MM_ROWS = 1024
DFT_ROWS = 512
AT_LOOKAHEAD = 2
FFN_ROWS = 512
VMEM_CAP = 56 * 1024 * 1024


def _cparams(n_axes, vmem_mb):
    return pltpu.CompilerParams(
        dimension_semantics=("arbitrary",) * n_axes,
        vmem_limit_bytes=min(int(vmem_mb) * 1024 * 1024, VMEM_CAP))


def _dot(a, b):
    return jnp.dot(a, b, preferred_element_type=F32)


def _dot_nt(a, b):
    return lax.dot_general(a, b, (((1,), (1,)), ((), ())), preferred_element_type=F32)


def _split_bf16(a):
    hi = a.astype(BF16)
    lo = (a - hi.astype(F32)).astype(BF16)
    return hi, lo


def _dot3(a, b):
    ah, al = _split_bf16(a)
    bh, bl = _split_bf16(b)
    return _dot(ah, bh) + (_dot(ah, bl) + _dot(al, bh))


def _rms(x, g):
    return x * lax.rsqrt(jnp.mean(x * x, axis=-1, keepdims=True) + NORM_EPS) * g


def _sigmoid(x):
    return 0.5 * jnp.tanh(0.5 * x) + 0.5


def _ada_kernel(src_ref, w_ref, b_ref, o_ref):
    s = src_ref[...]
    s = s * _sigmoid(s)
    o_ref[0] = _dot3(s, w_ref[0]) + b_ref[0]


def _ada(src, w_ada, b_ada):
    depth, d, d6 = w_ada.shape
    rp = src.shape[0]
    tn = d
    return pl.pallas_call(
        _ada_kernel,
        grid=(depth, d6 // tn),
        in_specs=[pl.BlockSpec((rp, d), lambda l, j: (0, 0)),
                  pl.BlockSpec((1, d, tn), lambda l, j: (l, 0, j)),
                  pl.BlockSpec((1, 1, tn), lambda l, j: (l, 0, j))],
        out_specs=pl.BlockSpec((1, rp, tn), lambda l, j: (l, 0, j)),
        out_shape=jax.ShapeDtypeStruct((depth, rp, d6), F32),
        compiler_params=_cparams(2, 32),
        name="ada",
    )(src, w_ada, b_ada.reshape(depth, 1, d6))


def _mod_kernel(c_ref, x_ref, m_ref, g_ref, xs_ref, o_ref, *, k_shift, k_scale, n_ctx_tiles):
    for b in range(x_ref.shape[0]):
        x = jnp.where(pl.program_id(1) < n_ctx_tiles, c_ref[b], x_ref[b])
        xs_ref[b] = x
        shift = m_ref[b, k_shift:k_shift + 1, :]
        scale = m_ref[b, k_scale:k_scale + 1, :]
        o_ref[b] = (_rms(x, g_ref[...]) * (1.0 + scale) + shift).astype(o_ref.dtype)


def _join_modulate(ctx, x, mods, g, k_shift, k_scale):
    b, n_lat, d = x.shape
    n_ctx_tiles = ctx.shape[1] // ROW_TILE
    nt = n_ctx_tiles + n_lat // ROW_TILE
    nb = MERGE_SAMPLES if b % MERGE_SAMPLES == 0 else 1
    ctx_blk = b // nb
    ospec = pl.BlockSpec((nb, ROW_TILE, d), lambda bi, i: (bi, i, 0))
    return pl.pallas_call(
        functools.partial(_mod_kernel, k_shift=k_shift, k_scale=k_scale, n_ctx_tiles=n_ctx_tiles),
        grid=(b // nb, nt),
        in_specs=[pl.BlockSpec((nb, ROW_TILE, d), lambda bi, i: (bi, jnp.minimum(i, n_ctx_tiles - 1), 0)),
                  pl.BlockSpec((nb, ROW_TILE, d), lambda bi, i: (bi, jnp.maximum(i - n_ctx_tiles, 0), 0)),
                  pl.BlockSpec((nb, 6, d), lambda bi, i: (jnp.where(i < n_ctx_tiles, ctx_blk, bi), 0, 0)),
                  pl.BlockSpec((1, d), lambda bi, i: (0, 0))],
        out_specs=[ospec, ospec],
        out_shape=[jax.ShapeDtypeStruct((b, nt * ROW_TILE, d), F32),
                   jax.ShapeDtypeStruct((b, nt * ROW_TILE, d), BF16)],
        compiler_params=_cparams(2, 16),
        name="modulate",
    )(ctx, x, mods, g.reshape(1, d))


def _mm_kernel(a_ref, b_ref, o_ref):
    o_ref[...] = _dot(a_ref[...], b_ref[...]).astype(o_ref.dtype)


def _matmul(a, w, out_dtype, tn, name):
    m, k = a.shape
    n = w.shape[1]
    tm = MM_ROWS
    assert m % tm == 0 and n % tn == 0
    return pl.pallas_call(
        _mm_kernel,
        grid=(m // tm, n // tn),
        in_specs=[pl.BlockSpec((tm, k), lambda i, j: (i, 0)),
                  pl.BlockSpec((k, tn), lambda i, j: (0, j))],
        out_specs=pl.BlockSpec((tm, tn), lambda i, j: (i, j)),
        out_shape=jax.ShapeDtypeStruct((m, n), out_dtype),
        compiler_params=_cparams(2, 48),
        name=name,
    )(a, w)


def _mm3_kernel(a_ref, b_ref, o_ref):
    o_ref[...] = _dot3(a_ref[...], b_ref[...])


def _matmul_f32(a, w, tm, name):
    m, k = a.shape
    n = w.shape[1]
    return pl.pallas_call(
        _mm3_kernel,
        grid=(m // tm,),
        in_specs=[pl.BlockSpec((tm, k), lambda i: (i, 0)),
                  pl.BlockSpec((k, n), lambda i: (0, 0))],
        out_specs=pl.BlockSpec((tm, n), lambda i: (i, 0)),
        out_shape=jax.ShapeDtypeStruct((m, n), F32),
        compiler_params=_cparams(1, 48),
        name=name,
    )(a, w)


def _hg_scan_matrix(reverse):
    t_n = HG_CHUNK
    t = np.arange(t_n)[:, None]
    u = np.arange(t_n)[None, :]
    rows = [(u >= t) if reverse else (u <= t)]
    for w in HG_LEVELS:
        base = (t // (2 * w)) * (2 * w)
        mid = base + w
        upper = (t - base) >= w
        if reverse:
            m = np.where(upper, (u >= mid) & (u < t), (u >= t) & (u < mid))
        else:
            m = np.where(upper, (u >= mid) & (u <= t), (u > t) & (u < mid))
        rows.append(m)
    m = np.concatenate(rows, axis=0).astype(np.float32)
    return np.concatenate([m, m], axis=1)


def _hgrn_kernel(qf_ref, zf_ref, vf_ref, qb_ref, zb_ref, vb_ref, lb_ref, pf_ref, pb_ref,
                 of_ref, ob_ref, s_ref):
    @pl.when(pl.program_id(1) == 0)
    def _():
        s_ref[...] = jnp.zeros_like(s_ref)

    t_n = HG_CHUNK
    ti = lax.broadcasted_iota(jnp.int32, (t_n, t_n), 0)
    si = lax.broadcasted_iota(jnp.int32, (t_n, t_n), 1)
    tx = ti ^ si
    dirs = ((qf_ref, zf_ref, vf_ref, pf_ref, of_ref), (qb_ref, zb_ref, vb_ref, pb_ref, ob_ref))
    masks = []
    for reverse in (False, True):
        later = (ti < si) if reverse else (ti > si)
        masks.append([later & (tx >= w) & (tx < 2 * w) for w in HG_LEVELS])
    n_sub = qf_ref.shape[1] // t_n
    heads = range(HG_HEADS)
    units = [(d, c, h) for d in range(2) for c in range(n_sub) for h in heads]
    cols = lambda h: slice(h * HG_DIM, (h + 1) * HG_DIM)
    rows = lambda c: slice(c * t_n, (c + 1) * t_n)

    kk, kb, qb, cat, x, ex = {}, {}, {}, {}, {}, {}
    a = {u: jnp.zeros((t_n, t_n), F32) for u in units}

    def gates(d, c, h):
        z = dirs[d][1][0, rows(c), cols(h)]
        lb = lb_ref[d:d + 1, cols(h)]
        e = jnp.exp(-jnp.abs(z))
        log_num = jnp.log(jnp.where(z >= 0.0, 1.0 + lb * e, e + lb))
        log_clip = z + jnp.log(1.0 + lb * math.exp(HG_EXP_CLIP))
        lf = jnp.where(z < -HG_EXP_CLIP, log_clip, log_num) - jnp.log(1.0 + e)
        half = 0.5 * (1.0 - lb)
        kk[d, c, h] = half - half * jnp.tanh(0.5 * z)
        kb[d, c, h] = kk[d, c, h].astype(BF16)
        qb[d, c, h] = dirs[d][0][0, rows(c), cols(h)].astype(BF16)
        hi, lo = _split_bf16(lf)
        cat[d, c, h] = jnp.concatenate([hi, lo], axis=0)

    def exponents(d):
        rhs = jnp.concatenate([cat[d, c, h] for c in range(n_sub) for h in heads], axis=1)
        xd = _dot(dirs[d][3][...], rhs)
        ed = jnp.exp(xd[t_n:]).astype(BF16)
        for c in range(n_sub):
            for h in heads:
                x[d, c, h] = xd[0:t_n, cols(c * HG_HEADS + h)]
                ex[d, c, h] = ed[:, cols(c * HG_HEADS + h)]

    def level(d, c, j):
        for h in heads:
            u = (d, c, h)
            ew = ex[u][j * t_n:(j + 1) * t_n]
            pw = _dot_nt(qb[u] * ew, kb[u] * ew)
            a[u] = jnp.where(masks[d][j], pw, a[u])

    def finish(d, c, h):
        u = (d, c, h)
        q = dirs[d][0][0, rows(c), cols(h)]
        v = dirs[d][2][0, rows(c), cols(h)]
        g = x[u]
        g_last = g[0:1] if d == 1 else g[t_n - 1:t_n]
        st = s_ref[d * HG_HEADS + h]
        dqk = jnp.sum(q * kk[u], axis=1, keepdims=True)
        o = (_dot(a[u].astype(BF16), v.astype(BF16)) + dqk * v
             + _dot_nt((q * jnp.exp(g)).astype(BF16), st.astype(BF16)))
        kd = (kk[u] * jnp.exp(g_last - g)).astype(BF16)
        dirs[d][4][0, rows(c), cols(h)] = o
        s_ref[d * HG_HEADS + h] = st * jnp.exp(g_last) + _dot(v.T.astype(BF16), kd)

    for d in range(2):
        for c in range(n_sub):
            for h in heads:
                gates(d, c, h)
        exponents(d)
        for j in range(len(HG_LEVELS)):
            for c in range(n_sub):
                level(d, c, j)
        for c in (range(n_sub) if d == 0 else reversed(range(n_sub))):
            for h in heads:
                finish(d, c, h)


def _hgrn(p_hg, lb, n_ctx_rows):
    b, s, _ = p_hg.shape
    assert s % HG_BLOCK == 0 and n_ctx_rows % HG_BLOCK == 0
    nb = s // HG_BLOCK
    nc = n_ctx_rows // HG_BLOCK

    def bidx(n):
        return jnp.where(n < nc, nc - 1 - n, nb - 1 - (n - nc))

    blk = (1, HG_BLOCK, HG_WIDTH)
    pf = jnp.asarray(_hg_scan_matrix(False)).astype(BF16)
    pb = jnp.asarray(_hg_scan_matrix(True)).astype(BF16)
    pshape = pf.shape
    return pl.pallas_call(
        _hgrn_kernel,
        grid=(b, nb),
        in_specs=[pl.BlockSpec(blk, lambda bi, n: (bi, n, 0)),
                  pl.BlockSpec(blk, lambda bi, n: (bi, n, 1)),
                  pl.BlockSpec(blk, lambda bi, n: (bi, n, 3)),
                  pl.BlockSpec(blk, lambda bi, n: (bi, bidx(n), 0)),
                  pl.BlockSpec(blk, lambda bi, n: (bi, bidx(n), 2)),
                  pl.BlockSpec(blk, lambda bi, n: (bi, bidx(n), 3)),
                  pl.BlockSpec((2, HG_WIDTH), lambda bi, n: (0, 0)),
                  pl.BlockSpec(pshape, lambda bi, n: (0, 0)),
                  pl.BlockSpec(pshape, lambda bi, n: (0, 0))],
        out_specs=[pl.BlockSpec(blk, lambda bi, n: (bi, n, 0)),
                   pl.BlockSpec(blk, lambda bi, n: (bi, bidx(n), 0))],
        out_shape=[jax.ShapeDtypeStruct((b, s, HG_WIDTH), F32)] * 2,
        scratch_shapes=[pltpu.VMEM((2 * HG_HEADS, HG_DIM, HG_DIM), F32)],
        compiler_params=_cparams(2, 40),
        name="hgrn",
    )(p_hg, p_hg, p_hg, p_hg, p_hg, p_hg, lb, pf, pb)


def _rope(x, cos, sin_signed, first_half):
    n = x.shape[-1]
    half = AT_DIM // 2
    partner = jnp.where(first_half, pltpu.roll(x, n - half, axis=1), pltpu.roll(x, half, axis=1))
    return x * cos + partner * sin_signed


def _head_norm(x, gain, group_mean):
    ms = _dot((x * x).astype(BF16), group_mean)
    return x * lax.rsqrt(ms + NORM_EPS) * gain


def _attn_kernel(q_ref, k_ref, v_ref, cq_ref, sq_ref, ck_ref, sk_ref, gq_ref, gk_ref, mq_ref, mk_ref,
                 o_ref, k_scr, v_scr, *, q_off, n_ctx_tiles, n_ctx_rows):
    i = pl.program_id(1)

    @pl.when(i == 0)
    def _():
        kr = k_ref[0].astype(F32)
        lane = lax.broadcasted_iota(jnp.int32, kr.shape, 1)
        kn = _head_norm(kr, gk_ref[...], mk_ref[...])
        k_scr[...] = _rope(kn, ck_ref[...], sk_ref[...], (lane % AT_DIM) < AT_DIM // 2).astype(BF16)
        v = v_ref[0]
        one = jnp.ones_like(v)
        v_scr[0] = jnp.where(lane < AT_DIM, v, one)
        v_scr[1] = jnp.where(lane < AT_DIM, one, v)

    qr = q_ref[0].astype(F32)
    lane = lax.broadcasted_iota(jnp.int32, qr.shape, 1)
    qn = _head_norm(qr, gq_ref[...], mq_ref[...])
    qn = _rope(qn, cq_ref[...], sq_ref[...], (lane % AT_DIM) < AT_DIM // 2)
    qn = (qn * (AT_DIM ** -0.5 * math.log2(math.e))).astype(BF16)
    lane_t = lax.broadcasted_iota(jnp.int32, (ROW_TILE, LANE), 1)
    kv0 = lane_t < AT_DIM

    def attend(n_keys):
        keys = k_scr[0:n_keys, :]
        heads = [(j, g) for j in range(AT_WIDTH // LANE) for g in range(AT_KV_HEADS)]

        def scores(j, g):
            qt = qn[:, j * LANE:(j + 1) * LANE]
            return _dot_nt(jnp.where(kv0 if g == 0 else ~kv0, qt, jnp.zeros_like(qt)), keys)

        pending = [scores(*heads[n]) for n in range(AT_LOOKAHEAD)]
        outs = {}
        for n, (j, g) in enumerate(heads):
            s = pending.pop(0)
            if n + AT_LOOKAHEAD < len(heads):
                pending.append(scores(*heads[n + AT_LOOKAHEAD]))
            p = jnp.exp2(s - jnp.max(s, axis=1, keepdims=True))
            outs[g] = _dot(p.astype(BF16), v_scr[g, 0:n_keys, :])
            if g == AT_KV_HEADS - 1:
                num = jnp.where(kv0, outs[0], outs[1])
                den = pltpu.roll(jnp.where(kv0, outs[1], outs[0]), AT_DIM, axis=1)
                o_ref[0, :, j * LANE:(j + 1) * LANE] = (num / den).astype(o_ref.dtype)

    n_all = k_scr.shape[0]
    if q_off < n_ctx_tiles:
        @pl.when(i + q_off < n_ctx_tiles)
        def _():
            attend(n_ctx_rows)

        @pl.when(i + q_off >= n_ctx_tiles)
        def _():
            attend(n_all)
    else:
        attend(n_all)


def _attention(p_rest, col_q, col_k, col_v, tabs, gq, gk, q_off, n_ctx_rows):
    b, s, _ = p_rest.shape
    nt = s // ROW_TILE
    cq, sq, ck, sk = tabs
    mq = jnp.asarray(np.kron(np.eye(AT_HEADS), np.full((AT_DIM, AT_DIM), 1.0 / AT_DIM)), BF16)
    mk = jnp.asarray(np.kron(np.eye(AT_KV_HEADS), np.full((AT_DIM, AT_DIM), 1.0 / AT_DIM)), BF16)
    kern = functools.partial(_attn_kernel, q_off=q_off, n_ctx_tiles=n_ctx_rows // ROW_TILE,
                             n_ctx_rows=n_ctx_rows)
    return pl.pallas_call(
        kern,
        grid=(b, nt - q_off),
        in_specs=[pl.BlockSpec((1, ROW_TILE, AT_WIDTH), lambda bi, i: (bi, i + q_off, col_q)),
                  pl.BlockSpec((1, s, AT_KV_WIDTH), lambda bi, i: (bi, 0, col_k)),
                  pl.BlockSpec((1, s, AT_KV_WIDTH), lambda bi, i: (bi, 0, col_v)),
                  pl.BlockSpec((ROW_TILE, AT_WIDTH), lambda bi, i: (i + q_off, 0)),
                  pl.BlockSpec((ROW_TILE, AT_WIDTH), lambda bi, i: (i + q_off, 0)),
                  pl.BlockSpec((s, AT_KV_WIDTH), lambda bi, i: (0, 0)),
                  pl.BlockSpec((s, AT_KV_WIDTH), lambda bi, i: (0, 0)),
                  pl.BlockSpec((1, AT_WIDTH), lambda bi, i: (0, 0)),
                  pl.BlockSpec((1, AT_KV_WIDTH), lambda bi, i: (0, 0)),
                  pl.BlockSpec((AT_WIDTH, AT_WIDTH), lambda bi, i: (0, 0)),
                  pl.BlockSpec((AT_KV_WIDTH, AT_KV_WIDTH), lambda bi, i: (0, 0))],
        out_specs=pl.BlockSpec((1, ROW_TILE, AT_WIDTH), lambda bi, i: (bi, i + q_off, 0)),
        out_shape=jax.ShapeDtypeStruct((b, s, AT_WIDTH), BF16),
        scratch_shapes=[pltpu.VMEM((s, AT_KV_WIDTH), BF16),
                        pltpu.VMEM((AT_KV_HEADS, s, AT_KV_WIDTH), BF16)],
        compiler_params=_cparams(2, 48),
        name="attention",
    )(p_rest, p_rest, p_rest, cq, sq, ck, sk, gq, gk, mq, mk)


def _rope_tables(n_ctx_rows, n_lat_rows):
    rows = n_lat_rows // GRID_W
    row = jnp.repeat(jnp.arange(rows), GRID_W).astype(F32)
    col = jnp.tile(jnp.arange(GRID_W), rows).astype(F32)
    n_freq = AT_DIM // 4
    inv = ROPE_THETA ** (-jnp.arange(n_freq, dtype=F32) / n_freq)
    ang = jnp.concatenate([row[:, None] * inv, col[:, None] * inv], axis=-1)
    cos = jnp.concatenate([jnp.cos(ang), jnp.cos(ang)], axis=-1)
    sin = jnp.concatenate([-jnp.sin(ang), jnp.sin(ang)], axis=-1)
    cos = jnp.concatenate([jnp.ones((n_ctx_rows, AT_DIM), F32), cos], axis=0)
    sin = jnp.concatenate([jnp.zeros((n_ctx_rows, AT_DIM), F32), sin], axis=0)
    return (jnp.tile(cos, (1, AT_HEADS)), jnp.tile(sin, (1, AT_HEADS)),
            jnp.tile(cos, (1, AT_KV_HEADS)), jnp.tile(sin, (1, AT_KV_HEADS)))


def _shift_matrices(n):
    i = np.arange(n)
    down = i[:, None] - 1 == i[None, :]
    up = i[:, None] + 1 == i[None, :]
    return jnp.asarray(np.stack([down, up]).astype(np.float32), BF16)


def _conv3(xb, shift_ref, prev_row, next_row, w, bias):
    n, c = xb.shape
    sub = 8
    r = lax.broadcasted_iota(jnp.int32, (sub, c), 0)
    x = xb.astype(F32)
    if shift_ref is None:
        xm = pltpu.roll(x, 1, axis=0)
        xp = pltpu.roll(x, n - 1, axis=0)
    else:
        xm = _dot(shift_ref[0], xb)
        xp = _dot(shift_ref[1], xb)
    xm = jnp.concatenate([jnp.where(r == 0, prev_row, xm[0:sub]), xm[sub:]], axis=0)
    xp = jnp.concatenate([xp[:n - sub], jnp.where(r == sub - 1, next_row, xp[n - sub:])], axis=0)
    return xm * w[0:1] + x * w[1:2] + xp * w[2:3] + bias


def _halo_specs(width, col, row_off, n_rows, rows=ROW_TILE, samples=1):
    per = rows // BF16_SUBLANES
    last = n_rows // BF16_SUBLANES - 1
    return [
        pl.BlockSpec((samples, rows, width), lambda bi, i: (bi, i + row_off, col)),
        pl.BlockSpec((samples, BF16_SUBLANES, width),
                     lambda bi, i: (bi, jnp.maximum((i + row_off) * per - 1, 0), col)),
        pl.BlockSpec((samples, BF16_SUBLANES, width),
                     lambda bi, i: (bi, jnp.minimum((i + row_off + 1) * per, last), col)),
    ]


def _halo_rows(prev_ref, next_ref, is_first, is_last, b):
    prev_row = prev_ref[b, BF16_SUBLANES - 1:BF16_SUBLANES, :].astype(F32)
    next_row = next_ref[b, 0:1, :].astype(F32)
    prev_row = jnp.where(is_first, 0.0, prev_row)
    next_row = jnp.where(is_last, 0.0, next_row)
    return prev_row, next_row


HY_PRE_SAMPLES = 2


def _hypre_kernel(z_ref, zp_ref, zn_ref, sh_ref, w_ref, b_ref, db_ref, u_ref, ud_ref, x0_ref):
    i = pl.program_id(1)
    for b in range(z_ref.shape[0]):
        prev_row, next_row = _halo_rows(zp_ref, zn_ref, i == 0, i == pl.num_programs(1) - 1, b)
        zc = _conv3(z_ref[b], sh_ref, prev_row, next_row, w_ref[...], b_ref[...])
        x0 = zc[:, :HY_WIDTH]
        x1 = zc[:, HY_WIDTH:2 * HY_WIDTH]
        v = zc[:, 2 * HY_WIDTH:]
        u = v * x1
        u_ref[b] = u.astype(BF16)
        ud_ref[b] = (u * db_ref[...]).astype(BF16)
        x0_ref[b] = x0.astype(BF16)


def _hyena_pre(p_rest, col, row_off, n_rows, conv_w, conv_b, d_bias):
    b, s, _ = p_rest.shape
    width = 3 * HY_WIDTH
    nb = HY_PRE_SAMPLES if b % HY_PRE_SAMPLES == 0 else 1
    out = jax.ShapeDtypeStruct((b, n_rows, HY_WIDTH), BF16)
    ospec = pl.BlockSpec((nb, ROW_TILE, HY_WIDTH), lambda bi, i: (bi, i, 0))
    return pl.pallas_call(
        _hypre_kernel,
        grid=(b // nb, n_rows // ROW_TILE),
        in_specs=_halo_specs(width, col, row_off, s, samples=nb) + [
            pl.BlockSpec((2, ROW_TILE, ROW_TILE), lambda bi, i: (0, 0, 0)),
            pl.BlockSpec((3, width), lambda bi, i: (0, 0)),
            pl.BlockSpec((1, width), lambda bi, i: (0, 0)),
            pl.BlockSpec((1, HY_WIDTH), lambda bi, i: (0, 0))],
        out_specs=[ospec, ospec, ospec],
        out_shape=[out, out, out],
        compiler_params=_cparams(2, 32),
        name="hyena_pre",
    )(p_rest, p_rest, p_rest, _shift_matrices(ROW_TILE), conv_w, conv_b.reshape(1, width),
      d_bias.reshape(1, HY_WIDTH))


def _hyfilt_kernel(z_ref, t_ref, dl_ref, w1_ref, b1_ref, wi_ref, bi_ref, fr_ref, wl_ref, o_ref):
    fr = fr_ref[...]
    h = jnp.sin(fr * (_dot3(z_ref[...], w1_ref[...]) + b1_ref[...]))
    for j in range(HY_INNER):
        h = jnp.sin(fr * (_dot3(h, wi_ref[j]) + bi_ref[j]))
    h = _dot3(h, wl_ref[...])
    decay = jnp.exp(-t_ref[...] * dl_ref[...])
    hf = h[:, :HY_WIDTH] * decay
    hb = h[:, HY_WIDTH:] * decay
    o_ref[...] = jnp.concatenate([hf + hb, hf - hb], axis=1)


def _pad2(a, rows, cols):
    return jnp.pad(a, ((0, rows - a.shape[0]), (0, cols - a.shape[1])))


def _hyena_filter_sums(n, w1, b1, wi, bi, freq, w_last):
    t = jnp.linspace(0.0, 1.0, n, dtype=F32)[:, None]
    w = 2.0 * math.pi * jnp.arange(n, dtype=F32)[:, None] / n
    f = jnp.linspace(1e-4, HY_BANDS - 1, HY_BANDS, dtype=F32)[None, :]
    z = jnp.concatenate([t, jnp.cos(f * w), -jnp.sin(f * w)], axis=-1)
    max_decay = math.log(HY_TARGET) / HY_FAST_DECAY
    min_decay = math.log(HY_TARGET) / HY_SLOW_DECAY
    deltas = jnp.abs(jnp.linspace(min_decay, max_decay, HY_WIDTH, dtype=F32))[None, :]
    zp = _pad2(z, n, LANE)
    w1p = _pad2(w1, LANE, LANE)
    b1p = _pad2(b1[None, :], 1, LANE)
    wip = jnp.stack([_pad2(wi[j], LANE, LANE) for j in range(HY_INNER)])
    bip = jnp.stack([_pad2(bi[j][None, :], 1, LANE) for j in range(HY_INNER)])
    frp = _pad2(freq[None, :], 1, LANE)
    wlp = _pad2(w_last, LANE, 2 * HY_WIDTH)
    tr = min(n, ROW_TILE)
    full = lambda shape: pl.BlockSpec(shape, lambda i: (0,) * len(shape))
    return pl.pallas_call(
        _hyfilt_kernel,
        grid=(n // tr,),
        in_specs=[pl.BlockSpec((tr, LANE), lambda i: (i, 0)),
                  pl.BlockSpec((tr, 1), lambda i: (i, 0)),
                  full((1, HY_WIDTH)), full((LANE, LANE)), full((1, LANE)),
                  full((HY_INNER, LANE, LANE)), full((HY_INNER, 1, LANE)), full((1, LANE)),
                  full((LANE, 2 * HY_WIDTH))],
        out_specs=pl.BlockSpec((tr, 2 * HY_WIDTH), lambda i: (i, 0)),
        out_shape=jax.ShapeDtypeStruct((n, 2 * HY_WIDTH), F32),
        compiler_params=_cparams(1, 32),
        name="hyena_filter",
    )(zp, t, deltas, w1p, b1p, wip, bip, frp, wlp)


def _dft_tables(n):
    f = jnp.arange(n, dtype=jnp.int32)[:, None]
    t = jnp.arange(n, dtype=jnp.int32)[None, :]
    ang = ((f * t) % (2 * n)).astype(F32) * (math.pi / n)
    nyq = jnp.where(t % 2 == 0, 1.0, -1.0).astype(F32)
    return jnp.stack([jnp.cos(ang), jnp.where(f == 0, nyq, jnp.sin(ang))])


def _hyfwd_kernel(u_ref, f_ref, co_ref, o_ref):
    u = u_ref[0]
    ure = _dot(f_ref[0], u)
    uim = _dot(f_ref[1], u)
    o_ref[0, 0] = (ure * co_ref[0] - uim * co_ref[1]).astype(BF16)
    o_ref[0, 1] = (ure * co_ref[2] + uim * co_ref[3]).astype(BF16)


def _hyinv_kernel(y_ref, ft_ref, ud_ref, x0_ref, o_ref):
    y = _dot(ft_ref[0], y_ref[0, 0]) + _dot(ft_ref[1], y_ref[0, 1])
    o_ref[0] = ((y + ud_ref[0].astype(F32)) * x0_ref[0].astype(F32)).astype(BF16)


def _hyena_conv(u, ud, x0, tables, coef):
    b, n, c = u.shape
    tf = min(n, DFT_ROWS)
    f_bf = tables.astype(BF16)
    ft_bf = jnp.swapaxes(tables, 1, 2).astype(BF16)
    spec = pl.pallas_call(
        _hyfwd_kernel,
        grid=(n // tf, b),
        in_specs=[pl.BlockSpec((1, n, c), lambda j, bi: (bi, 0, 0)),
                  pl.BlockSpec((2, tf, n), lambda j, bi: (0, j, 0)),
                  pl.BlockSpec((4, tf, c), lambda j, bi: (0, j, 0))],
        out_specs=pl.BlockSpec((1, 2, tf, c), lambda j, bi: (bi, 0, j, 0)),
        out_shape=jax.ShapeDtypeStruct((b, 2, n, c), BF16),
        compiler_params=_cparams(2, 40),
        name="hyena_dft",
    )(u, f_bf, coef)
    return pl.pallas_call(
        _hyinv_kernel,
        grid=(b, n // tf),
        in_specs=[pl.BlockSpec((1, 2, n, c), lambda bi, j: (bi, 0, 0, 0)),
                  pl.BlockSpec((2, tf, n), lambda bi, j: (0, j, 0)),
                  pl.BlockSpec((1, tf, c), lambda bi, j: (bi, j, 0)),
                  pl.BlockSpec((1, tf, c), lambda bi, j: (bi, j, 0))],
        out_specs=pl.BlockSpec((1, tf, c), lambda bi, j: (bi, j, 0)),
        out_shape=jax.ShapeDtypeStruct((b, n, c), BF16),
        compiler_params=_cparams(2, 40),
        name="hyena_idft",
    )(spec, ft_bf, ud, x0)


def _hyena_coef(tables, hsum_hdiff):
    n = tables.shape[1]
    c = HY_WIDTH
    r = _matmul_f32(tables.reshape(2 * n, n), hsum_hdiff, min(n, ROW_TILE), "hyena_kernel_dft")
    k_re = r[:n, :c]
    k_im = r[n:, c:]
    k_nyq = r[n:n + 1, :c]
    first = (jnp.arange(n) == 0)[:, None]
    scale = jnp.where(first, 1.0 / (2 * n), 2.0 / (2 * n)).astype(F32)
    zero = jnp.zeros_like(k_im)
    return jnp.stack([k_re * scale,
                      jnp.where(first, zero, k_im * scale),
                      jnp.where(first, zero, k_im * scale),
                      jnp.where(first, k_nyq, k_re) * scale])


def _merge_kernel(*refs, n_ctx_tiles, row_off, has_ctx):
    if has_ctx:
        (of_ref, ob_ref, zg_ref, att_ref, cx_ref, cc_ref, gate_ref, x_ref, m_ref, ghg_ref, gpost_ref,
         gffn_ref, woa_ref, wob_ref, woc_ref, wout_ref, o_ref, h_ref) = refs
    else:
        (of_ref, ob_ref, zg_ref, att_ref, cx_ref, gate_ref, x_ref, m_ref, ghg_ref, gpost_ref,
         gffn_ref, woa_ref, wob_ref, woc_ref, wout_ref, o_ref, h_ref) = refs
    nb, rows, d = x_ref.shape
    flat = lambda ref: ref[...].reshape(nb * rows, ref.shape[-1])
    o = flat(of_ref) + flat(ob_ref)
    ghg = ghg_ref[...]
    a = jnp.concatenate([_rms(o[:, h * HG_DIM:(h + 1) * HG_DIM], ghg) for h in range(HG_HEADS)], axis=1)
    zg = flat(zg_ref)
    a = a * (zg * _sigmoid(zg))
    c = flat(cx_ref)
    if has_ctx:
        c = jnp.where(pl.program_id(1) + row_off < n_ctx_tiles, flat(cc_ref), c)
    ya = _dot(a.astype(BF16), woa_ref[...])
    yb = _dot(flat(att_ref), wob_ref[...])
    yc = _dot(c, woc_ref[...])
    gates = flat(gate_ref)
    m = (_sigmoid(gates[:, 0:d].astype(F32)) * ya
         + _sigmoid(gates[:, d:2 * d].astype(F32)) * yb
         + _sigmoid(gates[:, 2 * d:3 * d].astype(F32)) * yc)
    y = _rms(_dot(m.astype(BF16), wout_ref[...]), gpost_ref[...])
    for b in range(nb):
        x_new = x_ref[b] + m_ref[b, 2:3, :] * y[b * rows:(b + 1) * rows]
        o_ref[b] = x_new
        h_ref[b] = (_rms(x_new, gffn_ref[...]) * (1.0 + m_ref[b, 4:5, :]) + m_ref[b, 3:4, :]).astype(BF16)


def _merge(o_f, o_b, p_hg, att, c_x, c_c, p_rest, xs, mods, g_hg, g_post, g_ffn, w_oa, w_ob, w_oc, w_out,
           row_off, n_ctx_rows):
    b, s, d = xs.shape
    nct = n_ctx_rows // ROW_TILE
    n_tiles = s // ROW_TILE - row_off
    has_ctx = c_c is not None
    nb = MERGE_SAMPLES if b % MERGE_SAMPLES == 0 else 1
    ctx_blk = b // nb

    def stream(width, col=0):
        return pl.BlockSpec((nb, ROW_TILE, width), lambda bi, i: (bi, i + row_off, col))

    def full(shape):
        return pl.BlockSpec(shape, lambda bi, i: (0,) * len(shape))

    in_specs = [stream(HG_WIDTH), stream(HG_WIDTH), stream(HG_WIDTH, 4), stream(AT_WIDTH),
                pl.BlockSpec((nb, ROW_TILE, HY_WIDTH),
                             lambda bi, i: (bi, jnp.maximum(i + row_off - nct, 0), 0))]
    args = [o_f, o_b, p_hg, att, c_x]
    if has_ctx:
        in_specs.append(pl.BlockSpec((nb, ROW_TILE, HY_WIDTH),
                                     lambda bi, i: (bi, jnp.minimum(i + row_off, nct - 1), 0)))
        args.append(c_c)
    in_specs += [stream(3 * d), stream(d),
                 pl.BlockSpec((nb, 6, d), lambda bi, i: (jnp.where(i + row_off < nct, ctx_blk, bi), 0, 0)),
                 full((1, HG_DIM)), full((1, d)), full((1, d)),
                 full((HG_WIDTH, d)), full((AT_WIDTH, d)), full((HY_WIDTH, d)), full((d, d))]
    args += [p_rest, xs, mods, g_hg.reshape(1, HG_DIM), g_post.reshape(1, d), g_ffn.reshape(1, d),
             w_oa, w_ob, w_oc, w_out]
    ospec = pl.BlockSpec((nb, ROW_TILE, d), lambda bi, i: (bi, i, 0))
    return pl.pallas_call(
        functools.partial(_merge_kernel, n_ctx_tiles=nct, row_off=row_off, has_ctx=has_ctx),
        grid=(b // nb, n_tiles),
        in_specs=in_specs,
        out_specs=[ospec, ospec],
        out_shape=[jax.ShapeDtypeStruct((b, n_tiles * ROW_TILE, d), F32),
                   jax.ShapeDtypeStruct((b, n_tiles * ROW_TILE, d), BF16)],
        compiler_params=_cparams(2, 48),
        name="merge",
    )(*args)


FFN_COLS = 256
FFN_DOWN_GROUPS = 2


def _ffn_kernel(*refs, first_tiles, last_tiles, has_next):
    if has_next:
        (h_ref, hp_ref, hn_ref, wu_ref, w_ref, b_ref, x_ref, m_ref, g_ref, wd_ref, mn_ref, gn_ref,
         o_ref, hx_ref, act_ref) = refs
    else:
        h_ref, hp_ref, hn_ref, wu_ref, w_ref, b_ref, x_ref, m_ref, g_ref, wd_ref, o_ref, act_ref = refs
    i = pl.program_id(1)
    is_first = functools.reduce(jnp.logical_or, [i == t for t in first_tiles])
    is_last = functools.reduce(jnp.logical_or, [i == t for t in last_tiles])
    d_ff = wd_ref.shape[0]
    nb, rows, _ = h_ref.shape
    halo = BF16_SUBLANES
    ext = rows + 2 * halo
    pieces = []
    for b in range(nb):
        pieces += [jnp.where(is_first, jnp.zeros_like(hp_ref[b]), hp_ref[b]), h_ref[b],
                   jnp.where(is_last, jnp.zeros_like(hn_ref[b]), hn_ref[b])]
    h_ext = jnp.concatenate(pieces, axis=0)

    def up(j):
        return [_dot(h_ext, wu_ref[:, base + j * FFN_COLS:base + (j + 1) * FFN_COLS]) for base in (0, d_ff)]

    def conv(u, cols):
        w = w_ref[:, cols]
        full = pltpu.roll(u, 1, axis=0) * w[0:1] + u * w[1:2] + pltpu.roll(u, nb * ext - 1, axis=0) * w[2:3]
        kept = [full[b * ext + halo:b * ext + halo + rows] for b in range(nb)]
        return (kept[0] if nb == 1 else jnp.concatenate(kept, axis=0)) + b_ref[:, cols]

    n_chunks = d_ff // FFN_COLS
    per_group = -(-n_chunks // FFN_DOWN_GROUPS)
    acc = None
    u_next = up(0)
    for j in range(n_chunks):
        u = u_next
        if j + 1 < n_chunks:
            u_next = up(j + 1)
        a = conv(u[0], slice(j * FFN_COLS, (j + 1) * FFN_COLS))
        g = conv(u[1], slice(d_ff + j * FFN_COLS, d_ff + (j + 1) * FFN_COLS))
        act_ref[:, j * FFN_COLS:(j + 1) * FFN_COLS] = (a * _sigmoid(a) * g).astype(BF16)
        if (j + 1) % per_group == 0 or j + 1 == n_chunks:
            lo = (j // per_group) * per_group * FFN_COLS
            part = _dot(act_ref[:, lo:(j + 1) * FFN_COLS], wd_ref[lo:(j + 1) * FFN_COLS, :])
            acc = part if acc is None else acc + part
    y = _rms(acc, g_ref[...])
    for b in range(nb):
        x_new = x_ref[b] + m_ref[b, 5:6, :] * y[b * rows:(b + 1) * rows]
        o_ref[b] = x_new
        if has_next:
            hx_ref[b] = (_rms(x_new, gn_ref[...]) * (1.0 + mn_ref[b, 1:2, :])
                         + mn_ref[b, 0:1, :]).astype(BF16)


def _ffn(h, xs, mods, w_up, conv_w, conv_b, g_post, w_down, n_ctx_rows, mods_next=None, g_next=None):
    b, s, d = xs.shape
    d_ff = w_down.shape[0]
    rows = FFN_ROWS if (n_ctx_rows % FFN_ROWS == 0 and s % FFN_ROWS == 0) else ROW_TILE
    nt = s // rows
    nct = n_ctx_rows // rows
    first_tiles = tuple(sorted({0, nct}))
    last_tiles = tuple(sorted({nct - 1, nt - 1} - {-1}))
    nb = FFN_ROWS // rows if b % (FFN_ROWS // rows) == 0 and FFN_ROWS // rows <= MERGE_SAMPLES else 1
    ctx_blk = b // nb
    has_next = mods_next is not None
    full = lambda shape: pl.BlockSpec(shape, lambda bi, i: (0,) * len(shape))
    mspec = pl.BlockSpec((nb, 6, d), lambda bi, i: (jnp.where(i < nct, ctx_blk, bi), 0, 0))
    ospec = pl.BlockSpec((nb, rows, d), lambda bi, i: (bi, i, 0))
    resident = lambda shape: pl.BlockSpec(shape, lambda bi, i: (0,) * len(shape), pipeline_mode=pl.Buffered(1))
    in_specs = _halo_specs(d, 0, 0, s, rows, nb) + [
        resident((d, 2 * d_ff)), full((3, 2 * d_ff)), full((1, 2 * d_ff)),
        ospec, mspec, full((1, d)), resident((d_ff, d))]
    args = [h, h, h, w_up, conv_w, conv_b.reshape(1, 2 * d_ff), xs, mods, g_post.reshape(1, d), w_down]
    out_specs = [ospec]
    out_shape = [jax.ShapeDtypeStruct((b, s, d), F32)]
    if has_next:
        in_specs += [mspec, full((1, d))]
        args += [mods_next, g_next.reshape(1, d)]
        out_specs.append(ospec)
        out_shape.append(jax.ShapeDtypeStruct((b, s, d), BF16))
    return pl.pallas_call(
        functools.partial(_ffn_kernel, first_tiles=first_tiles, last_tiles=last_tiles, has_next=has_next),
        grid=(b // nb, nt),
        in_specs=in_specs,
        out_specs=out_specs,
        out_shape=out_shape,
        scratch_shapes=[pltpu.VMEM((nb * rows, d_ff), BF16)],
        compiler_params=_cparams(2, 56),
        name="ffn",
    )(*args)


def _deinterleave():
    return np.concatenate([np.arange(0, AT_DIM, 2), np.arange(1, AT_DIM, 2)])


def _q_head_order():
    return [h for j in range(AT_GROUP) for h in (j, AT_GROUP + j)]


def _largest_tile(n, cap):
    best = LANE
    for t in range(LANE, cap + 1, LANE):
        if n % t == 0:
            best = t
    return best


def kernel(x, c, ctx, c_ctx, w_ada, b_ada, g_pre_mix, g_post_mix, g_pre_ffn, g_post_ffn, w_in, hg_lower_bounds, hg_norm, q_norm, k_norm, hy_conv_w, hy_conv_b, hy_w1, hy_b1, hy_wi, hy_bi, hy_freq, hy_w_last, hy_bias, w_oa, w_ob, w_oc, w_out, w_up, ffn_conv_w, ffn_conv_b, w_down):
    bsz, n_lat, d = x.shape
    n_ctx = ctx.shape[1]
    depth = w_ada.shape[0]
    d_ff = w_down.shape[1]
    assert AT_KV_HEADS == 2 and AT_GROUP * LANE == AT_WIDTH and AT_KV_WIDTH == LANE
    assert n_ctx % ROW_TILE == 0 and n_lat % ROW_TILE == 0 and n_lat % GRID_W == 0
    assert (bsz * (n_ctx + n_lat)) % MM_ROWS == 0 and (bsz * n_lat) % MM_ROWS == 0

    lbp = jax.nn.softmax(hg_lower_bounds.astype(F32), axis=0)
    lower = jnp.cumsum(lbp, axis=0) - lbp[0]

    rp = -(-(bsz + MERGE_SAMPLES) // 8) * 8
    src = jnp.concatenate([c, jnp.tile(c_ctx[None, :], (MERGE_SAMPLES, 1)),
                           jnp.zeros((rp - bsz - MERGE_SAMPLES, d), F32)], axis=0)
    mods_all = _ada(src, w_ada, b_ada).reshape(depth, rp, 6, d)

    o_q = 5 * HG_WIDTH
    o_k = o_q + AT_WIDTH
    o_v = o_k + AT_KV_WIDTH
    o_hy = o_v + AT_KV_WIDTH
    o_gate = o_hy + 3 * HY_WIDTH
    deint = _deinterleave()
    q_cols = np.concatenate([o_q + h * AT_DIM + deint for h in _q_head_order()])
    k_cols = np.concatenate([o_k + g * AT_DIM + deint for g in range(AT_KV_HEADS)])
    qk_cols = np.concatenate([q_cols, k_cols])
    col_hy = (3 * d) // (3 * HY_WIDTH)
    col_q = (3 * d + 3 * HY_WIDTH) // AT_WIDTH
    col_k = (3 * d + 3 * HY_WIDTH + AT_WIDTH) // AT_KV_WIDTH
    col_v = col_k + 1
    assert (3 * d) % (3 * HY_WIDTH) == 0 and (3 * d + 3 * HY_WIDTH) % AT_WIDTH == 0
    ob_rows = np.concatenate([np.arange(h * AT_DIM, (h + 1) * AT_DIM) for h in _q_head_order()])

    rope_tabs = _rope_tables(n_ctx, n_lat)
    dft_lat = _dft_tables(n_lat)
    dft_ctx = _dft_tables(n_ctx)
    nct = n_ctx // ROW_TILE

    s_all = n_ctx + n_lat
    xs, h = _join_modulate(ctx, x, mods_all[0], g_pre_mix[0], 0, 1)
    for l in range(depth):
        need_ctx = l < depth - 1
        mods = mods_all[l]
        w_hg = w_in[l][:, :5 * HG_WIDTH].astype(BF16)
        w_rest = jnp.concatenate([w_in[l][:, o_gate:o_gate + 3 * d], w_in[l][:, o_hy:o_hy + 3 * HY_WIDTH],
                                  w_in[l][:, qk_cols], w_in[l][:, o_v:o_v + AT_KV_WIDTH]], axis=1).astype(BF16)

        h = h.reshape(bsz * s_all, d)
        p_hg = _matmul(h, w_hg, F32, _largest_tile(5 * HG_WIDTH, 1280), "proj_hgrn").reshape(bsz, s_all, -1)
        p_rest = _matmul(h, w_rest, BF16, _largest_tile(w_rest.shape[1], 1792), "proj_rest").reshape(bsz, s_all, -1)

        o_f, o_b = _hgrn(p_hg, lower[l], n_ctx)

        gq = jnp.tile(q_norm[l][deint], AT_HEADS)[None, :]
        gk = jnp.tile(k_norm[l][deint], AT_KV_HEADS)[None, :]
        row_off = 0 if need_ctx else nct
        att = _attention(p_rest, col_q, col_k, col_v, rope_tabs, gq, gk, row_off, n_ctx)

        filt_args = (hy_w1[l], hy_b1[l], hy_wi[l], hy_bi[l], hy_freq[l], hy_w_last[l])
        coef = _hyena_coef(dft_lat, _hyena_filter_sums(n_lat, *filt_args))
        c_x = _hyena_conv(*_hyena_pre(p_rest, col_hy, nct, n_lat, hy_conv_w[l], hy_conv_b[l], hy_bias[l]),
                          dft_lat, coef)
        c_c = None
        if need_ctx:
            coef_c = _hyena_coef(dft_ctx, _hyena_filter_sums(n_ctx, *filt_args))
            c_c = _hyena_conv(*_hyena_pre(p_rest, col_hy, 0, n_ctx, hy_conv_w[l], hy_conv_b[l], hy_bias[l]),
                              dft_ctx, coef_c)

        xs, h2 = _merge(o_f, o_b, p_hg, att, c_x, c_c, p_rest, xs, mods, hg_norm[l], g_post_mix[l],
                        g_pre_ffn[l], w_oa[l].astype(BF16), w_ob[l][ob_rows].astype(BF16),
                        w_oc[l].astype(BF16), w_out[l].astype(BF16), row_off, n_ctx)
        n_ctx_now = n_ctx if need_ctx else 0
        ffn_args = (h2, xs, mods, w_up[l].astype(BF16), ffn_conv_w[l], ffn_conv_b[l], g_post_ffn[l],
                    w_down[l].astype(BF16), n_ctx_now)
        if need_ctx:
            xs, h = _ffn(*ffn_args, mods_all[l + 1], g_pre_mix[l + 1])
        else:
            xs, = _ffn(*ffn_args)
    return xs
```

```python
import functools
import math

import jax
import jax.numpy as jnp
import numpy as np
from jax import lax
from jax.experimental import pallas as pl
from jax.experimental.pallas import tpu as pltpu

F32 = jnp.float32
BF16 = jnp.bfloat16

NORM_EPS = 1e-6
GRID_W = 64
HG_HEADS = 4
HG_DIM = 128
HG_WIDTH = HG_HEADS * HG_DIM
HG_EXP_CLIP = 30.0
AT_HEADS = 8
AT_KV_HEADS = 2
AT_DIM = 64
AT_GROUP = AT_HEADS // AT_KV_HEADS
AT_WIDTH = AT_HEADS * AT_DIM
AT_KV_WIDTH = AT_KV_HEADS * AT_DIM
ROPE_THETA = 10000.0
HY_WIDTH = 512
HY_EMB_DIM = 33
HY_BANDS = (HY_EMB_DIM - 1) // 2
HY_FILTER_WIDTH = 64
HY_INNER = 2
HY_FAST_DECAY = 0.3
HY_SLOW_DECAY = 1.5
HY_TARGET = 1e-2

LANE = 128
BF16_SUBLANES = 16
ROW_TILE = 256
MERGE_SAMPLES = 2
HG_CHUNK = 128
HG_BLOCK = 256
HG_LEVELS = tuple(HG_CHUNK >> (j + 1) for j in range(int(math.log2(HG_CHUNK))))
MM_ROWS = 1024
DFT_ROWS = 512
DFT_T_LO = 64
AT_LOOKAHEAD = 2
FFN_ROWS = 512
VMEM_CAP = 56 * 1024 * 1024


def _cparams(n_axes, vmem_mb):
    return pltpu.CompilerParams(
        dimension_semantics=("arbitrary",) * n_axes,
        vmem_limit_bytes=min(int(vmem_mb) * 1024 * 1024, VMEM_CAP))


def _dot(a, b):
    return jnp.dot(a, b, preferred_element_type=F32)


def _dot_nt(a, b):
    return lax.dot_general(a, b, (((1,), (1,)), ((), ())), preferred_element_type=F32)


def _split_bf16(a):
    hi = a.astype(BF16)
    lo = (a - hi.astype(F32)).astype(BF16)
    return hi, lo


def _dot3(a, b):
    ah, al = _split_bf16(a)
    bh, bl = _split_bf16(b)
    return _dot(ah, bh) + (_dot(ah, bl) + _dot(al, bh))


def _rms(x, g):
    return x * lax.rsqrt(jnp.mean(x * x, axis=-1, keepdims=True) + NORM_EPS) * g


def _sigmoid(x):
    return 0.5 * jnp.tanh(0.5 * x) + 0.5


def _ada_kernel(src_ref, w_ref, b_ref, o_ref):
    s = src_ref[...]
    s = s * _sigmoid(s)
    o_ref[0] = _dot3(s, w_ref[0]) + b_ref[0]


def _ada(src, w_ada, b_ada):
    depth, d, d6 = w_ada.shape
    rp = src.shape[0]
    tn = d
    return pl.pallas_call(
        _ada_kernel,
        grid=(depth, d6 // tn),
        in_specs=[pl.BlockSpec((rp, d), lambda l, j: (0, 0)),
                  pl.BlockSpec((1, d, tn), lambda l, j: (l, 0, j)),
                  pl.BlockSpec((1, 1, tn), lambda l, j: (l, 0, j))],
        out_specs=pl.BlockSpec((1, rp, tn), lambda l, j: (l, 0, j)),
        out_shape=jax.ShapeDtypeStruct((depth, rp, d6), F32),
        compiler_params=_cparams(2, 32),
        name="ada",
    )(src, w_ada, b_ada.reshape(depth, 1, d6))


def _mod_kernel(c_ref, x_ref, m_ref, g_ref, xs_ref, o_ref, *, k_shift, k_scale, n_ctx_tiles):
    for b in range(x_ref.shape[0]):
        x = jnp.where(pl.program_id(1) < n_ctx_tiles, c_ref[b], x_ref[b])
        xs_ref[b] = x
        shift = m_ref[b, k_shift:k_shift + 1, :]
        scale = m_ref[b, k_scale:k_scale + 1, :]
        o_ref[b] = (_rms(x, g_ref[...]) * (1.0 + scale) + shift).astype(o_ref.dtype)


def _join_modulate(ctx, x, mods, g, k_shift, k_scale):
    b, n_lat, d = x.shape
    n_ctx_tiles = ctx.shape[1] // ROW_TILE
    nt = n_ctx_tiles + n_lat // ROW_TILE
    nb = MERGE_SAMPLES if b % MERGE_SAMPLES == 0 else 1
    ctx_blk = b // nb
    ospec = pl.BlockSpec((nb, ROW_TILE, d), lambda bi, i: (bi, i, 0))
    return pl.pallas_call(
        functools.partial(_mod_kernel, k_shift=k_shift, k_scale=k_scale, n_ctx_tiles=n_ctx_tiles),
        grid=(b // nb, nt),
        in_specs=[pl.BlockSpec((nb, ROW_TILE, d), lambda bi, i: (bi, jnp.minimum(i, n_ctx_tiles - 1), 0)),
                  pl.BlockSpec((nb, ROW_TILE, d), lambda bi, i: (bi, jnp.maximum(i - n_ctx_tiles, 0), 0)),
                  pl.BlockSpec((nb, 6, d), lambda bi, i: (jnp.where(i < n_ctx_tiles, ctx_blk, bi), 0, 0)),
                  pl.BlockSpec((1, d), lambda bi, i: (0, 0))],
        out_specs=[ospec, ospec],
        out_shape=[jax.ShapeDtypeStruct((b, nt * ROW_TILE, d), F32),
                   jax.ShapeDtypeStruct((b, nt * ROW_TILE, d), BF16)],
        compiler_params=_cparams(2, 16),
        name="modulate",
    )(ctx, x, mods, g.reshape(1, d))


def _mm_kernel(a_ref, b_ref, o_ref):
    o_ref[...] = _dot(a_ref[...], b_ref[...]).astype(o_ref.dtype)


def _matmul(a, w, out_dtype, tn, name):
    m, k = a.shape
    n = w.shape[1]
    tm = MM_ROWS
    assert m % tm == 0 and n % tn == 0
    return pl.pallas_call(
        _mm_kernel,
        grid=(m // tm, n // tn),
        in_specs=[pl.BlockSpec((tm, k), lambda i, j: (i, 0)),
                  pl.BlockSpec((k, tn), lambda i, j: (0, j))],
        out_specs=pl.BlockSpec((tm, tn), lambda i, j: (i, j)),
        out_shape=jax.ShapeDtypeStruct((m, n), out_dtype),
        compiler_params=_cparams(2, 48),
        name=name,
    )(a, w)


def _mm3_kernel(a_ref, b_ref, o_ref):
    o_ref[...] = _dot3(a_ref[...], b_ref[...])


def _matmul_f32(a, w, tm, name):
    m, k = a.shape
    n = w.shape[1]
    return pl.pallas_call(
        _mm3_kernel,
        grid=(m // tm,),
        in_specs=[pl.BlockSpec((tm, k), lambda i: (i, 0)),
                  pl.BlockSpec((k, n), lambda i: (0, 0))],
        out_specs=pl.BlockSpec((tm, n), lambda i: (i, 0)),
        out_shape=jax.ShapeDtypeStruct((m, n), F32),
        compiler_params=_cparams(1, 48),
        name=name,
    )(a, w)


def _hg_scan_matrix(reverse):
    t_n = HG_CHUNK
    t = np.arange(t_n)[:, None]
    u = np.arange(t_n)[None, :]
    rows = [(u >= t) if reverse else (u <= t)]
    for w in HG_LEVELS:
        base = (t // (2 * w)) * (2 * w)
        mid = base + w
        upper = (t - base) >= w
        if reverse:
            m = np.where(upper, (u >= mid) & (u < t), (u >= t) & (u < mid))
        else:
            m = np.where(upper, (u >= mid) & (u <= t), (u > t) & (u < mid))
        rows.append(m)
    m = np.concatenate(rows, axis=0).astype(np.float32)
    return np.concatenate([m, m], axis=1)


def _hgrn_kernel(qf_ref, zf_ref, vf_ref, qb_ref, zb_ref, vb_ref, lb_ref, pf_ref, pb_ref,
                 of_ref, ob_ref, s_ref):
    @pl.when(pl.program_id(1) == 0)
    def _():
        s_ref[...] = jnp.zeros_like(s_ref)

    t_n = HG_CHUNK
    ti = lax.broadcasted_iota(jnp.int32, (t_n, t_n), 0)
    si = lax.broadcasted_iota(jnp.int32, (t_n, t_n), 1)
    tx = ti ^ si
    dirs = ((qf_ref, zf_ref, vf_ref, pf_ref, of_ref), (qb_ref, zb_ref, vb_ref, pb_ref, ob_ref))
    masks = []
    for reverse in (False, True):
        later = (ti < si) if reverse else (ti > si)
        masks.append([later & (tx >= w) & (tx < 2 * w) for w in HG_LEVELS])
    n_sub = qf_ref.shape[1] // t_n
    heads = range(HG_HEADS)
    units = [(d, c, h) for d in range(2) for c in range(n_sub) for h in heads]
    cols = lambda h: slice(h * HG_DIM, (h + 1) * HG_DIM)
    rows = lambda c: slice(c * t_n, (c + 1) * t_n)

    kk, kb, qb, cat, x, ex = {}, {}, {}, {}, {}, {}
    a = {u: jnp.zeros((t_n, t_n), F32) for u in units}

    def gates(d, c, h):
        z = dirs[d][1][0, rows(c), cols(h)]
        lb = lb_ref[d:d + 1, cols(h)]
        e = jnp.exp(-jnp.abs(z))
        log_num = jnp.log(jnp.where(z >= 0.0, 1.0 + lb * e, e + lb))
        log_clip = z + jnp.log(1.0 + lb * math.exp(HG_EXP_CLIP))
        lf = jnp.where(z < -HG_EXP_CLIP, log_clip, log_num) - jnp.log(1.0 + e)
        half = 0.5 * (1.0 - lb)
        kk[d, c, h] = half - half * jnp.tanh(0.5 * z)
        kb[d, c, h] = kk[d, c, h].astype(BF16)
        qb[d, c, h] = dirs[d][0][0, rows(c), cols(h)].astype(BF16)
        hi, lo = _split_bf16(lf)
        cat[d, c, h] = jnp.concatenate([hi, lo], axis=0)

    def exponents(d):
        rhs = jnp.concatenate([cat[d, c, h] for c in range(n_sub) for h in heads], axis=1)
        xd = _dot(dirs[d][3][...], rhs)
        ed = jnp.exp(xd[t_n:]).astype(BF16)
        for c in range(n_sub):
            for h in heads:
                x[d, c, h] = xd[0:t_n, cols(c * HG_HEADS + h)]
                ex[d, c, h] = ed[:, cols(c * HG_HEADS + h)]

    def level(d, c, j):
        for h in heads:
            u = (d, c, h)
            ew = ex[u][j * t_n:(j + 1) * t_n]
            pw = _dot_nt(qb[u] * ew, kb[u] * ew)
            a[u] = jnp.where(masks[d][j], pw, a[u])

    def finish(d, c, h):
        u = (d, c, h)
        q = dirs[d][0][0, rows(c), cols(h)]
        v = dirs[d][2][0, rows(c), cols(h)]
        g = x[u]
        g_last = g[0:1] if d == 1 else g[t_n - 1:t_n]
        st = s_ref[d * HG_HEADS + h]
        dqk = jnp.sum(q * kk[u], axis=1, keepdims=True)
        o = (_dot(a[u].astype(BF16), v.astype(BF16)) + dqk * v
             + _dot_nt((q * jnp.exp(g)).astype(BF16), st.astype(BF16)))
        kd = (kk[u] * jnp.exp(g_last - g)).astype(BF16)
        dirs[d][4][0, rows(c), cols(h)] = o.astype(BF16)
        s_ref[d * HG_HEADS + h] = st * jnp.exp(g_last) + _dot(v.T.astype(BF16), kd)

    for d in range(2):
        for c in range(n_sub):
            for h in heads:
                gates(d, c, h)
        exponents(d)
        for j in range(len(HG_LEVELS)):
            for c in range(n_sub):
                level(d, c, j)
        for c in (range(n_sub) if d == 0 else reversed(range(n_sub))):
            for h in heads:
                finish(d, c, h)


def _hgrn(p_hg, lb, n_ctx_rows):
    b, s, _ = p_hg.shape
    assert s % HG_BLOCK == 0 and n_ctx_rows % HG_BLOCK == 0
    nb = s // HG_BLOCK
    nc = n_ctx_rows // HG_BLOCK

    def bidx(n):
        return jnp.where(n < nc, nc - 1 - n, nb - 1 - (n - nc))

    blk = (1, HG_BLOCK, HG_WIDTH)
    pf = jnp.asarray(_hg_scan_matrix(False)).astype(BF16)
    pb = jnp.asarray(_hg_scan_matrix(True)).astype(BF16)
    pshape = pf.shape
    return pl.pallas_call(
        _hgrn_kernel,
        grid=(b, nb),
        in_specs=[pl.BlockSpec(blk, lambda bi, n: (bi, n, 0)),
                  pl.BlockSpec(blk, lambda bi, n: (bi, n, 1)),
                  pl.BlockSpec(blk, lambda bi, n: (bi, n, 3)),
                  pl.BlockSpec(blk, lambda bi, n: (bi, bidx(n), 0)),
                  pl.BlockSpec(blk, lambda bi, n: (bi, bidx(n), 2)),
                  pl.BlockSpec(blk, lambda bi, n: (bi, bidx(n), 3)),
                  pl.BlockSpec((2, HG_WIDTH), lambda bi, n: (0, 0)),
                  pl.BlockSpec(pshape, lambda bi, n: (0, 0)),
                  pl.BlockSpec(pshape, lambda bi, n: (0, 0))],
        out_specs=[pl.BlockSpec(blk, lambda bi, n: (bi, n, 0)),
                   pl.BlockSpec(blk, lambda bi, n: (bi, bidx(n), 0))],
        out_shape=[jax.ShapeDtypeStruct((b, s, HG_WIDTH), BF16)] * 2,
        scratch_shapes=[pltpu.VMEM((2 * HG_HEADS, HG_DIM, HG_DIM), F32)],
        compiler_params=_cparams(2, 40),
        name="hgrn",
    )(p_hg, p_hg, p_hg, p_hg, p_hg, p_hg, lb, pf, pb)


def _rope(x, cos, sin_signed, first_half):
    n = x.shape[-1]
    half = AT_DIM // 2
    partner = jnp.where(first_half, pltpu.roll(x, n - half, axis=1), pltpu.roll(x, half, axis=1))
    return x * cos + partner * sin_signed


def _head_norm(x, gain, group_mean):
    ms = _dot((x * x).astype(BF16), group_mean)
    return x * lax.rsqrt(ms + NORM_EPS) * gain


def _attn_kernel(q_ref, k_ref, v_ref, cq_ref, sq_ref, ck_ref, sk_ref, gq_ref, gk_ref, mq_ref, mk_ref,
                 o_ref, k_scr, v_scr, *, q_off, n_ctx_tiles, n_ctx_rows):
    i = pl.program_id(1)

    @pl.when(i == 0)
    def _():
        kr = k_ref[0].astype(F32)
        lane = lax.broadcasted_iota(jnp.int32, kr.shape, 1)
        kn = _head_norm(kr, gk_ref[...], mk_ref[...])
        k_scr[...] = _rope(kn, ck_ref[...], sk_ref[...], (lane % AT_DIM) < AT_DIM // 2).astype(BF16)
        v = v_ref[0]
        one = jnp.ones_like(v)
        v_scr[0] = jnp.where(lane < AT_DIM, v, one)
        v_scr[1] = jnp.where(lane < AT_DIM, one, v)

    qr = q_ref[0].astype(F32)
    lane = lax.broadcasted_iota(jnp.int32, qr.shape, 1)
    qn = _head_norm(qr, gq_ref[...], mq_ref[...])
    qn = _rope(qn, cq_ref[...], sq_ref[...], (lane % AT_DIM) < AT_DIM // 2)
    qn = (qn * (AT_DIM ** -0.5 * math.log2(math.e))).astype(BF16)
    lane_t = lax.broadcasted_iota(jnp.int32, (ROW_TILE, LANE), 1)
    kv0 = lane_t < AT_DIM

    def attend(n_keys):
        keys = k_scr[0:n_keys, :]
        heads = [(j, g) for j in range(AT_WIDTH // LANE) for g in range(AT_KV_HEADS)]

        def scores(j, g):
            qt = qn[:, j * LANE:(j + 1) * LANE]
            return _dot_nt(jnp.where(kv0 if g == 0 else ~kv0, qt, jnp.zeros_like(qt)), keys)

        pending = [scores(*heads[n]) for n in range(AT_LOOKAHEAD)]
        outs = {}
        for n, (j, g) in enumerate(heads):
            s = pending.pop(0)
            if n + AT_LOOKAHEAD < len(heads):
                pending.append(scores(*heads[n + AT_LOOKAHEAD]))
            p = jnp.exp2(s - jnp.max(s, axis=1, keepdims=True))
            outs[g] = _dot(p.astype(BF16), v_scr[g, 0:n_keys, :])
            if g == AT_KV_HEADS - 1:
                num = jnp.where(kv0, outs[0], outs[1])
                den = pltpu.roll(jnp.where(kv0, outs[1], outs[0]), AT_DIM, axis=1)
                o_ref[0, :, j * LANE:(j + 1) * LANE] = (num / den).astype(o_ref.dtype)

    n_all = k_scr.shape[0]
    if q_off < n_ctx_tiles:
        @pl.when(i + q_off < n_ctx_tiles)
        def _():
            attend(n_ctx_rows)

        @pl.when(i + q_off >= n_ctx_tiles)
        def _():
            attend(n_all)
    else:
        attend(n_all)


def _attention(p_rest, col_q, col_k, col_v, tabs, gq, gk, q_off, n_ctx_rows):
    b, s, _ = p_rest.shape
    nt = s // ROW_TILE
    cq, sq, ck, sk = tabs
    mq = jnp.asarray(np.kron(np.eye(AT_HEADS), np.full((AT_DIM, AT_DIM), 1.0 / AT_DIM)), BF16)
    mk = jnp.asarray(np.kron(np.eye(AT_KV_HEADS), np.full((AT_DIM, AT_DIM), 1.0 / AT_DIM)), BF16)
    kern = functools.partial(_attn_kernel, q_off=q_off, n_ctx_tiles=n_ctx_rows // ROW_TILE,
                             n_ctx_rows=n_ctx_rows)
    return pl.pallas_call(
        kern,
        grid=(b, nt - q_off),
        in_specs=[pl.BlockSpec((1, ROW_TILE, AT_WIDTH), lambda bi, i: (bi, i + q_off, col_q)),
                  pl.BlockSpec((1, s, AT_KV_WIDTH), lambda bi, i: (bi, 0, col_k)),
                  pl.BlockSpec((1, s, AT_KV_WIDTH), lambda bi, i: (bi, 0, col_v)),
                  pl.BlockSpec((ROW_TILE, AT_WIDTH), lambda bi, i: (i + q_off, 0)),
                  pl.BlockSpec((ROW_TILE, AT_WIDTH), lambda bi, i: (i + q_off, 0)),
                  pl.BlockSpec((s, AT_KV_WIDTH), lambda bi, i: (0, 0)),
                  pl.BlockSpec((s, AT_KV_WIDTH), lambda bi, i: (0, 0)),
                  pl.BlockSpec((1, AT_WIDTH), lambda bi, i: (0, 0)),
                  pl.BlockSpec((1, AT_KV_WIDTH), lambda bi, i: (0, 0)),
                  pl.BlockSpec((AT_WIDTH, AT_WIDTH), lambda bi, i: (0, 0)),
                  pl.BlockSpec((AT_KV_WIDTH, AT_KV_WIDTH), lambda bi, i: (0, 0))],
        out_specs=pl.BlockSpec((1, ROW_TILE, AT_WIDTH), lambda bi, i: (bi, i + q_off, 0)),
        out_shape=jax.ShapeDtypeStruct((b, s, AT_WIDTH), BF16),
        scratch_shapes=[pltpu.VMEM((s, AT_KV_WIDTH), BF16),
                        pltpu.VMEM((AT_KV_HEADS, s, AT_KV_WIDTH), BF16)],
        compiler_params=_cparams(2, 48),
        name="attention",
    )(p_rest, p_rest, p_rest, cq, sq, ck, sk, gq, gk, mq, mk)


def _rope_tables(n_ctx_rows, n_lat_rows):
    rows = n_lat_rows // GRID_W
    row = jnp.repeat(jnp.arange(rows), GRID_W).astype(F32)
    col = jnp.tile(jnp.arange(GRID_W), rows).astype(F32)
    n_freq = AT_DIM // 4
    inv = ROPE_THETA ** (-jnp.arange(n_freq, dtype=F32) / n_freq)
    ang = jnp.concatenate([row[:, None] * inv, col[:, None] * inv], axis=-1)
    cos = jnp.concatenate([jnp.cos(ang), jnp.cos(ang)], axis=-1)
    sin = jnp.concatenate([-jnp.sin(ang), jnp.sin(ang)], axis=-1)
    cos = jnp.concatenate([jnp.ones((n_ctx_rows, AT_DIM), F32), cos], axis=0)
    sin = jnp.concatenate([jnp.zeros((n_ctx_rows, AT_DIM), F32), sin], axis=0)
    return (jnp.tile(cos, (1, AT_HEADS)), jnp.tile(sin, (1, AT_HEADS)),
            jnp.tile(cos, (1, AT_KV_HEADS)), jnp.tile(sin, (1, AT_KV_HEADS)))


def _shift_matrices(n):
    i = np.arange(n)
    down = i[:, None] - 1 == i[None, :]
    up = i[:, None] + 1 == i[None, :]
    return jnp.asarray(np.stack([down, up]).astype(np.float32), BF16)


def _conv3(xb, shift_ref, prev_row, next_row, w, bias):
    n, c = xb.shape
    sub = 8
    r = lax.broadcasted_iota(jnp.int32, (sub, c), 0)
    x = xb.astype(F32)
    if shift_ref is None:
        xm = pltpu.roll(x, 1, axis=0)
        xp = pltpu.roll(x, n - 1, axis=0)
    else:
        xm = _dot(shift_ref[0], xb)
        xp = _dot(shift_ref[1], xb)
    xm = jnp.concatenate([jnp.where(r == 0, prev_row, xm[0:sub]), xm[sub:]], axis=0)
    xp = jnp.concatenate([xp[:n - sub], jnp.where(r == sub - 1, next_row, xp[n - sub:])], axis=0)
    return xm * w[0:1] + x * w[1:2] + xp * w[2:3] + bias


def _halo_specs(width, col, row_off, n_rows, rows=ROW_TILE, samples=1):
    per = rows // BF16_SUBLANES
    last = n_rows // BF16_SUBLANES - 1
    return [
        pl.BlockSpec((samples, rows, width), lambda bi, i: (bi, i + row_off, col)),
        pl.BlockSpec((samples, BF16_SUBLANES, width),
                     lambda bi, i: (bi, jnp.maximum((i + row_off) * per - 1, 0), col)),
        pl.BlockSpec((samples, BF16_SUBLANES, width),
                     lambda bi, i: (bi, jnp.minimum((i + row_off + 1) * per, last), col)),
    ]


def _halo_rows(prev_ref, next_ref, is_first, is_last, b):
    prev_row = prev_ref[b, BF16_SUBLANES - 1:BF16_SUBLANES, :].astype(F32)
    next_row = next_ref[b, 0:1, :].astype(F32)
    prev_row = jnp.where(is_first, 0.0, prev_row)
    next_row = jnp.where(is_last, 0.0, next_row)
    return prev_row, next_row


HY_PRE_SAMPLES = 2


def _hypre_kernel(z_ref, zp_ref, zn_ref, sh_ref, w_ref, b_ref, db_ref, u_ref, ud_ref, x0_ref):
    i = pl.program_id(1)
    for b in range(z_ref.shape[0]):
        prev_row, next_row = _halo_rows(zp_ref, zn_ref, i == 0, i == pl.num_programs(1) - 1, b)
        zc = _conv3(z_ref[b], sh_ref, prev_row, next_row, w_ref[...], b_ref[...])
        x0 = zc[:, :HY_WIDTH]
        x1 = zc[:, HY_WIDTH:2 * HY_WIDTH]
        v = zc[:, 2 * HY_WIDTH:]
        u = v * x1
        u_ref[b] = u.astype(BF16)
        ud_ref[b] = (u * db_ref[...]).astype(BF16)
        x0_ref[b] = x0.astype(BF16)


def _hyena_pre(p_rest, col, row_off, n_rows, conv_w, conv_b, d_bias):
    b, s, _ = p_rest.shape
    width = 3 * HY_WIDTH
    nb = HY_PRE_SAMPLES if b % HY_PRE_SAMPLES == 0 else 1
    out = jax.ShapeDtypeStruct((b, n_rows, HY_WIDTH), BF16)
    ospec = pl.BlockSpec((nb, ROW_TILE, HY_WIDTH), lambda bi, i: (bi, i, 0))
    return pl.pallas_call(
        _hypre_kernel,
        grid=(b // nb, n_rows // ROW_TILE),
        in_specs=_halo_specs(width, col, row_off, s, samples=nb) + [
            pl.BlockSpec((2, ROW_TILE, ROW_TILE), lambda bi, i: (0, 0, 0)),
            pl.BlockSpec((3, width), lambda bi, i: (0, 0)),
            pl.BlockSpec((1, width), lambda bi, i: (0, 0)),
            pl.BlockSpec((1, HY_WIDTH), lambda bi, i: (0, 0))],
        out_specs=[ospec, ospec, ospec],
        out_shape=[out, out, out],
        compiler_params=_cparams(2, 32),
        name="hyena_pre",
    )(p_rest, p_rest, p_rest, _shift_matrices(ROW_TILE), conv_w, conv_b.reshape(1, width),
      d_bias.reshape(1, HY_WIDTH))


def _hyfilt_kernel(z_ref, t_ref, dl_ref, w1_ref, b1_ref, wi_ref, bi_ref, fr_ref, wl_ref, o_ref):
    fr = fr_ref[...]
    h = jnp.sin(fr * (_dot3(z_ref[...], w1_ref[...]) + b1_ref[...]))
    for j in range(HY_INNER):
        h = jnp.sin(fr * (_dot3(h, wi_ref[j]) + bi_ref[j]))
    h = _dot3(h, wl_ref[...])
    decay = jnp.exp(-t_ref[...] * dl_ref[...])
    hf = h[:, :HY_WIDTH] * decay
    hb = h[:, HY_WIDTH:] * decay
    o_ref[...] = jnp.concatenate([hf + hb, hf - hb], axis=1)


def _pad2(a, rows, cols):
    return jnp.pad(a, ((0, rows - a.shape[0]), (0, cols - a.shape[1])))


def _hyena_filter_sums(n, w1, b1, wi, bi, freq, w_last):
    t = jnp.linspace(0.0, 1.0, n, dtype=F32)[:, None]
    w = 2.0 * math.pi * jnp.arange(n, dtype=F32)[:, None] / n
    f = jnp.linspace(1e-4, HY_BANDS - 1, HY_BANDS, dtype=F32)[None, :]
    z = jnp.concatenate([t, jnp.cos(f * w), -jnp.sin(f * w)], axis=-1)
    max_decay = math.log(HY_TARGET) / HY_FAST_DECAY
    min_decay = math.log(HY_TARGET) / HY_SLOW_DECAY
    deltas = jnp.abs(jnp.linspace(min_decay, max_decay, HY_WIDTH, dtype=F32))[None, :]
    zp = _pad2(z, n, LANE)
    w1p = _pad2(w1, LANE, LANE)
    b1p = _pad2(b1[None, :], 1, LANE)
    wip = jnp.stack([_pad2(wi[j], LANE, LANE) for j in range(HY_INNER)])
    bip = jnp.stack([_pad2(bi[j][None, :], 1, LANE) for j in range(HY_INNER)])
    frp = _pad2(freq[None, :], 1, LANE)
    wlp = _pad2(w_last, LANE, 2 * HY_WIDTH)
    tr = min(n, ROW_TILE)
    full = lambda shape: pl.BlockSpec(shape, lambda i: (0,) * len(shape))
    return pl.pallas_call(
        _hyfilt_kernel,
        grid=(n // tr,),
        in_specs=[pl.BlockSpec((tr, LANE), lambda i: (i, 0)),
                  pl.BlockSpec((tr, 1), lambda i: (i, 0)),
                  full((1, HY_WIDTH)), full((LANE, LANE)), full((1, LANE)),
                  full((HY_INNER, LANE, LANE)), full((HY_INNER, 1, LANE)), full((1, LANE)),
                  full((LANE, 2 * HY_WIDTH))],
        out_specs=pl.BlockSpec((tr, 2 * HY_WIDTH), lambda i: (i, 0)),
        out_shape=jax.ShapeDtypeStruct((n, 2 * HY_WIDTH), F32),
        compiler_params=_cparams(1, 32),
        name="hyena_filter",
    )(zp, t, deltas, w1p, b1p, wip, bip, frp, wlp)


def _dft_tables(n):
    assert n % DFT_T_LO == 0
    f = jnp.arange(n, dtype=jnp.int32)[:, None]
    t_hi = jnp.arange(n // DFT_T_LO, dtype=jnp.int32)[None, :] * DFT_T_LO
    t_lo = jnp.arange(DFT_T_LO, dtype=jnp.int32)[None, :]
    a = ((f * t_hi) % (2 * n)).astype(F32) * (math.pi / n)
    b = ((f * t_lo) % (2 * n)).astype(F32) * (math.pi / n)
    ca, sa = jnp.cos(a)[:, :, None], jnp.sin(a)[:, :, None]
    cb, sb = jnp.cos(b)[:, None, :], jnp.sin(b)[:, None, :]
    cos_t = (ca * cb - sa * sb).reshape(n, n)
    sin_t = (sa * cb + ca * sb).reshape(n, n)
    t = jnp.arange(n, dtype=jnp.int32)[None, :]
    nyq = jnp.where(t % 2 == 0, 1.0, -1.0).astype(F32)
    return jnp.stack([cos_t, jnp.where(f == 0, nyq, sin_t)])


def _hyfwd_kernel(u_ref, f_ref, co_ref, o_ref):
    u = u_ref[0]
    ure = _dot(f_ref[0], u)
    uim = _dot(f_ref[1], u)
    o_ref[0, 0] = (ure * co_ref[0] - uim * co_ref[1]).astype(BF16)
    o_ref[0, 1] = (ure * co_ref[2] + uim * co_ref[3]).astype(BF16)


def _hyinv_kernel(y_ref, ft_ref, ud_ref, x0_ref, o_ref):
    y = _dot(ft_ref[0], y_ref[0, 0]) + _dot(ft_ref[1], y_ref[0, 1])
    o_ref[0] = ((y + ud_ref[0].astype(F32)) * x0_ref[0].astype(F32)).astype(BF16)


def _hyena_conv(u, ud, x0, tables, coef):
    b, n, c = u.shape
    tf = min(n, DFT_ROWS)
    f_bf = tables.astype(BF16)
    ft_bf = jnp.swapaxes(tables, 1, 2).astype(BF16)
    spec = pl.pallas_call(
        _hyfwd_kernel,
        grid=(n // tf, b),
        in_specs=[pl.BlockSpec((1, n, c), lambda j, bi: (bi, 0, 0)),
                  pl.BlockSpec((2, tf, n), lambda j, bi: (0, j, 0)),
                  pl.BlockSpec((4, tf, c), lambda j, bi: (0, j, 0))],
        out_specs=pl.BlockSpec((1, 2, tf, c), lambda j, bi: (bi, 0, j, 0)),
        out_shape=jax.ShapeDtypeStruct((b, 2, n, c), BF16),
        compiler_params=_cparams(2, 40),
        name="hyena_dft",
    )(u, f_bf, coef)
    return pl.pallas_call(
        _hyinv_kernel,
        grid=(b, n // tf),
        in_specs=[pl.BlockSpec((1, 2, n, c), lambda bi, j: (bi, 0, 0, 0)),
                  pl.BlockSpec((2, tf, n), lambda bi, j: (0, j, 0)),
                  pl.BlockSpec((1, tf, c), lambda bi, j: (bi, j, 0)),
                  pl.BlockSpec((1, tf, c), lambda bi, j: (bi, j, 0))],
        out_specs=pl.BlockSpec((1, tf, c), lambda bi, j: (bi, j, 0)),
        out_shape=jax.ShapeDtypeStruct((b, n, c), BF16),
        compiler_params=_cparams(2, 40),
        name="hyena_idft",
    )(spec, ft_bf, ud, x0)


def _hyena_coef(tables, hsum_hdiff):
    n = tables.shape[1]
    c = HY_WIDTH
    r = _matmul_f32(tables.reshape(2 * n, n), hsum_hdiff, min(n, ROW_TILE), "hyena_kernel_dft")
    k_re = r[:n, :c]
    k_im = r[n:, c:]
    k_nyq = r[n:n + 1, :c]
    first = (jnp.arange(n) == 0)[:, None]
    scale = jnp.where(first, 1.0 / (2 * n), 2.0 / (2 * n)).astype(F32)
    zero = jnp.zeros_like(k_im)
    return jnp.stack([k_re * scale,
                      jnp.where(first, zero, k_im * scale),
                      jnp.where(first, zero, k_im * scale),
                      jnp.where(first, k_nyq, k_re) * scale])


def _merge_kernel(*refs, n_ctx_tiles, row_off, has_ctx):
    if has_ctx:
        (of_ref, ob_ref, zg_ref, att_ref, cx_ref, cc_ref, gate_ref, x_ref, m_ref, ghg_ref, gpost_ref,
         gffn_ref, woa_ref, wob_ref, woc_ref, wout_ref, o_ref, h_ref) = refs
    else:
        (of_ref, ob_ref, zg_ref, att_ref, cx_ref, gate_ref, x_ref, m_ref, ghg_ref, gpost_ref,
         gffn_ref, woa_ref, wob_ref, woc_ref, wout_ref, o_ref, h_ref) = refs
    nb, rows, d = x_ref.shape
    flat = lambda ref: ref[...].reshape(nb * rows, ref.shape[-1])
    o = flat(of_ref).astype(F32) + flat(ob_ref).astype(F32)
    ghg = ghg_ref[...]
    a = jnp.concatenate([_rms(o[:, h * HG_DIM:(h + 1) * HG_DIM], ghg) for h in range(HG_HEADS)], axis=1)
    zg = flat(zg_ref)
    a = a * (zg * _sigmoid(zg))
    c = flat(cx_ref)
    if has_ctx:
        c = jnp.where(pl.program_id(1) + row_off < n_ctx_tiles, flat(cc_ref), c)
    ya = _dot(a.astype(BF16), woa_ref[...])
    yb = _dot(flat(att_ref), wob_ref[...])
    yc = _dot(c, woc_ref[...])
    gates = flat(gate_ref)
    m = (_sigmoid(gates[:, 0:d].astype(F32)) * ya
         + _sigmoid(gates[:, d:2 * d].astype(F32)) * yb
         + _sigmoid(gates[:, 2 * d:3 * d].astype(F32)) * yc)
    y = _rms(_dot(m.astype(BF16), wout_ref[...]), gpost_ref[...])
    for b in range(nb):
        x_new = x_ref[b] + m_ref[b, 2:3, :] * y[b * rows:(b + 1) * rows]
        o_ref[b] = x_new
        h_ref[b] = (_rms(x_new, gffn_ref[...]) * (1.0 + m_ref[b, 4:5, :]) + m_ref[b, 3:4, :]).astype(BF16)


def _merge(o_f, o_b, p_hg, att, c_x, c_c, p_rest, xs, mods, g_hg, g_post, g_ffn, w_oa, w_ob, w_oc, w_out,
           row_off, n_ctx_rows):
    b, s, d = xs.shape
    nct = n_ctx_rows // ROW_TILE
    n_tiles = s // ROW_TILE - row_off
    has_ctx = c_c is not None
    nb = MERGE_SAMPLES if b % MERGE_SAMPLES == 0 else 1
    ctx_blk = b // nb

    def stream(width, col=0):
        return pl.BlockSpec((nb, ROW_TILE, width), lambda bi, i: (bi, i + row_off, col))

    def full(shape):
        return pl.BlockSpec(shape, lambda bi, i: (0,) * len(shape))

    in_specs = [stream(HG_WIDTH), stream(HG_WIDTH), stream(HG_WIDTH, 4), stream(AT_WIDTH),
                pl.BlockSpec((nb, ROW_TILE, HY_WIDTH),
                             lambda bi, i: (bi, jnp.maximum(i + row_off - nct, 0), 0))]
    args = [o_f, o_b, p_hg, att, c_x]
    if has_ctx:
        in_specs.append(pl.BlockSpec((nb, ROW_TILE, HY_WIDTH),
                                     lambda bi, i: (bi, jnp.minimum(i + row_off, nct - 1), 0)))
        args.append(c_c)
    in_specs += [stream(3 * d), stream(d),
                 pl.BlockSpec((nb, 6, d), lambda bi, i: (jnp.where(i + row_off < nct, ctx_blk, bi), 0, 0)),
                 full((1, HG_DIM)), full((1, d)), full((1, d)),
                 full((HG_WIDTH, d)), full((AT_WIDTH, d)), full((HY_WIDTH, d)), full((d, d))]
    args += [p_rest, xs, mods, g_hg.reshape(1, HG_DIM), g_post.reshape(1, d), g_ffn.reshape(1, d),
             w_oa, w_ob, w_oc, w_out]
    ospec = pl.BlockSpec((nb, ROW_TILE, d), lambda bi, i: (bi, i, 0))
    return pl.pallas_call(
        functools.partial(_merge_kernel, n_ctx_tiles=nct, row_off=row_off, has_ctx=has_ctx),
        grid=(b // nb, n_tiles),
        in_specs=in_specs,
        out_specs=[ospec, ospec],
        out_shape=[jax.ShapeDtypeStruct((b, n_tiles * ROW_TILE, d), F32),
                   jax.ShapeDtypeStruct((b, n_tiles * ROW_TILE, d), BF16)],
        compiler_params=_cparams(2, 48),
        name="merge",
    )(*args)


FFN_COLS = 256
FFN_DOWN_GROUPS = 2


def _ffn_kernel(*refs, first_tiles, last_tiles, has_next):
    if has_next:
        (h_ref, hp_ref, hn_ref, wu_ref, w_ref, b_ref, x_ref, m_ref, g_ref, wd_ref, mn_ref, gn_ref,
         o_ref, hx_ref, act_ref) = refs
    else:
        h_ref, hp_ref, hn_ref, wu_ref, w_ref, b_ref, x_ref, m_ref, g_ref, wd_ref, o_ref, act_ref = refs
    i = pl.program_id(1)
    is_first = functools.reduce(jnp.logical_or, [i == t for t in first_tiles])
    is_last = functools.reduce(jnp.logical_or, [i == t for t in last_tiles])
    d_ff = wd_ref.shape[0]
    nb, rows, _ = h_ref.shape
    halo = BF16_SUBLANES
    ext = rows + 2 * halo
    pieces = []
    for b in range(nb):
        pieces += [jnp.where(is_first, jnp.zeros_like(hp_ref[b]), hp_ref[b]), h_ref[b],
                   jnp.where(is_last, jnp.zeros_like(hn_ref[b]), hn_ref[b])]
    h_ext = jnp.concatenate(pieces, axis=0)

    def up(j):
        return [_dot(h_ext, wu_ref[:, base + j * FFN_COLS:base + (j + 1) * FFN_COLS]) for base in (0, d_ff)]

    def conv(u, cols):
        w = w_ref[:, cols]
        full = pltpu.roll(u, 1, axis=0) * w[0:1] + u * w[1:2] + pltpu.roll(u, nb * ext - 1, axis=0) * w[2:3]
        kept = [full[b * ext + halo:b * ext + halo + rows] for b in range(nb)]
        return (kept[0] if nb == 1 else jnp.concatenate(kept, axis=0)) + b_ref[:, cols]

    n_chunks = d_ff // FFN_COLS
    per_group = -(-n_chunks // FFN_DOWN_GROUPS)
    acc = None
    u_next = up(0)
    for j in range(n_chunks):
        u = u_next
        if j + 1 < n_chunks:
            u_next = up(j + 1)
        a = conv(u[0], slice(j * FFN_COLS, (j + 1) * FFN_COLS))
        g = conv(u[1], slice(d_ff + j * FFN_COLS, d_ff + (j + 1) * FFN_COLS))
        act_ref[:, j * FFN_COLS:(j + 1) * FFN_COLS] = (a * _sigmoid(a) * g).astype(BF16)
        if (j + 1) % per_group == 0 or j + 1 == n_chunks:
            lo = (j // per_group) * per_group * FFN_COLS
            part = _dot(act_ref[:, lo:(j + 1) * FFN_COLS], wd_ref[lo:(j + 1) * FFN_COLS, :])
            acc = part if acc is None else acc + part
    y = _rms(acc, g_ref[...])
    for b in range(nb):
        x_new = x_ref[b] + m_ref[b, 5:6, :] * y[b * rows:(b + 1) * rows]
        o_ref[b] = x_new
        if has_next:
            hx_ref[b] = (_rms(x_new, gn_ref[...]) * (1.0 + mn_ref[b, 1:2, :])
                         + mn_ref[b, 0:1, :]).astype(BF16)


def _ffn(h, xs, mods, w_up, conv_w, conv_b, g_post, w_down, n_ctx_rows, mods_next=None, g_next=None):
    b, s, d = xs.shape
    d_ff = w_down.shape[0]
    rows = FFN_ROWS if (n_ctx_rows % FFN_ROWS == 0 and s % FFN_ROWS == 0) else ROW_TILE
    nt = s // rows
    nct = n_ctx_rows // rows
    first_tiles = tuple(sorted({0, nct}))
    last_tiles = tuple(sorted({nct - 1, nt - 1} - {-1}))
    nb = FFN_ROWS // rows if b % (FFN_ROWS // rows) == 0 and FFN_ROWS // rows <= MERGE_SAMPLES else 1
    ctx_blk = b // nb
    has_next = mods_next is not None
    full = lambda shape: pl.BlockSpec(shape, lambda bi, i: (0,) * len(shape))
    mspec = pl.BlockSpec((nb, 6, d), lambda bi, i: (jnp.where(i < nct, ctx_blk, bi), 0, 0))
    ospec = pl.BlockSpec((nb, rows, d), lambda bi, i: (bi, i, 0))
    resident = lambda shape: pl.BlockSpec(shape, lambda bi, i: (0,) * len(shape), pipeline_mode=pl.Buffered(1))
    in_specs = _halo_specs(d, 0, 0, s, rows, nb) + [
        resident((d, 2 * d_ff)), full((3, 2 * d_ff)), full((1, 2 * d_ff)),
        ospec, mspec, full((1, d)), resident((d_ff, d))]
    args = [h, h, h, w_up, conv_w, conv_b.reshape(1, 2 * d_ff), xs, mods, g_post.reshape(1, d), w_down]
    out_specs = [ospec]
    out_shape = [jax.ShapeDtypeStruct((b, s, d), F32)]
    if has_next:
        in_specs += [mspec, full((1, d))]
        args += [mods_next, g_next.reshape(1, d)]
        out_specs.append(ospec)
        out_shape.append(jax.ShapeDtypeStruct((b, s, d), BF16))
    return pl.pallas_call(
        functools.partial(_ffn_kernel, first_tiles=first_tiles, last_tiles=last_tiles, has_next=has_next),
        grid=(b // nb, nt),
        in_specs=in_specs,
        out_specs=out_specs,
        out_shape=out_shape,
        scratch_shapes=[pltpu.VMEM((nb * rows, d_ff), BF16)],
        compiler_params=_cparams(2, 56),
        name="ffn",
    )(*args)


def _deinterleave():
    return np.concatenate([np.arange(0, AT_DIM, 2), np.arange(1, AT_DIM, 2)])


def _q_head_order():
    return [h for j in range(AT_GROUP) for h in (j, AT_GROUP + j)]


def _largest_tile(n, cap):
    best = LANE
    for t in range(LANE, cap + 1, LANE):
        if n % t == 0:
            best = t
    return best


def kernel(x, c, ctx, c_ctx, w_ada, b_ada, g_pre_mix, g_post_mix, g_pre_ffn, g_post_ffn, w_in, hg_lower_bounds, hg_norm, q_norm, k_norm, hy_conv_w, hy_conv_b, hy_w1, hy_b1, hy_wi, hy_bi, hy_freq, hy_w_last, hy_bias, w_oa, w_ob, w_oc, w_out, w_up, ffn_conv_w, ffn_conv_b, w_down):
    bsz, n_lat, d = x.shape
    n_ctx = ctx.shape[1]
    depth = w_ada.shape[0]
    d_ff = w_down.shape[1]
    assert AT_KV_HEADS == 2 and AT_GROUP * LANE == AT_WIDTH and AT_KV_WIDTH == LANE
    assert n_ctx % ROW_TILE == 0 and n_lat % ROW_TILE == 0 and n_lat % GRID_W == 0
    assert (bsz * (n_ctx + n_lat)) % MM_ROWS == 0 and (bsz * n_lat) % MM_ROWS == 0

    lbp = jax.nn.softmax(hg_lower_bounds.astype(F32), axis=0)
    lower = jnp.cumsum(lbp, axis=0) - lbp[0]

    rp = -(-(bsz + MERGE_SAMPLES) // 8) * 8
    src = jnp.concatenate([c, jnp.tile(c_ctx[None, :], (MERGE_SAMPLES, 1)),
                           jnp.zeros((rp - bsz - MERGE_SAMPLES, d), F32)], axis=0)
    mods_all = _ada(src, w_ada, b_ada).reshape(depth, rp, 6, d)

    o_q = 5 * HG_WIDTH
    o_k = o_q + AT_WIDTH
    o_v = o_k + AT_KV_WIDTH
    o_hy = o_v + AT_KV_WIDTH
    o_gate = o_hy + 3 * HY_WIDTH
    deint = _deinterleave()
    q_cols = np.concatenate([o_q + h * AT_DIM + deint for h in _q_head_order()])
    k_cols = np.concatenate([o_k + g * AT_DIM + deint for g in range(AT_KV_HEADS)])
    qk_cols = np.concatenate([q_cols, k_cols])
    col_hy = (3 * d) // (3 * HY_WIDTH)
    col_q = (3 * d + 3 * HY_WIDTH) // AT_WIDTH
    col_k = (3 * d + 3 * HY_WIDTH + AT_WIDTH) // AT_KV_WIDTH
    col_v = col_k + 1
    assert (3 * d) % (3 * HY_WIDTH) == 0 and (3 * d + 3 * HY_WIDTH) % AT_WIDTH == 0
    ob_rows = np.concatenate([np.arange(h * AT_DIM, (h + 1) * AT_DIM) for h in _q_head_order()])

    rope_tabs = _rope_tables(n_ctx, n_lat)
    dft_lat = _dft_tables(n_lat)
    dft_ctx = _dft_tables(n_ctx)
    nct = n_ctx // ROW_TILE

    s_all = n_ctx + n_lat
    xs, h = _join_modulate(ctx, x, mods_all[0], g_pre_mix[0], 0, 1)
    for l in range(depth):
        need_ctx = l < depth - 1
        mods = mods_all[l]
        w_hg = w_in[l][:, :5 * HG_WIDTH].astype(BF16)
        w_rest = jnp.concatenate([w_in[l][:, o_gate:o_gate + 3 * d], w_in[l][:, o_hy:o_hy + 3 * HY_WIDTH],
                                  w_in[l][:, qk_cols], w_in[l][:, o_v:o_v + AT_KV_WIDTH]], axis=1).astype(BF16)

        h = h.reshape(bsz * s_all, d)
        p_hg = _matmul(h, w_hg, F32, _largest_tile(5 * HG_WIDTH, 1280), "proj_hgrn").reshape(bsz, s_all, -1)
        p_rest = _matmul(h, w_rest, BF16, _largest_tile(w_rest.shape[1], 1792), "proj_rest").reshape(bsz, s_all, -1)

        o_f, o_b = _hgrn(p_hg, lower[l], n_ctx)

        gq = jnp.tile(q_norm[l][deint], AT_HEADS)[None, :]
        gk = jnp.tile(k_norm[l][deint], AT_KV_HEADS)[None, :]
        row_off = 0 if need_ctx else nct
        att = _attention(p_rest, col_q, col_k, col_v, rope_tabs, gq, gk, row_off, n_ctx)

        filt_args = (hy_w1[l], hy_b1[l], hy_wi[l], hy_bi[l], hy_freq[l], hy_w_last[l])
        coef = _hyena_coef(dft_lat, _hyena_filter_sums(n_lat, *filt_args))
        c_x = _hyena_conv(*_hyena_pre(p_rest, col_hy, nct, n_lat, hy_conv_w[l], hy_conv_b[l], hy_bias[l]),
                          dft_lat, coef)
        c_c = None
        if need_ctx:
            coef_c = _hyena_coef(dft_ctx, _hyena_filter_sums(n_ctx, *filt_args))
            c_c = _hyena_conv(*_hyena_pre(p_rest, col_hy, 0, n_ctx, hy_conv_w[l], hy_conv_b[l], hy_bias[l]),
                              dft_ctx, coef_c)

        xs, h2 = _merge(o_f, o_b, p_hg, att, c_x, c_c, p_rest, xs, mods, hg_norm[l], g_post_mix[l],
                        g_pre_ffn[l], w_oa[l].astype(BF16), w_ob[l][ob_rows].astype(BF16),
                        w_oc[l].astype(BF16), w_out[l].astype(BF16), row_off, n_ctx)
        n_ctx_now = n_ctx if need_ctx else 0
        ffn_args = (h2, xs, mods, w_up[l].astype(BF16), ffn_conv_w[l], ffn_conv_b[l], g_post_ffn[l],
                    w_down[l].astype(BF16), n_ctx_now)
        if need_ctx:
            xs, h = _ffn(*ffn_args, mods_all[l + 1], g_pre_mix[l + 1])
        else:
            xs, = _ffn(*ffn_args)
    return xs
```

```python
import functools
import math

import jax
import jax.numpy as jnp
import numpy as np
from jax import lax
from jax.experimental import pallas as pl
from jax.experimental.pallas import tpu as pltpu

F32 = jnp.float32
BF16 = jnp.bfloat16

NORM_EPS = 1e-6
GRID_W = 64
HG_HEADS = 4
HG_DIM = 128
HG_WIDTH = HG_HEADS * HG_DIM
HG_EXP_CLIP = 30.0
AT_HEADS = 8
AT_KV_HEADS = 2
AT_DIM = 64
AT_GROUP = AT_HEADS // AT_KV_HEADS
AT_WIDTH = AT_HEADS * AT_DIM
AT_KV_WIDTH = AT_KV_HEADS * AT_DIM
ROPE_THETA = 10000.0
HY_WIDTH = 512
HY_EMB_DIM = 33
HY_BANDS = (HY_EMB_DIM - 1) // 2
HY_FILTER_WIDTH = 64
HY_INNER = 2
HY_FAST_DECAY = 0.3
HY_SLOW_DECAY = 1.5
HY_TARGET = 1e-2

LANE = 128
BF16_SUBLANES = 16
ROW_TILE = 256
MERGE_SAMPLES = 2
HG_CHUNK = 128
HG_BLOCK = 256
HG_LEVELS = tuple(HG_CHUNK >> (j + 1) for j in range(int(math.log2(HG_CHUNK))))
MM_ROWS = 1024
DFT_ROWS = 1024
DFT_T_LO = 64
AT_LOOKAHEAD = 2
FFN_ROWS = 512
VMEM_CAP = 56 * 1024 * 1024


def _cparams(n_axes, vmem_mb):
    return pltpu.CompilerParams(
        dimension_semantics=("arbitrary",) * n_axes,
        vmem_limit_bytes=min(int(vmem_mb) * 1024 * 1024, VMEM_CAP))


def _dot(a, b):
    return jnp.dot(a, b, preferred_element_type=F32)


def _dot_nt(a, b):
    return lax.dot_general(a, b, (((1,), (1,)), ((), ())), preferred_element_type=F32)


def _split_bf16(a):
    hi = a.astype(BF16)
    lo = (a - hi.astype(F32)).astype(BF16)
    return hi, lo


def _dot3(a, b):
    ah, al = _split_bf16(a)
    bh, bl = _split_bf16(b)
    return _dot(ah, bh) + (_dot(ah, bl) + _dot(al, bh))


def _rms(x, g):
    return x * lax.rsqrt(jnp.mean(x * x, axis=-1, keepdims=True) + NORM_EPS) * g


def _sigmoid(x):
    return 0.5 * jnp.tanh(0.5 * x) + 0.5


def _ada_kernel(src_ref, w_ref, b_ref, o_ref):
    s = src_ref[...]
    s = s * _sigmoid(s)
    o_ref[0] = _dot3(s, w_ref[0]) + b_ref[0]


def _ada(src, w_ada, b_ada):
    depth, d, d6 = w_ada.shape
    rp = src.shape[0]
    tn = d
    return pl.pallas_call(
        _ada_kernel,
        grid=(depth, d6 // tn),
        in_specs=[pl.BlockSpec((rp, d), lambda l, j: (0, 0)),
                  pl.BlockSpec((1, d, tn), lambda l, j: (l, 0, j)),
                  pl.BlockSpec((1, 1, tn), lambda l, j: (l, 0, j))],
        out_specs=pl.BlockSpec((1, rp, tn), lambda l, j: (l, 0, j)),
        out_shape=jax.ShapeDtypeStruct((depth, rp, d6), F32),
        compiler_params=_cparams(2, 32),
        name="ada",
    )(src, w_ada, b_ada.reshape(depth, 1, d6))


def _mod_kernel(c_ref, x_ref, m_ref, g_ref, xs_ref, o_ref, *, k_shift, k_scale, n_ctx_tiles):
    for b in range(x_ref.shape[0]):
        x = jnp.where(pl.program_id(1) < n_ctx_tiles, c_ref[b], x_ref[b])
        xs_ref[b] = x
        shift = m_ref[b, k_shift:k_shift + 1, :]
        scale = m_ref[b, k_scale:k_scale + 1, :]
        o_ref[b] = (_rms(x, g_ref[...]) * (1.0 + scale) + shift).astype(o_ref.dtype)


def _join_modulate(ctx, x, mods, g, k_shift, k_scale):
    b, n_lat, d = x.shape
    n_ctx_tiles = ctx.shape[1] // ROW_TILE
    nt = n_ctx_tiles + n_lat // ROW_TILE
    nb = MERGE_SAMPLES if b % MERGE_SAMPLES == 0 else 1
    ctx_blk = b // nb
    ospec = pl.BlockSpec((nb, ROW_TILE, d), lambda bi, i: (bi, i, 0))
    return pl.pallas_call(
        functools.partial(_mod_kernel, k_shift=k_shift, k_scale=k_scale, n_ctx_tiles=n_ctx_tiles),
        grid=(b // nb, nt),
        in_specs=[pl.BlockSpec((nb, ROW_TILE, d), lambda bi, i: (bi, jnp.minimum(i, n_ctx_tiles - 1), 0)),
                  pl.BlockSpec((nb, ROW_TILE, d), lambda bi, i: (bi, jnp.maximum(i - n_ctx_tiles, 0), 0)),
                  pl.BlockSpec((nb, 6, d), lambda bi, i: (jnp.where(i < n_ctx_tiles, ctx_blk, bi), 0, 0)),
                  pl.BlockSpec((1, d), lambda bi, i: (0, 0))],
        out_specs=[ospec, ospec],
        out_shape=[jax.ShapeDtypeStruct((b, nt * ROW_TILE, d), F32),
                   jax.ShapeDtypeStruct((b, nt * ROW_TILE, d), BF16)],
        compiler_params=_cparams(2, 16),
        name="modulate",
    )(ctx, x, mods, g.reshape(1, d))


def _mm_kernel(a_ref, b_ref, o_ref):
    o_ref[...] = _dot(a_ref[...], b_ref[...]).astype(o_ref.dtype)


def _matmul(a, w, out_dtype, tn, name):
    m, k = a.shape
    n = w.shape[1]
    tm = MM_ROWS
    assert m % tm == 0 and n % tn == 0
    return pl.pallas_call(
        _mm_kernel,
        grid=(m // tm, n // tn),
        in_specs=[pl.BlockSpec((tm, k), lambda i, j: (i, 0)),
                  pl.BlockSpec((k, tn), lambda i, j: (0, j))],
        out_specs=pl.BlockSpec((tm, tn), lambda i, j: (i, j)),
        out_shape=jax.ShapeDtypeStruct((m, n), out_dtype),
        compiler_params=_cparams(2, 48),
        name=name,
    )(a, w)


def _mm3_kernel(a_ref, b_ref, o_ref):
    o_ref[...] = _dot3(a_ref[...], b_ref[...])


def _matmul_f32(a, w, tm, name):
    m, k = a.shape
    n = w.shape[1]
    return pl.pallas_call(
        _mm3_kernel,
        grid=(m // tm,),
        in_specs=[pl.BlockSpec((tm, k), lambda i: (i, 0)),
                  pl.BlockSpec((k, n), lambda i: (0, 0))],
        out_specs=pl.BlockSpec((tm, n), lambda i: (i, 0)),
        out_shape=jax.ShapeDtypeStruct((m, n), F32),
        compiler_params=_cparams(1, 48),
        name=name,
    )(a, w)


def _hg_scan_matrix(reverse):
    t_n = HG_CHUNK
    t = np.arange(t_n)[:, None]
    u = np.arange(t_n)[None, :]
    rows = [(u >= t) if reverse else (u <= t)]
    for w in HG_LEVELS:
        base = (t // (2 * w)) * (2 * w)
        mid = base + w
        upper = (t - base) >= w
        if reverse:
            m = np.where(upper, (u >= mid) & (u < t), (u >= t) & (u < mid))
        else:
            m = np.where(upper, (u >= mid) & (u <= t), (u > t) & (u < mid))
        rows.append(m)
    m = np.concatenate(rows, axis=0).astype(np.float32)
    return np.concatenate([m, m], axis=1)


def _hgrn_kernel(qf_ref, zf_ref, vf_ref, qb_ref, zb_ref, vb_ref, lb_ref, pf_ref, pb_ref,
                 of_ref, ob_ref, s_ref):
    @pl.when(pl.program_id(1) == 0)
    def _():
        s_ref[...] = jnp.zeros_like(s_ref)

    t_n = HG_CHUNK
    ti = lax.broadcasted_iota(jnp.int32, (t_n, t_n), 0)
    si = lax.broadcasted_iota(jnp.int32, (t_n, t_n), 1)
    tx = ti ^ si
    dirs = ((qf_ref, zf_ref, vf_ref, pf_ref, of_ref), (qb_ref, zb_ref, vb_ref, pb_ref, ob_ref))
    masks = []
    for reverse in (False, True):
        later = (ti < si) if reverse else (ti > si)
        masks.append([later & (tx >= w) & (tx < 2 * w) for w in HG_LEVELS])
    n_sub = qf_ref.shape[1] // t_n
    heads = range(HG_HEADS)
    units = [(d, c, h) for d in range(2) for c in range(n_sub) for h in heads]
    cols = lambda h: slice(h * HG_DIM, (h + 1) * HG_DIM)
    rows = lambda c: slice(c * t_n, (c + 1) * t_n)

    kk, kb, qb, cat, x, ex = {}, {}, {}, {}, {}, {}
    a = {u: jnp.zeros((t_n, t_n), F32) for u in units}

    def gates(d, c, h):
        z = dirs[d][1][0, rows(c), cols(h)]
        lb = lb_ref[d:d + 1, cols(h)]
        e = jnp.exp(-jnp.abs(z))
        log_num = jnp.log(jnp.where(z >= 0.0, 1.0 + lb * e, e + lb))
        log_clip = z + jnp.log(1.0 + lb * math.exp(HG_EXP_CLIP))
        lf = jnp.where(z < -HG_EXP_CLIP, log_clip, log_num) - jnp.log(1.0 + e)
        half = 0.5 * (1.0 - lb)
        kk[d, c, h] = half - half * jnp.tanh(0.5 * z)
        kb[d, c, h] = kk[d, c, h].astype(BF16)
        qb[d, c, h] = dirs[d][0][0, rows(c), cols(h)].astype(BF16)
        hi, lo = _split_bf16(lf)
        cat[d, c, h] = jnp.concatenate([hi, lo], axis=0)

    def exponents(d):
        rhs = jnp.concatenate([cat[d, c, h] for c in range(n_sub) for h in heads], axis=1)
        xd = _dot(dirs[d][3][...], rhs)
        ed = jnp.exp(xd[t_n:]).astype(BF16)
        for c in range(n_sub):
            for h in heads:
                x[d, c, h] = xd[0:t_n, cols(c * HG_HEADS + h)]
                ex[d, c, h] = ed[:, cols(c * HG_HEADS + h)]

    def level(d, c, j):
        for h in heads:
            u = (d, c, h)
            ew = ex[u][j * t_n:(j + 1) * t_n]
            pw = _dot_nt(qb[u] * ew, kb[u] * ew)
            a[u] = jnp.where(masks[d][j], pw, a[u])

    def finish(d, c, h):
        u = (d, c, h)
        q = dirs[d][0][0, rows(c), cols(h)]
        v = dirs[d][2][0, rows(c), cols(h)]
        g = x[u]
        g_last = g[0:1] if d == 1 else g[t_n - 1:t_n]
        st = s_ref[d * HG_HEADS + h]
        dqk = jnp.sum(q * kk[u], axis=1, keepdims=True)
        o = (_dot(a[u].astype(BF16), v.astype(BF16)) + dqk * v
             + _dot_nt((q * jnp.exp(g)).astype(BF16), st.astype(BF16)))
        kd = (kk[u] * jnp.exp(g_last - g)).astype(BF16)
        dirs[d][4][0, rows(c), cols(h)] = o.astype(BF16)
        s_ref[d * HG_HEADS + h] = st * jnp.exp(g_last) + _dot(v.T.astype(BF16), kd)

    for d in range(2):
        for c in range(n_sub):
            for h in heads:
                gates(d, c, h)
        exponents(d)
        for j in range(len(HG_LEVELS)):
            for c in range(n_sub):
                level(d, c, j)
        for c in (range(n_sub) if d == 0 else reversed(range(n_sub))):
            for h in heads:
                finish(d, c, h)


def _hgrn(p_hg, lb, n_ctx_rows):
    b, s, _ = p_hg.shape
    assert s % HG_BLOCK == 0 and n_ctx_rows % HG_BLOCK == 0
    nb = s // HG_BLOCK
    nc = n_ctx_rows // HG_BLOCK

    def bidx(n):
        return jnp.where(n < nc, nc - 1 - n, nb - 1 - (n - nc))

    blk = (1, HG_BLOCK, HG_WIDTH)
    pf = jnp.asarray(_hg_scan_matrix(False)).astype(BF16)
    pb = jnp.asarray(_hg_scan_matrix(True)).astype(BF16)
    pshape = pf.shape
    return pl.pallas_call(
        _hgrn_kernel,
        grid=(b, nb),
        in_specs=[pl.BlockSpec(blk, lambda bi, n: (bi, n, 0)),
                  pl.BlockSpec(blk, lambda bi, n: (bi, n, 1)),
                  pl.BlockSpec(blk, lambda bi, n: (bi, n, 3)),
                  pl.BlockSpec(blk, lambda bi, n: (bi, bidx(n), 0)),
                  pl.BlockSpec(blk, lambda bi, n: (bi, bidx(n), 2)),
                  pl.BlockSpec(blk, lambda bi, n: (bi, bidx(n), 3)),
                  pl.BlockSpec((2, HG_WIDTH), lambda bi, n: (0, 0)),
                  pl.BlockSpec(pshape, lambda bi, n: (0, 0)),
                  pl.BlockSpec(pshape, lambda bi, n: (0, 0))],
        out_specs=[pl.BlockSpec(blk, lambda bi, n: (bi, n, 0)),
                   pl.BlockSpec(blk, lambda bi, n: (bi, bidx(n), 0))],
        out_shape=[jax.ShapeDtypeStruct((b, s, HG_WIDTH), BF16)] * 2,
        scratch_shapes=[pltpu.VMEM((2 * HG_HEADS, HG_DIM, HG_DIM), F32)],
        compiler_params=_cparams(2, 40),
        name="hgrn",
    )(p_hg, p_hg, p_hg, p_hg, p_hg, p_hg, lb, pf, pb)


def _rope(x, cos, sin_signed, first_half):
    n = x.shape[-1]
    half = AT_DIM // 2
    partner = jnp.where(first_half, pltpu.roll(x, n - half, axis=1), pltpu.roll(x, half, axis=1))
    return x * cos + partner * sin_signed


def _head_norm(x, gain, group_mean):
    ms = _dot((x * x).astype(BF16), group_mean)
    return x * lax.rsqrt(ms + NORM_EPS) * gain


def _attn_kernel(q_ref, k_ref, v_ref, cq_ref, sq_ref, ck_ref, sk_ref, gq_ref, gk_ref, mq_ref, mk_ref,
                 o_ref, k_scr, v_scr, *, q_off, n_ctx_tiles, n_ctx_rows):
    i = pl.program_id(1)

    @pl.when(i == 0)
    def _():
        kr = k_ref[0].astype(F32)
        lane = lax.broadcasted_iota(jnp.int32, kr.shape, 1)
        kn = _head_norm(kr, gk_ref[...], mk_ref[...])
        k_scr[...] = _rope(kn, ck_ref[...], sk_ref[...], (lane % AT_DIM) < AT_DIM // 2).astype(BF16)
        v = v_ref[0]
        one = jnp.ones_like(v)
        v_scr[0] = jnp.where(lane < AT_DIM, v, one)
        v_scr[1] = jnp.where(lane < AT_DIM, one, v)

    qr = q_ref[0].astype(F32)
    lane = lax.broadcasted_iota(jnp.int32, qr.shape, 1)
    qn = _head_norm(qr, gq_ref[...], mq_ref[...])
    qn = _rope(qn, cq_ref[...], sq_ref[...], (lane % AT_DIM) < AT_DIM // 2)
    qn = (qn * (AT_DIM ** -0.5 * math.log2(math.e))).astype(BF16)
    lane_t = lax.broadcasted_iota(jnp.int32, (ROW_TILE, LANE), 1)
    kv0 = lane_t < AT_DIM

    def attend(n_keys):
        keys = k_scr[0:n_keys, :]
        heads = [(j, g) for j in range(AT_WIDTH // LANE) for g in range(AT_KV_HEADS)]

        def scores(j, g):
            qt = qn[:, j * LANE:(j + 1) * LANE]
            return _dot_nt(jnp.where(kv0 if g == 0 else ~kv0, qt, jnp.zeros_like(qt)), keys)

        pending = [scores(*heads[n]) for n in range(AT_LOOKAHEAD)]
        outs = {}
        for n, (j, g) in enumerate(heads):
            s = pending.pop(0)
            if n + AT_LOOKAHEAD < len(heads):
                pending.append(scores(*heads[n + AT_LOOKAHEAD]))
            p = jnp.exp2(s - jnp.max(s, axis=1, keepdims=True))
            outs[g] = _dot(p.astype(BF16), v_scr[g, 0:n_keys, :])
            if g == AT_KV_HEADS - 1:
                num = jnp.where(kv0, outs[0], outs[1])
                den = pltpu.roll(jnp.where(kv0, outs[1], outs[0]), AT_DIM, axis=1)
                o_ref[0, :, j * LANE:(j + 1) * LANE] = (num / den).astype(o_ref.dtype)

    n_all = k_scr.shape[0]
    if q_off < n_ctx_tiles:
        @pl.when(i + q_off < n_ctx_tiles)
        def _():
            attend(n_ctx_rows)

        @pl.when(i + q_off >= n_ctx_tiles)
        def _():
            attend(n_all)
    else:
        attend(n_all)


def _attention(p_rest, col_q, col_k, col_v, tabs, gq, gk, q_off, n_ctx_rows):
    b, s, _ = p_rest.shape
    nt = s // ROW_TILE
    cq, sq, ck, sk = tabs
    mq = jnp.asarray(np.kron(np.eye(AT_HEADS), np.full((AT_DIM, AT_DIM), 1.0 / AT_DIM)), BF16)
    mk = jnp.asarray(np.kron(np.eye(AT_KV_HEADS), np.full((AT_DIM, AT_DIM), 1.0 / AT_DIM)), BF16)
    kern = functools.partial(_attn_kernel, q_off=q_off, n_ctx_tiles=n_ctx_rows // ROW_TILE,
                             n_ctx_rows=n_ctx_rows)
    return pl.pallas_call(
        kern,
        grid=(b, nt - q_off),
        in_specs=[pl.BlockSpec((1, ROW_TILE, AT_WIDTH), lambda bi, i: (bi, i + q_off, col_q)),
                  pl.BlockSpec((1, s, AT_KV_WIDTH), lambda bi, i: (bi, 0, col_k)),
                  pl.BlockSpec((1, s, AT_KV_WIDTH), lambda bi, i: (bi, 0, col_v)),
                  pl.BlockSpec((ROW_TILE, AT_WIDTH), lambda bi, i: (i + q_off, 0)),
                  pl.BlockSpec((ROW_TILE, AT_WIDTH), lambda bi, i: (i + q_off, 0)),
                  pl.BlockSpec((s, AT_KV_WIDTH), lambda bi, i: (0, 0)),
                  pl.BlockSpec((s, AT_KV_WIDTH), lambda bi, i: (0, 0)),
                  pl.BlockSpec((1, AT_WIDTH), lambda bi, i: (0, 0)),
                  pl.BlockSpec((1, AT_KV_WIDTH), lambda bi, i: (0, 0)),
                  pl.BlockSpec((AT_WIDTH, AT_WIDTH), lambda bi, i: (0, 0)),
                  pl.BlockSpec((AT_KV_WIDTH, AT_KV_WIDTH), lambda bi, i: (0, 0))],
        out_specs=pl.BlockSpec((1, ROW_TILE, AT_WIDTH), lambda bi, i: (bi, i + q_off, 0)),
        out_shape=jax.ShapeDtypeStruct((b, s, AT_WIDTH), BF16),
        scratch_shapes=[pltpu.VMEM((s, AT_KV_WIDTH), BF16),
                        pltpu.VMEM((AT_KV_HEADS, s, AT_KV_WIDTH), BF16)],
        compiler_params=_cparams(2, 48),
        name="attention",
    )(p_rest, p_rest, p_rest, cq, sq, ck, sk, gq, gk, mq, mk)


def _rope_tables(n_ctx_rows, n_lat_rows):
    rows = n_lat_rows // GRID_W
    row = jnp.repeat(jnp.arange(rows), GRID_W).astype(F32)
    col = jnp.tile(jnp.arange(GRID_W), rows).astype(F32)
    n_freq = AT_DIM // 4
    inv = ROPE_THETA ** (-jnp.arange(n_freq, dtype=F32) / n_freq)
    ang = jnp.concatenate([row[:, None] * inv, col[:, None] * inv], axis=-1)
    cos = jnp.concatenate([jnp.cos(ang), jnp.cos(ang)], axis=-1)
    sin = jnp.concatenate([-jnp.sin(ang), jnp.sin(ang)], axis=-1)
    cos = jnp.concatenate([jnp.ones((n_ctx_rows, AT_DIM), F32), cos], axis=0)
    sin = jnp.concatenate([jnp.zeros((n_ctx_rows, AT_DIM), F32), sin], axis=0)
    return (jnp.tile(cos, (1, AT_HEADS)), jnp.tile(sin, (1, AT_HEADS)),
            jnp.tile(cos, (1, AT_KV_HEADS)), jnp.tile(sin, (1, AT_KV_HEADS)))


def _shift_matrices(n):
    i = np.arange(n)
    down = i[:, None] - 1 == i[None, :]
    up = i[:, None] + 1 == i[None, :]
    return jnp.asarray(np.stack([down, up]).astype(np.float32), BF16)


def _conv3(xb, shift_ref, prev_row, next_row, w, bias):
    n, c = xb.shape
    sub = 8
    r = lax.broadcasted_iota(jnp.int32, (sub, c), 0)
    x = xb.astype(F32)
    if shift_ref is None:
        xm = pltpu.roll(x, 1, axis=0)
        xp = pltpu.roll(x, n - 1, axis=0)
    else:
        xm = _dot(shift_ref[0], xb)
        xp = _dot(shift_ref[1], xb)
    xm = jnp.concatenate([jnp.where(r == 0, prev_row, xm[0:sub]), xm[sub:]], axis=0)
    xp = jnp.concatenate([xp[:n - sub], jnp.where(r == sub - 1, next_row, xp[n - sub:])], axis=0)
    return xm * w[0:1] + x * w[1:2] + xp * w[2:3] + bias


def _halo_specs(width, col, row_off, n_rows, rows=ROW_TILE, samples=1):
    per = rows // BF16_SUBLANES
    last = n_rows // BF16_SUBLANES - 1
    return [
        pl.BlockSpec((samples, rows, width), lambda bi, i: (bi, i + row_off, col)),
        pl.BlockSpec((samples, BF16_SUBLANES, width),
                     lambda bi, i: (bi, jnp.maximum((i + row_off) * per - 1, 0), col)),
        pl.BlockSpec((samples, BF16_SUBLANES, width),
                     lambda bi, i: (bi, jnp.minimum((i + row_off + 1) * per, last), col)),
    ]


def _halo_rows(prev_ref, next_ref, is_first, is_last, b):
    prev_row = prev_ref[b, BF16_SUBLANES - 1:BF16_SUBLANES, :].astype(F32)
    next_row = next_ref[b, 0:1, :].astype(F32)
    prev_row = jnp.where(is_first, 0.0, prev_row)
    next_row = jnp.where(is_last, 0.0, next_row)
    return prev_row, next_row


HY_PRE_SAMPLES = 4


def _hypre_kernel(z_ref, zp_ref, zn_ref, sh_ref, w_ref, b_ref, db_ref, u_ref, ud_ref, x0_ref):
    i = pl.program_id(1)
    for b in range(z_ref.shape[0]):
        prev_row, next_row = _halo_rows(zp_ref, zn_ref, i == 0, i == pl.num_programs(1) - 1, b)
        zc = _conv3(z_ref[b], sh_ref, prev_row, next_row, w_ref[...], b_ref[...])
        x0 = zc[:, :HY_WIDTH]
        x1 = zc[:, HY_WIDTH:2 * HY_WIDTH]
        v = zc[:, 2 * HY_WIDTH:]
        u = v * x1
        u_ref[b] = u.astype(BF16)
        ud_ref[b] = (u * db_ref[...]).astype(BF16)
        x0_ref[b] = x0.astype(BF16)


def _hyena_pre(p_rest, col, row_off, n_rows, conv_w, conv_b, d_bias):
    b, s, _ = p_rest.shape
    width = 3 * HY_WIDTH
    nb = HY_PRE_SAMPLES if b % HY_PRE_SAMPLES == 0 else 1
    out = jax.ShapeDtypeStruct((b, n_rows, HY_WIDTH), BF16)
    ospec = pl.BlockSpec((nb, ROW_TILE, HY_WIDTH), lambda bi, i: (bi, i, 0))
    return pl.pallas_call(
        _hypre_kernel,
        grid=(b // nb, n_rows // ROW_TILE),
        in_specs=_halo_specs(width, col, row_off, s, samples=nb) + [
            pl.BlockSpec((2, ROW_TILE, ROW_TILE), lambda bi, i: (0, 0, 0)),
            pl.BlockSpec((3, width), lambda bi, i: (0, 0)),
            pl.BlockSpec((1, width), lambda bi, i: (0, 0)),
            pl.BlockSpec((1, HY_WIDTH), lambda bi, i: (0, 0))],
        out_specs=[ospec, ospec, ospec],
        out_shape=[out, out, out],
        compiler_params=_cparams(2, 32),
        name="hyena_pre",
    )(p_rest, p_rest, p_rest, _shift_matrices(ROW_TILE), conv_w, conv_b.reshape(1, width),
      d_bias.reshape(1, HY_WIDTH))


def _hyfilt_kernel(z_ref, t_ref, dl_ref, w1_ref, b1_ref, wi_ref, bi_ref, fr_ref, wl_ref, o_ref):
    fr = fr_ref[...]
    h = jnp.sin(fr * (_dot3(z_ref[...], w1_ref[...]) + b1_ref[...]))
    for j in range(HY_INNER):
        h = jnp.sin(fr * (_dot3(h, wi_ref[j]) + bi_ref[j]))
    h = _dot3(h, wl_ref[...])
    decay = jnp.exp(-t_ref[...] * dl_ref[...])
    hf = h[:, :HY_WIDTH] * decay
    hb = h[:, HY_WIDTH:] * decay
    o_ref[...] = jnp.concatenate([hf + hb, hf - hb], axis=1)


def _pad2(a, rows, cols):
    return jnp.pad(a, ((0, rows - a.shape[0]), (0, cols - a.shape[1])))


def _hyena_filter_sums(n, w1, b1, wi, bi, freq, w_last):
    t = jnp.linspace(0.0, 1.0, n, dtype=F32)[:, None]
    w = 2.0 * math.pi * jnp.arange(n, dtype=F32)[:, None] / n
    f = jnp.linspace(1e-4, HY_BANDS - 1, HY_BANDS, dtype=F32)[None, :]
    z = jnp.concatenate([t, jnp.cos(f * w), -jnp.sin(f * w)], axis=-1)
    max_decay = math.log(HY_TARGET) / HY_FAST_DECAY
    min_decay = math.log(HY_TARGET) / HY_SLOW_DECAY
    deltas = jnp.abs(jnp.linspace(min_decay, max_decay, HY_WIDTH, dtype=F32))[None, :]
    zp = _pad2(z, n, LANE)
    w1p = _pad2(w1, LANE, LANE)
    b1p = _pad2(b1[None, :], 1, LANE)
    wip = jnp.stack([_pad2(wi[j], LANE, LANE) for j in range(HY_INNER)])
    bip = jnp.stack([_pad2(bi[j][None, :], 1, LANE) for j in range(HY_INNER)])
    frp = _pad2(freq[None, :], 1, LANE)
    wlp = _pad2(w_last, LANE, 2 * HY_WIDTH)
    tr = min(n, ROW_TILE)
    full = lambda shape: pl.BlockSpec(shape, lambda i: (0,) * len(shape))
    return pl.pallas_call(
        _hyfilt_kernel,
        grid=(n // tr,),
        in_specs=[pl.BlockSpec((tr, LANE), lambda i: (i, 0)),
                  pl.BlockSpec((tr, 1), lambda i: (i, 0)),
                  full((1, HY_WIDTH)), full((LANE, LANE)), full((1, LANE)),
                  full((HY_INNER, LANE, LANE)), full((HY_INNER, 1, LANE)), full((1, LANE)),
                  full((LANE, 2 * HY_WIDTH))],
        out_specs=pl.BlockSpec((tr, 2 * HY_WIDTH), lambda i: (i, 0)),
        out_shape=jax.ShapeDtypeStruct((n, 2 * HY_WIDTH), F32),
        compiler_params=_cparams(1, 32),
        name="hyena_filter",
    )(zp, t, deltas, w1p, b1p, wip, bip, frp, wlp)


def _dft_tables(n):
    assert n % DFT_T_LO == 0
    f = jnp.arange(n, dtype=jnp.int32)[:, None]
    t_hi = jnp.arange(n // DFT_T_LO, dtype=jnp.int32)[None, :] * DFT_T_LO
    t_lo = jnp.arange(DFT_T_LO, dtype=jnp.int32)[None, :]
    a = ((f * t_hi) % (2 * n)).astype(F32) * (math.pi / n)
    b = ((f * t_lo) % (2 * n)).astype(F32) * (math.pi / n)
    ca, sa = jnp.cos(a)[:, :, None], jnp.sin(a)[:, :, None]
    cb, sb = jnp.cos(b)[:, None, :], jnp.sin(b)[:, None, :]
    cos_t = (ca * cb - sa * sb).reshape(n, n)
    sin_t = (sa * cb + ca * sb).reshape(n, n)
    t = jnp.arange(n, dtype=jnp.int32)[None, :]
    nyq = jnp.where(t % 2 == 0, 1.0, -1.0).astype(F32)
    return jnp.stack([cos_t, jnp.where(f == 0, nyq, sin_t)])


def _hyfwd_kernel(u_ref, f_ref, co_ref, o_ref):
    u = u_ref[0]
    ure = _dot(f_ref[0], u)
    uim = _dot(f_ref[1], u)
    o_ref[0, 0] = (ure * co_ref[0] - uim * co_ref[1]).astype(BF16)
    o_ref[0, 1] = (ure * co_ref[2] + uim * co_ref[3]).astype(BF16)


def _hyinv_kernel(y_ref, ft_ref, ud_ref, x0_ref, o_ref):
    y = _dot(ft_ref[0], y_ref[0, 0]) + _dot(ft_ref[1], y_ref[0, 1])
    o_ref[0] = ((y + ud_ref[0].astype(F32)) * x0_ref[0].astype(F32)).astype(BF16)


def _hyena_conv(u, ud, x0, tables, coef):
    b, n, c = u.shape
    tf = min(n, DFT_ROWS)
    f_bf = tables.astype(BF16)
    ft_bf = jnp.swapaxes(tables, 1, 2).astype(BF16)
    spec = pl.pallas_call(
        _hyfwd_kernel,
        grid=(n // tf, b),
        in_specs=[pl.BlockSpec((1, n, c), lambda j, bi: (bi, 0, 0)),
                  pl.BlockSpec((2, tf, n), lambda j, bi: (0, j, 0)),
                  pl.BlockSpec((4, tf, c), lambda j, bi: (0, j, 0))],
        out_specs=pl.BlockSpec((1, 2, tf, c), lambda j, bi: (bi, 0, j, 0)),
        out_shape=jax.ShapeDtypeStruct((b, 2, n, c), BF16),
        compiler_params=_cparams(2, 56),
        name="hyena_dft",
    )(u, f_bf, coef)
    return pl.pallas_call(
        _hyinv_kernel,
        grid=(b, n // tf),
        in_specs=[pl.BlockSpec((1, 2, n, c), lambda bi, j: (bi, 0, 0, 0)),
                  pl.BlockSpec((2, tf, n), lambda bi, j: (0, j, 0)),
                  pl.BlockSpec((1, tf, c), lambda bi, j: (bi, j, 0)),
                  pl.BlockSpec((1, tf, c), lambda bi, j: (bi, j, 0))],
        out_specs=pl.BlockSpec((1, tf, c), lambda bi, j: (bi, j, 0)),
        out_shape=jax.ShapeDtypeStruct((b, n, c), BF16),
        compiler_params=_cparams(2, 40),
        name="hyena_idft",
    )(spec, ft_bf, ud, x0)


def _hyena_coef(tables, hsum_hdiff):
    n = tables.shape[1]
    c = HY_WIDTH
    r = _matmul_f32(tables.reshape(2 * n, n), hsum_hdiff, min(n, ROW_TILE), "hyena_kernel_dft")
    k_re = r[:n, :c]
    k_im = r[n:, c:]
    k_nyq = r[n:n + 1, :c]
    first = (jnp.arange(n) == 0)[:, None]
    scale = jnp.where(first, 1.0 / (2 * n), 2.0 / (2 * n)).astype(F32)
    zero = jnp.zeros_like(k_im)
    return jnp.stack([k_re * scale,
                      jnp.where(first, zero, k_im * scale),
                      jnp.where(first, zero, k_im * scale),
                      jnp.where(first, k_nyq, k_re) * scale])


def _merge_kernel(*refs, n_ctx_tiles, row_off, has_ctx):
    if has_ctx:
        (of_ref, ob_ref, zg_ref, att_ref, cx_ref, cc_ref, gate_ref, x_ref, m_ref, ghg_ref, gpost_ref,
         gffn_ref, woa_ref, wob_ref, woc_ref, wout_ref, o_ref, h_ref) = refs
    else:
        (of_ref, ob_ref, zg_ref, att_ref, cx_ref, gate_ref, x_ref, m_ref, ghg_ref, gpost_ref,
         gffn_ref, woa_ref, wob_ref, woc_ref, wout_ref, o_ref, h_ref) = refs
    nb, rows, d = x_ref.shape
    flat = lambda ref: ref[...].reshape(nb * rows, ref.shape[-1])
    o = flat(of_ref).astype(F32) + flat(ob_ref).astype(F32)
    ghg = ghg_ref[...]
    a = jnp.concatenate([_rms(o[:, h * HG_DIM:(h + 1) * HG_DIM], ghg) for h in range(HG_HEADS)], axis=1)
    zg = flat(zg_ref)
    a = a * (zg * _sigmoid(zg))
    c = flat(cx_ref)
    if has_ctx:
        c = jnp.where(pl.program_id(1) + row_off < n_ctx_tiles, flat(cc_ref), c)
    ya = _dot(a.astype(BF16), woa_ref[...])
    yb = _dot(flat(att_ref), wob_ref[...])
    yc = _dot(c, woc_ref[...])
    gates = flat(gate_ref)
    m = (_sigmoid(gates[:, 0:d].astype(F32)) * ya
         + _sigmoid(gates[:, d:2 * d].astype(F32)) * yb
         + _sigmoid(gates[:, 2 * d:3 * d].astype(F32)) * yc)
    y = _rms(_dot(m.astype(BF16), wout_ref[...]), gpost_ref[...])
    for b in range(nb):
        x_new = x_ref[b] + m_ref[b, 2:3, :] * y[b * rows:(b + 1) * rows]
        o_ref[b] = x_new
        h_ref[b] = (_rms(x_new, gffn_ref[...]) * (1.0 + m_ref[b, 4:5, :]) + m_ref[b, 3:4, :]).astype(BF16)


def _merge(o_f, o_b, p_hg, att, c_x, c_c, p_rest, xs, mods, g_hg, g_post, g_ffn, w_oa, w_ob, w_oc, w_out,
           row_off, n_ctx_rows):
    b, s, d = xs.shape
    nct = n_ctx_rows // ROW_TILE
    n_tiles = s // ROW_TILE - row_off
    has_ctx = c_c is not None
    nb = MERGE_SAMPLES if b % MERGE_SAMPLES == 0 else 1
    ctx_blk = b // nb

    def stream(width, col=0):
        return pl.BlockSpec((nb, ROW_TILE, width), lambda bi, i: (bi, i + row_off, col))

    def full(shape):
        return pl.BlockSpec(shape, lambda bi, i: (0,) * len(shape))

    in_specs = [stream(HG_WIDTH), stream(HG_WIDTH), stream(HG_WIDTH, 4), stream(AT_WIDTH),
                pl.BlockSpec((nb, ROW_TILE, HY_WIDTH),
                             lambda bi, i: (bi, jnp.maximum(i + row_off - nct, 0), 0))]
    args = [o_f, o_b, p_hg, att, c_x]
    if has_ctx:
        in_specs.append(pl.BlockSpec((nb, ROW_TILE, HY_WIDTH),
                                     lambda bi, i: (bi, jnp.minimum(i + row_off, nct - 1), 0)))
        args.append(c_c)
    in_specs += [stream(3 * d), stream(d),
                 pl.BlockSpec((nb, 6, d), lambda bi, i: (jnp.where(i + row_off < nct, ctx_blk, bi), 0, 0)),
                 full((1, HG_DIM)), full((1, d)), full((1, d)),
                 full((HG_WIDTH, d)), full((AT_WIDTH, d)), full((HY_WIDTH, d)), full((d, d))]
    args += [p_rest, xs, mods, g_hg.reshape(1, HG_DIM), g_post.reshape(1, d), g_ffn.reshape(1, d),
             w_oa, w_ob, w_oc, w_out]
    ospec = pl.BlockSpec((nb, ROW_TILE, d), lambda bi, i: (bi, i, 0))
    return pl.pallas_call(
        functools.partial(_merge_kernel, n_ctx_tiles=nct, row_off=row_off, has_ctx=has_ctx),
        grid=(b // nb, n_tiles),
        in_specs=in_specs,
        out_specs=[ospec, ospec],
        out_shape=[jax.ShapeDtypeStruct((b, n_tiles * ROW_TILE, d), F32),
                   jax.ShapeDtypeStruct((b, n_tiles * ROW_TILE, d), BF16)],
        compiler_params=_cparams(2, 48),
        name="merge",
    )(*args)


FFN_COLS = 256
FFN_DOWN_GROUPS = 2


def _ffn_kernel(*refs, first_tiles, last_tiles, has_next):
    if has_next:
        (h_ref, hp_ref, hn_ref, wu_ref, w_ref, b_ref, x_ref, m_ref, g_ref, wd_ref, mn_ref, gn_ref,
         o_ref, hx_ref, act_ref) = refs
    else:
        h_ref, hp_ref, hn_ref, wu_ref, w_ref, b_ref, x_ref, m_ref, g_ref, wd_ref, o_ref, act_ref = refs
    i = pl.program_id(1)
    is_first = functools.reduce(jnp.logical_or, [i == t for t in first_tiles])
    is_last = functools.reduce(jnp.logical_or, [i == t for t in last_tiles])
    d_ff = wd_ref.shape[0]
    nb, rows, _ = h_ref.shape
    halo = BF16_SUBLANES
    ext = rows + 2 * halo
    pieces = []
    for b in range(nb):
        pieces += [jnp.where(is_first, jnp.zeros_like(hp_ref[b]), hp_ref[b]), h_ref[b],
                   jnp.where(is_last, jnp.zeros_like(hn_ref[b]), hn_ref[b])]
    h_ext = jnp.concatenate(pieces, axis=0)

    def up(j):
        return [_dot(h_ext, wu_ref[:, base + j * FFN_COLS:base + (j + 1) * FFN_COLS]) for base in (0, d_ff)]

    def conv(u, cols):
        w = w_ref[:, cols]
        full = pltpu.roll(u, 1, axis=0) * w[0:1] + u * w[1:2] + pltpu.roll(u, nb * ext - 1, axis=0) * w[2:3]
        kept = [full[b * ext + halo:b * ext + halo + rows] for b in range(nb)]
        return (kept[0] if nb == 1 else jnp.concatenate(kept, axis=0)) + b_ref[:, cols]

    n_chunks = d_ff // FFN_COLS
    per_group = -(-n_chunks // FFN_DOWN_GROUPS)
    acc = None
    u_next = up(0)
    for j in range(n_chunks):
        u = u_next
        if j + 1 < n_chunks:
            u_next = up(j + 1)
        a = conv(u[0], slice(j * FFN_COLS, (j + 1) * FFN_COLS))
        g = conv(u[1], slice(d_ff + j * FFN_COLS, d_ff + (j + 1) * FFN_COLS))
        act_ref[:, j * FFN_COLS:(j + 1) * FFN_COLS] = (a * _sigmoid(a) * g).astype(BF16)
        if (j + 1) % per_group == 0 or j + 1 == n_chunks:
            lo = (j // per_group) * per_group * FFN_COLS
            part = _dot(act_ref[:, lo:(j + 1) * FFN_COLS], wd_ref[lo:(j + 1) * FFN_COLS, :])
            acc = part if acc is None else acc + part
    y = _rms(acc, g_ref[...])
    for b in range(nb):
        x_new = x_ref[b] + m_ref[b, 5:6, :] * y[b * rows:(b + 1) * rows]
        o_ref[b] = x_new
        if has_next:
            hx_ref[b] = (_rms(x_new, gn_ref[...]) * (1.0 + mn_ref[b, 1:2, :])
                         + mn_ref[b, 0:1, :]).astype(BF16)


def _ffn(h, xs, mods, w_up, conv_w, conv_b, g_post, w_down, n_ctx_rows, mods_next=None, g_next=None):
    b, s, d = xs.shape
    d_ff = w_down.shape[0]
    rows = FFN_ROWS if (n_ctx_rows % FFN_ROWS == 0 and s % FFN_ROWS == 0) else ROW_TILE
    nt = s // rows
    nct = n_ctx_rows // rows
    first_tiles = tuple(sorted({0, nct}))
    last_tiles = tuple(sorted({nct - 1, nt - 1} - {-1}))
    nb = FFN_ROWS // rows if b % (FFN_ROWS // rows) == 0 and FFN_ROWS // rows <= MERGE_SAMPLES else 1
    ctx_blk = b // nb
    has_next = mods_next is not None
    full = lambda shape: pl.BlockSpec(shape, lambda bi, i: (0,) * len(shape))
    mspec = pl.BlockSpec((nb, 6, d), lambda bi, i: (jnp.where(i < nct, ctx_blk, bi), 0, 0))
    ospec = pl.BlockSpec((nb, rows, d), lambda bi, i: (bi, i, 0))
    resident = lambda shape: pl.BlockSpec(shape, lambda bi, i: (0,) * len(shape), pipeline_mode=pl.Buffered(1))
    in_specs = _halo_specs(d, 0, 0, s, rows, nb) + [
        resident((d, 2 * d_ff)), full((3, 2 * d_ff)), full((1, 2 * d_ff)),
        ospec, mspec, full((1, d)), resident((d_ff, d))]
    args = [h, h, h, w_up, conv_w, conv_b.reshape(1, 2 * d_ff), xs, mods, g_post.reshape(1, d), w_down]
    out_specs = [ospec]
    out_shape = [jax.ShapeDtypeStruct((b, s, d), F32)]
    if has_next:
        in_specs += [mspec, full((1, d))]
        args += [mods_next, g_next.reshape(1, d)]
        out_specs.append(ospec)
        out_shape.append(jax.ShapeDtypeStruct((b, s, d), BF16))
    return pl.pallas_call(
        functools.partial(_ffn_kernel, first_tiles=first_tiles, last_tiles=last_tiles, has_next=has_next),
        grid=(b // nb, nt),
        in_specs=in_specs,
        out_specs=out_specs,
        out_shape=out_shape,
        scratch_shapes=[pltpu.VMEM((nb * rows, d_ff), BF16)],
        compiler_params=_cparams(2, 56),
        name="ffn",
    )(*args)


def _deinterleave():
    return np.concatenate([np.arange(0, AT_DIM, 2), np.arange(1, AT_DIM, 2)])


def _q_head_order():
    return [h for j in range(AT_GROUP) for h in (j, AT_GROUP + j)]


def _largest_tile(n, cap):
    best = LANE
    for t in range(LANE, cap + 1, LANE):
        if n % t == 0:
            best = t
    return best


def kernel(x, c, ctx, c_ctx, w_ada, b_ada, g_pre_mix, g_post_mix, g_pre_ffn, g_post_ffn, w_in, hg_lower_bounds, hg_norm, q_norm, k_norm, hy_conv_w, hy_conv_b, hy_w1, hy_b1, hy_wi, hy_bi, hy_freq, hy_w_last, hy_bias, w_oa, w_ob, w_oc, w_out, w_up, ffn_conv_w, ffn_conv_b, w_down):
    bsz, n_lat, d = x.shape
    n_ctx = ctx.shape[1]
    depth = w_ada.shape[0]
    d_ff = w_down.shape[1]
    assert AT_KV_HEADS == 2 and AT_GROUP * LANE == AT_WIDTH and AT_KV_WIDTH == LANE
    assert n_ctx % ROW_TILE == 0 and n_lat % ROW_TILE == 0 and n_lat % GRID_W == 0
    assert (bsz * (n_ctx + n_lat)) % MM_ROWS == 0 and (bsz * n_lat) % MM_ROWS == 0

    lbp = jax.nn.softmax(hg_lower_bounds.astype(F32), axis=0)
    lower = jnp.cumsum(lbp, axis=0) - lbp[0]

    rp = -(-(bsz + MERGE_SAMPLES) // 8) * 8
    src = jnp.concatenate([c, jnp.tile(c_ctx[None, :], (MERGE_SAMPLES, 1)),
                           jnp.zeros((rp - bsz - MERGE_SAMPLES, d), F32)], axis=0)
    mods_all = _ada(src, w_ada, b_ada).reshape(depth, rp, 6, d)

    o_q = 5 * HG_WIDTH
    o_k = o_q + AT_WIDTH
    o_v = o_k + AT_KV_WIDTH
    o_hy = o_v + AT_KV_WIDTH
    o_gate = o_hy + 3 * HY_WIDTH
    deint = _deinterleave()
    q_cols = np.concatenate([o_q + h * AT_DIM + deint for h in _q_head_order()])
    k_cols = np.concatenate([o_k + g * AT_DIM + deint for g in range(AT_KV_HEADS)])
    qk_cols = np.concatenate([q_cols, k_cols])
    col_hy = (3 * d) // (3 * HY_WIDTH)
    col_q = (3 * d + 3 * HY_WIDTH) // AT_WIDTH
    col_k = (3 * d + 3 * HY_WIDTH + AT_WIDTH) // AT_KV_WIDTH
    col_v = col_k + 1
    assert (3 * d) % (3 * HY_WIDTH) == 0 and (3 * d + 3 * HY_WIDTH) % AT_WIDTH == 0
    ob_rows = np.concatenate([np.arange(h * AT_DIM, (h + 1) * AT_DIM) for h in _q_head_order()])

    rope_tabs = _rope_tables(n_ctx, n_lat)
    dft_lat = _dft_tables(n_lat)
    dft_ctx = _dft_tables(n_ctx)
    nct = n_ctx // ROW_TILE

    s_all = n_ctx + n_lat
    xs, h = _join_modulate(ctx, x, mods_all[0], g_pre_mix[0], 0, 1)
    for l in range(depth):
        need_ctx = l < depth - 1
        mods = mods_all[l]
        w_hg = w_in[l][:, :5 * HG_WIDTH].astype(BF16)
        w_rest = jnp.concatenate([w_in[l][:, o_gate:o_gate + 3 * d], w_in[l][:, o_hy:o_hy + 3 * HY_WIDTH],
                                  w_in[l][:, qk_cols], w_in[l][:, o_v:o_v + AT_KV_WIDTH]], axis=1).astype(BF16)

        h = h.reshape(bsz * s_all, d)
        p_hg = _matmul(h, w_hg, F32, _largest_tile(5 * HG_WIDTH, 1280), "proj_hgrn").reshape(bsz, s_all, -1)
        p_rest = _matmul(h, w_rest, BF16, _largest_tile(w_rest.shape[1], 1792), "proj_rest").reshape(bsz, s_all, -1)

        o_f, o_b = _hgrn(p_hg, lower[l], n_ctx)

        gq = jnp.tile(q_norm[l][deint], AT_HEADS)[None, :]
        gk = jnp.tile(k_norm[l][deint], AT_KV_HEADS)[None, :]
        row_off = 0 if need_ctx else nct
        att = _attention(p_rest, col_q, col_k, col_v, rope_tabs, gq, gk, row_off, n_ctx)

        filt_args = (hy_w1[l], hy_b1[l], hy_wi[l], hy_bi[l], hy_freq[l], hy_w_last[l])
        coef = _hyena_coef(dft_lat, _hyena_filter_sums(n_lat, *filt_args))
        c_x = _hyena_conv(*_hyena_pre(p_rest, col_hy, nct, n_lat, hy_conv_w[l], hy_conv_b[l], hy_bias[l]),
                          dft_lat, coef)
        c_c = None
        if need_ctx:
            coef_c = _hyena_coef(dft_ctx, _hyena_filter_sums(n_ctx, *filt_args))
            c_c = _hyena_conv(*_hyena_pre(p_rest, col_hy, 0, n_ctx, hy_conv_w[l], hy_conv_b[l], hy_bias[l]),
                              dft_ctx, coef_c)

        xs, h2 = _merge(o_f, o_b, p_hg, att, c_x, c_c, p_rest, xs, mods, hg_norm[l], g_post_mix[l],
                        g_pre_ffn[l], w_oa[l].astype(BF16), w_ob[l][ob_rows].astype(BF16),
                        w_oc[l].astype(BF16), w_out[l].astype(BF16), row_off, n_ctx)
        n_ctx_now = n_ctx if need_ctx else 0
        ffn_args = (h2, xs, mods, w_up[l].astype(BF16), ffn_conv_w[l], ffn_conv_b[l], g_post_ffn[l],
                    w_down[l].astype(BF16), n_ctx_now)
        if need_ctx:
            xs, h = _ffn(*ffn_args, mods_all[l + 1], g_pre_mix[l + 1])
        else:
            xs, = _ffn(*ffn_args)
    return xs
```

```python
import functools
import math

import jax
import jax.numpy as jnp
import numpy as np
from jax import lax
from jax.experimental import pallas as pl
from jax.experimental.pallas import tpu as pltpu

F32 = jnp.float32
BF16 = jnp.bfloat16

NORM_EPS = 1e-6
GRID_W = 64
HG_HEADS = 4
HG_DIM = 128
HG_WIDTH = HG_HEADS * HG_DIM
HG_EXP_CLIP = 30.0
AT_HEADS = 8
AT_KV_HEADS = 2
AT_DIM = 64
AT_GROUP = AT_HEADS // AT_KV_HEADS
AT_WIDTH = AT_HEADS * AT_DIM
AT_KV_WIDTH = AT_KV_HEADS * AT_DIM
ROPE_THETA = 10000.0
HY_WIDTH = 512
HY_EMB_DIM = 33
HY_BANDS = (HY_EMB_DIM - 1) // 2
HY_FILTER_WIDTH = 64
HY_INNER = 2
HY_FAST_DECAY = 0.3
HY_SLOW_DECAY = 1.5
HY_TARGET = 1e-2

LANE = 128
BF16_SUBLANES = 16
ROW_TILE = 256
MERGE_SAMPLES = 2
HG_CHUNK = 128
HG_BLOCK = 256
HG_LEVELS = tuple(HG_CHUNK >> (j + 1) for j in range(int(math.log2(HG_CHUNK))))
MM_ROWS = 1024
DFT_ROWS = 1024
DFT_T_LO = 64
AT_LOOKAHEAD = 2
FFN_ROWS = 512
VMEM_CAP = 56 * 1024 * 1024


def _cparams(n_axes, vmem_mb):
    return pltpu.CompilerParams(
        dimension_semantics=("arbitrary",) * n_axes,
        vmem_limit_bytes=min(int(vmem_mb) * 1024 * 1024, VMEM_CAP))


def _dot(a, b):
    return jnp.dot(a, b, preferred_element_type=F32)


def _dot_nt(a, b):
    return lax.dot_general(a, b, (((1,), (1,)), ((), ())), preferred_element_type=F32)


def _split_bf16(a):
    hi = a.astype(BF16)
    lo = (a - hi.astype(F32)).astype(BF16)
    return hi, lo


def _dot3(a, b):
    ah, al = _split_bf16(a)
    bh, bl = _split_bf16(b)
    return _dot(ah, bh) + (_dot(ah, bl) + _dot(al, bh))


def _rms(x, g):
    return x * lax.rsqrt(jnp.mean(x * x, axis=-1, keepdims=True) + NORM_EPS) * g


def _sigmoid(x):
    return 0.5 * jnp.tanh(0.5 * x) + 0.5


def _ada_kernel(src_ref, w_ref, b_ref, o_ref):
    s = src_ref[...]
    s = s * _sigmoid(s)
    o_ref[0] = _dot3(s, w_ref[0]) + b_ref[0]


def _ada(src, w_ada, b_ada):
    depth, d, d6 = w_ada.shape
    rp = src.shape[0]
    tn = d
    return pl.pallas_call(
        _ada_kernel,
        grid=(depth, d6 // tn),
        in_specs=[pl.BlockSpec((rp, d), lambda l, j: (0, 0)),
                  pl.BlockSpec((1, d, tn), lambda l, j: (l, 0, j)),
                  pl.BlockSpec((1, 1, tn), lambda l, j: (l, 0, j))],
        out_specs=pl.BlockSpec((1, rp, tn), lambda l, j: (l, 0, j)),
        out_shape=jax.ShapeDtypeStruct((depth, rp, d6), F32),
        compiler_params=_cparams(2, 32),
        name="ada",
    )(src, w_ada, b_ada.reshape(depth, 1, d6))


def _mod_kernel(c_ref, x_ref, m_ref, g_ref, xs_ref, o_ref, *, k_shift, k_scale, n_ctx_tiles):
    for b in range(x_ref.shape[0]):
        x = jnp.where(pl.program_id(1) < n_ctx_tiles, c_ref[b], x_ref[b])
        xs_ref[b] = x
        shift = m_ref[b, k_shift:k_shift + 1, :]
        scale = m_ref[b, k_scale:k_scale + 1, :]
        o_ref[b] = (_rms(x, g_ref[...]) * (1.0 + scale) + shift).astype(o_ref.dtype)


def _join_modulate(ctx, x, mods, g, k_shift, k_scale):
    b, n_lat, d = x.shape
    n_ctx_tiles = ctx.shape[1] // ROW_TILE
    nt = n_ctx_tiles + n_lat // ROW_TILE
    nb = MERGE_SAMPLES if b % MERGE_SAMPLES == 0 else 1
    ctx_blk = b // nb
    ospec = pl.BlockSpec((nb, ROW_TILE, d), lambda bi, i: (bi, i, 0))
    return pl.pallas_call(
        functools.partial(_mod_kernel, k_shift=k_shift, k_scale=k_scale, n_ctx_tiles=n_ctx_tiles),
        grid=(b // nb, nt),
        in_specs=[pl.BlockSpec((nb, ROW_TILE, d), lambda bi, i: (bi, jnp.minimum(i, n_ctx_tiles - 1), 0)),
                  pl.BlockSpec((nb, ROW_TILE, d), lambda bi, i: (bi, jnp.maximum(i - n_ctx_tiles, 0), 0)),
                  pl.BlockSpec((nb, 6, d), lambda bi, i: (jnp.where(i < n_ctx_tiles, ctx_blk, bi), 0, 0)),
                  pl.BlockSpec((1, d), lambda bi, i: (0, 0))],
        out_specs=[ospec, ospec],
        out_shape=[jax.ShapeDtypeStruct((b, nt * ROW_TILE, d), F32),
                   jax.ShapeDtypeStruct((b, nt * ROW_TILE, d), BF16)],
        compiler_params=_cparams(2, 16),
        name="modulate",
    )(ctx, x, mods, g.reshape(1, d))


def _mm_kernel(a_ref, b_ref, o_ref):
    o_ref[...] = _dot(a_ref[...], b_ref[...]).astype(o_ref.dtype)


def _matmul(a, w, out_dtype, tn, name):
    m, k = a.shape
    n = w.shape[1]
    tm = MM_ROWS
    assert m % tm == 0 and n % tn == 0
    return pl.pallas_call(
        _mm_kernel,
        grid=(m // tm, n // tn),
        in_specs=[pl.BlockSpec((tm, k), lambda i, j: (i, 0)),
                  pl.BlockSpec((k, tn), lambda i, j: (0, j))],
        out_specs=pl.BlockSpec((tm, tn), lambda i, j: (i, j)),
        out_shape=jax.ShapeDtypeStruct((m, n), out_dtype),
        compiler_params=_cparams(2, 56),
        name=name,
    )(a, w)


def _mm3_kernel(a_ref, b_ref, o_ref):
    o_ref[...] = _dot3(a_ref[...], b_ref[...])


def _matmul_f32(a, w, tm, name):
    m, k = a.shape
    n = w.shape[1]
    return pl.pallas_call(
        _mm3_kernel,
        grid=(m // tm,),
        in_specs=[pl.BlockSpec((tm, k), lambda i: (i, 0)),
                  pl.BlockSpec((k, n), lambda i: (0, 0))],
        out_specs=pl.BlockSpec((tm, n), lambda i: (i, 0)),
        out_shape=jax.ShapeDtypeStruct((m, n), F32),
        compiler_params=_cparams(1, 48),
        name=name,
    )(a, w)


def _hg_scan_matrix(reverse):
    t_n = HG_CHUNK
    t = np.arange(t_n)[:, None]
    u = np.arange(t_n)[None, :]
    rows = [(u >= t) if reverse else (u <= t)]
    for w in HG_LEVELS:
        base = (t // (2 * w)) * (2 * w)
        mid = base + w
        upper = (t - base) >= w
        if reverse:
            m = np.where(upper, (u >= mid) & (u < t), (u >= t) & (u < mid))
        else:
            m = np.where(upper, (u >= mid) & (u <= t), (u > t) & (u < mid))
        rows.append(m)
    m = np.concatenate(rows, axis=0).astype(np.float32)
    return np.concatenate([m, m], axis=1)


def _hgrn_kernel(qf_ref, zf_ref, vf_ref, qb_ref, zb_ref, vb_ref, lb_ref, pf_ref, pb_ref,
                 of_ref, ob_ref, s_ref):
    @pl.when(pl.program_id(1) == 0)
    def _():
        s_ref[...] = jnp.zeros_like(s_ref)

    t_n = HG_CHUNK
    ti = lax.broadcasted_iota(jnp.int32, (t_n, t_n), 0)
    si = lax.broadcasted_iota(jnp.int32, (t_n, t_n), 1)
    tx = ti ^ si
    dirs = ((qf_ref, zf_ref, vf_ref, pf_ref, of_ref), (qb_ref, zb_ref, vb_ref, pb_ref, ob_ref))
    masks = []
    for reverse in (False, True):
        later = (ti < si) if reverse else (ti > si)
        masks.append([later & (tx >= w) & (tx < 2 * w) for w in HG_LEVELS])
    n_sub = qf_ref.shape[1] // t_n
    heads = range(HG_HEADS)
    units = [(d, c, h) for d in range(2) for c in range(n_sub) for h in heads]
    cols = lambda h: slice(h * HG_DIM, (h + 1) * HG_DIM)
    rows = lambda c: slice(c * t_n, (c + 1) * t_n)

    kk, kb, qb, cat, x, ex = {}, {}, {}, {}, {}, {}
    a = {u: jnp.zeros((t_n, t_n), F32) for u in units}

    def gates(d, c, h):
        z = dirs[d][1][0, rows(c), cols(h)]
        lb = lb_ref[d:d + 1, cols(h)]
        e = jnp.exp(-jnp.abs(z))
        log_num = jnp.log(jnp.where(z >= 0.0, 1.0 + lb * e, e + lb))
        log_clip = z + jnp.log(1.0 + lb * math.exp(HG_EXP_CLIP))
        lf = jnp.where(z < -HG_EXP_CLIP, log_clip, log_num) - jnp.log(1.0 + e)
        half = 0.5 * (1.0 - lb)
        kk[d, c, h] = half - half * jnp.tanh(0.5 * z)
        kb[d, c, h] = kk[d, c, h].astype(BF16)
        qb[d, c, h] = dirs[d][0][0, rows(c), cols(h)].astype(BF16)
        hi, lo = _split_bf16(lf)
        cat[d, c, h] = jnp.concatenate([hi, lo], axis=0)

    def exponents(d):
        rhs = jnp.concatenate([cat[d, c, h] for c in range(n_sub) for h in heads], axis=1)
        xd = _dot(dirs[d][3][...], rhs)
        ed = jnp.exp(xd[t_n:]).astype(BF16)
        for c in range(n_sub):
            for h in heads:
                x[d, c, h] = xd[0:t_n, cols(c * HG_HEADS + h)]
                ex[d, c, h] = ed[:, cols(c * HG_HEADS + h)]

    def level(d, c, j):
        for h in heads:
            u = (d, c, h)
            ew = ex[u][j * t_n:(j + 1) * t_n]
            pw = _dot_nt(qb[u] * ew, kb[u] * ew)
            a[u] = jnp.where(masks[d][j], pw, a[u])

    def finish(d, c, h):
        u = (d, c, h)
        q = dirs[d][0][0, rows(c), cols(h)]
        v = dirs[d][2][0, rows(c), cols(h)]
        g = x[u]
        g_last = g[0:1] if d == 1 else g[t_n - 1:t_n]
        st = s_ref[d * HG_HEADS + h]
        dqk = jnp.sum(q * kk[u], axis=1, keepdims=True)
        o = (_dot(a[u].astype(BF16), v.astype(BF16)) + dqk * v
             + _dot_nt((q * jnp.exp(g)).astype(BF16), st.astype(BF16)))
        kd = (kk[u] * jnp.exp(g_last - g)).astype(BF16)
        dirs[d][4][0, rows(c), cols(h)] = o.astype(BF16)
        s_ref[d * HG_HEADS + h] = st * jnp.exp(g_last) + _dot(v.T.astype(BF16), kd)

    for d in range(2):
        for c in range(n_sub):
            for h in heads:
                gates(d, c, h)
        exponents(d)
        for j in range(len(HG_LEVELS)):
            for c in range(n_sub):
                level(d, c, j)
        for c in (range(n_sub) if d == 0 else reversed(range(n_sub))):
            for h in heads:
                finish(d, c, h)


def _hgrn(p_hg, lb, n_ctx_rows):
    b, s, _ = p_hg.shape
    assert s % HG_BLOCK == 0 and n_ctx_rows % HG_BLOCK == 0
    nb = s // HG_BLOCK
    nc = n_ctx_rows // HG_BLOCK

    def bidx(n):
        return jnp.where(n < nc, nc - 1 - n, nb - 1 - (n - nc))

    blk = (1, HG_BLOCK, HG_WIDTH)
    pf = jnp.asarray(_hg_scan_matrix(False)).astype(BF16)
    pb = jnp.asarray(_hg_scan_matrix(True)).astype(BF16)
    pshape = pf.shape
    return pl.pallas_call(
        _hgrn_kernel,
        grid=(b, nb),
        in_specs=[pl.BlockSpec(blk, lambda bi, n: (bi, n, 0)),
                  pl.BlockSpec(blk, lambda bi, n: (bi, n, 1)),
                  pl.BlockSpec(blk, lambda bi, n: (bi, n, 3)),
                  pl.BlockSpec(blk, lambda bi, n: (bi, bidx(n), 0)),
                  pl.BlockSpec(blk, lambda bi, n: (bi, bidx(n), 2)),
                  pl.BlockSpec(blk, lambda bi, n: (bi, bidx(n), 3)),
                  pl.BlockSpec((2, HG_WIDTH), lambda bi, n: (0, 0)),
                  pl.BlockSpec(pshape, lambda bi, n: (0, 0)),
                  pl.BlockSpec(pshape, lambda bi, n: (0, 0))],
        out_specs=[pl.BlockSpec(blk, lambda bi, n: (bi, n, 0)),
                   pl.BlockSpec(blk, lambda bi, n: (bi, bidx(n), 0))],
        out_shape=[jax.ShapeDtypeStruct((b, s, HG_WIDTH), BF16)] * 2,
        scratch_shapes=[pltpu.VMEM((2 * HG_HEADS, HG_DIM, HG_DIM), F32)],
        compiler_params=_cparams(2, 40),
        name="hgrn",
    )(p_hg, p_hg, p_hg, p_hg, p_hg, p_hg, lb, pf, pb)


def _rope(x, cos, sin_signed, first_half):
    n = x.shape[-1]
    half = AT_DIM // 2
    partner = jnp.where(first_half, pltpu.roll(x, n - half, axis=1), pltpu.roll(x, half, axis=1))
    return x * cos + partner * sin_signed


def _head_norm(x, gain, group_mean):
    ms = _dot((x * x).astype(BF16), group_mean)
    return x * lax.rsqrt(ms + NORM_EPS) * gain


def _attn_kernel(q_ref, k_ref, v_ref, cq_ref, sq_ref, ck_ref, sk_ref, gq_ref, gk_ref, mq_ref, mk_ref,
                 o_ref, k_scr, v_scr, *, q_off, n_ctx_tiles, n_ctx_rows):
    i = pl.program_id(1)

    @pl.when(i == 0)
    def _():
        kr = k_ref[0].astype(F32)
        lane = lax.broadcasted_iota(jnp.int32, kr.shape, 1)
        kn = _head_norm(kr, gk_ref[...], mk_ref[...])
        k_scr[...] = _rope(kn, ck_ref[...], sk_ref[...], (lane % AT_DIM) < AT_DIM // 2).astype(BF16)
        v = v_ref[0]
        one = jnp.ones_like(v)
        v_scr[0] = jnp.where(lane < AT_DIM, v, one)
        v_scr[1] = jnp.where(lane < AT_DIM, one, v)

    qr = q_ref[0].astype(F32)
    lane = lax.broadcasted_iota(jnp.int32, qr.shape, 1)
    qn = _head_norm(qr, gq_ref[...], mq_ref[...])
    qn = _rope(qn, cq_ref[...], sq_ref[...], (lane % AT_DIM) < AT_DIM // 2)
    qn = (qn * (AT_DIM ** -0.5 * math.log2(math.e))).astype(BF16)
    lane_t = lax.broadcasted_iota(jnp.int32, (ROW_TILE, LANE), 1)
    kv0 = lane_t < AT_DIM

    def attend(n_keys):
        keys = k_scr[0:n_keys, :]
        heads = [(j, g) for j in range(AT_WIDTH // LANE) for g in range(AT_KV_HEADS)]

        def scores(j, g):
            qt = qn[:, j * LANE:(j + 1) * LANE]
            return _dot_nt(jnp.where(kv0 if g == 0 else ~kv0, qt, jnp.zeros_like(qt)), keys)

        pending = [scores(*heads[n]) for n in range(AT_LOOKAHEAD)]
        outs = {}
        for n, (j, g) in enumerate(heads):
            s = pending.pop(0)
            if n + AT_LOOKAHEAD < len(heads):
                pending.append(scores(*heads[n + AT_LOOKAHEAD]))
            p = jnp.exp2(s - jnp.max(s, axis=1, keepdims=True))
            outs[g] = _dot(p.astype(BF16), v_scr[g, 0:n_keys, :])
            if g == AT_KV_HEADS - 1:
                num = jnp.where(kv0, outs[0], outs[1])
                den = pltpu.roll(jnp.where(kv0, outs[1], outs[0]), AT_DIM, axis=1)
                o_ref[0, :, j * LANE:(j + 1) * LANE] = (num / den).astype(o_ref.dtype)

    n_all = k_scr.shape[0]
    if q_off < n_ctx_tiles:
        @pl.when(i + q_off < n_ctx_tiles)
        def _():
            attend(n_ctx_rows)

        @pl.when(i + q_off >= n_ctx_tiles)
        def _():
            attend(n_all)
    else:
        attend(n_all)


def _attention(p_rest, col_q, col_k, col_v, tabs, gq, gk, q_off, n_ctx_rows):
    b, s, _ = p_rest.shape
    nt = s // ROW_TILE
    cq, sq, ck, sk = tabs
    mq = jnp.asarray(np.kron(np.eye(AT_HEADS), np.full((AT_DIM, AT_DIM), 1.0 / AT_DIM)), BF16)
    mk = jnp.asarray(np.kron(np.eye(AT_KV_HEADS), np.full((AT_DIM, AT_DIM), 1.0 / AT_DIM)), BF16)
    kern = functools.partial(_attn_kernel, q_off=q_off, n_ctx_tiles=n_ctx_rows // ROW_TILE,
                             n_ctx_rows=n_ctx_rows)
    return pl.pallas_call(
        kern,
        grid=(b, nt - q_off),
        in_specs=[pl.BlockSpec((1, ROW_TILE, AT_WIDTH), lambda bi, i: (bi, i + q_off, col_q)),
                  pl.BlockSpec((1, s, AT_KV_WIDTH), lambda bi, i: (bi, 0, col_k)),
                  pl.BlockSpec((1, s, AT_KV_WIDTH), lambda bi, i: (bi, 0, col_v)),
                  pl.BlockSpec((ROW_TILE, AT_WIDTH), lambda bi, i: (i + q_off, 0)),
                  pl.BlockSpec((ROW_TILE, AT_WIDTH), lambda bi, i: (i + q_off, 0)),
                  pl.BlockSpec((s, AT_KV_WIDTH), lambda bi, i: (0, 0)),
                  pl.BlockSpec((s, AT_KV_WIDTH), lambda bi, i: (0, 0)),
                  pl.BlockSpec((1, AT_WIDTH), lambda bi, i: (0, 0)),
                  pl.BlockSpec((1, AT_KV_WIDTH), lambda bi, i: (0, 0)),
                  pl.BlockSpec((AT_WIDTH, AT_WIDTH), lambda bi, i: (0, 0)),
                  pl.BlockSpec((AT_KV_WIDTH, AT_KV_WIDTH), lambda bi, i: (0, 0))],
        out_specs=pl.BlockSpec((1, ROW_TILE, AT_WIDTH), lambda bi, i: (bi, i + q_off, 0)),
        out_shape=jax.ShapeDtypeStruct((b, s, AT_WIDTH), BF16),
        scratch_shapes=[pltpu.VMEM((s, AT_KV_WIDTH), BF16),
                        pltpu.VMEM((AT_KV_HEADS, s, AT_KV_WIDTH), BF16)],
        compiler_params=_cparams(2, 48),
        name="attention",
    )(p_rest, p_rest, p_rest, cq, sq, ck, sk, gq, gk, mq, mk)


def _rope_tables(n_ctx_rows, n_lat_rows):
    rows = n_lat_rows // GRID_W
    row = jnp.repeat(jnp.arange(rows), GRID_W).astype(F32)
    col = jnp.tile(jnp.arange(GRID_W), rows).astype(F32)
    n_freq = AT_DIM // 4
    inv = ROPE_THETA ** (-jnp.arange(n_freq, dtype=F32) / n_freq)
    ang = jnp.concatenate([row[:, None] * inv, col[:, None] * inv], axis=-1)
    cos = jnp.concatenate([jnp.cos(ang), jnp.cos(ang)], axis=-1)
    sin = jnp.concatenate([-jnp.sin(ang), jnp.sin(ang)], axis=-1)
    cos = jnp.concatenate([jnp.ones((n_ctx_rows, AT_DIM), F32), cos], axis=0)
    sin = jnp.concatenate([jnp.zeros((n_ctx_rows, AT_DIM), F32), sin], axis=0)
    return (jnp.tile(cos, (1, AT_HEADS)), jnp.tile(sin, (1, AT_HEADS)),
            jnp.tile(cos, (1, AT_KV_HEADS)), jnp.tile(sin, (1, AT_KV_HEADS)))


def _shift_matrices(n):
    i = np.arange(n)
    down = i[:, None] - 1 == i[None, :]
    up = i[:, None] + 1 == i[None, :]
    return jnp.asarray(np.stack([down, up]).astype(np.float32), BF16)


def _conv3(xb, shift_ref, prev_row, next_row, w, bias):
    n, c = xb.shape
    sub = 8
    r = lax.broadcasted_iota(jnp.int32, (sub, c), 0)
    x = xb.astype(F32)
    if shift_ref is None:
        xm = pltpu.roll(x, 1, axis=0)
        xp = pltpu.roll(x, n - 1, axis=0)
    else:
        xm = _dot(shift_ref[0], xb)
        xp = _dot(shift_ref[1], xb)
    xm = jnp.concatenate([jnp.where(r == 0, prev_row, xm[0:sub]), xm[sub:]], axis=0)
    xp = jnp.concatenate([xp[:n - sub], jnp.where(r == sub - 1, next_row, xp[n - sub:])], axis=0)
    return xm * w[0:1] + x * w[1:2] + xp * w[2:3] + bias


def _halo_specs(width, col, row_off, n_rows, rows=ROW_TILE, samples=1):
    per = rows // BF16_SUBLANES
    last = n_rows // BF16_SUBLANES - 1
    return [
        pl.BlockSpec((samples, rows, width), lambda bi, i: (bi, i + row_off, col)),
        pl.BlockSpec((samples, BF16_SUBLANES, width),
                     lambda bi, i: (bi, jnp.maximum((i + row_off) * per - 1, 0), col)),
        pl.BlockSpec((samples, BF16_SUBLANES, width),
                     lambda bi, i: (bi, jnp.minimum((i + row_off + 1) * per, last), col)),
    ]


def _halo_rows(prev_ref, next_ref, is_first, is_last, b):
    prev_row = prev_ref[b, BF16_SUBLANES - 1:BF16_SUBLANES, :].astype(F32)
    next_row = next_ref[b, 0:1, :].astype(F32)
    prev_row = jnp.where(is_first, 0.0, prev_row)
    next_row = jnp.where(is_last, 0.0, next_row)
    return prev_row, next_row


HY_PRE_SAMPLES = 4


def _hypre_kernel(z_ref, zp_ref, zn_ref, sh_ref, w_ref, b_ref, db_ref, u_ref, ud_ref, x0_ref):
    i = pl.program_id(1)
    for b in range(z_ref.shape[0]):
        prev_row, next_row = _halo_rows(zp_ref, zn_ref, i == 0, i == pl.num_programs(1) - 1, b)
        zc = _conv3(z_ref[b], sh_ref, prev_row, next_row, w_ref[...], b_ref[...])
        x0 = zc[:, :HY_WIDTH]
        x1 = zc[:, HY_WIDTH:2 * HY_WIDTH]
        v = zc[:, 2 * HY_WIDTH:]
        u = v * x1
        u_ref[b] = u.astype(BF16)
        ud_ref[b] = (u * db_ref[...]).astype(BF16)
        x0_ref[b] = x0.astype(BF16)


def _hyena_pre(p_rest, col, row_off, n_rows, conv_w, conv_b, d_bias):
    b, s, _ = p_rest.shape
    width = 3 * HY_WIDTH
    nb = HY_PRE_SAMPLES if b % HY_PRE_SAMPLES == 0 else 1
    out = jax.ShapeDtypeStruct((b, n_rows, HY_WIDTH), BF16)
    ospec = pl.BlockSpec((nb, ROW_TILE, HY_WIDTH), lambda bi, i: (bi, i, 0))
    return pl.pallas_call(
        _hypre_kernel,
        grid=(b // nb, n_rows // ROW_TILE),
        in_specs=_halo_specs(width, col, row_off, s, samples=nb) + [
            pl.BlockSpec((2, ROW_TILE, ROW_TILE), lambda bi, i: (0, 0, 0)),
            pl.BlockSpec((3, width), lambda bi, i: (0, 0)),
            pl.BlockSpec((1, width), lambda bi, i: (0, 0)),
            pl.BlockSpec((1, HY_WIDTH), lambda bi, i: (0, 0))],
        out_specs=[ospec, ospec, ospec],
        out_shape=[out, out, out],
        compiler_params=_cparams(2, 32),
        name="hyena_pre",
    )(p_rest, p_rest, p_rest, _shift_matrices(ROW_TILE), conv_w, conv_b.reshape(1, width),
      d_bias.reshape(1, HY_WIDTH))


def _hyfilt_kernel(z_ref, t_ref, dl_ref, w1_ref, b1_ref, wi_ref, bi_ref, fr_ref, wl_ref, o_ref):
    fr = fr_ref[...]
    h = jnp.sin(fr * (_dot3(z_ref[...], w1_ref[...]) + b1_ref[...]))
    for j in range(HY_INNER):
        h = jnp.sin(fr * (_dot3(h, wi_ref[j]) + bi_ref[j]))
    h = _dot3(h, wl_ref[...])
    decay = jnp.exp(-t_ref[...] * dl_ref[...])
    hf = h[:, :HY_WIDTH] * decay
    hb = h[:, HY_WIDTH:] * decay
    o_ref[...] = jnp.concatenate([hf + hb, hf - hb], axis=1)


def _pad2(a, rows, cols):
    return jnp.pad(a, ((0, rows - a.shape[0]), (0, cols - a.shape[1])))


def _hyena_filter_sums(n, w1, b1, wi, bi, freq, w_last):
    t = jnp.linspace(0.0, 1.0, n, dtype=F32)[:, None]
    w = 2.0 * math.pi * jnp.arange(n, dtype=F32)[:, None] / n
    f = jnp.linspace(1e-4, HY_BANDS - 1, HY_BANDS, dtype=F32)[None, :]
    z = jnp.concatenate([t, jnp.cos(f * w), -jnp.sin(f * w)], axis=-1)
    max_decay = math.log(HY_TARGET) / HY_FAST_DECAY
    min_decay = math.log(HY_TARGET) / HY_SLOW_DECAY
    deltas = jnp.abs(jnp.linspace(min_decay, max_decay, HY_WIDTH, dtype=F32))[None, :]
    zp = _pad2(z, n, LANE)
    w1p = _pad2(w1, LANE, LANE)
    b1p = _pad2(b1[None, :], 1, LANE)
    wip = jnp.stack([_pad2(wi[j], LANE, LANE) for j in range(HY_INNER)])
    bip = jnp.stack([_pad2(bi[j][None, :], 1, LANE) for j in range(HY_INNER)])
    frp = _pad2(freq[None, :], 1, LANE)
    wlp = _pad2(w_last, LANE, 2 * HY_WIDTH)
    tr = min(n, ROW_TILE)
    full = lambda shape: pl.BlockSpec(shape, lambda i: (0,) * len(shape))
    return pl.pallas_call(
        _hyfilt_kernel,
        grid=(n // tr,),
        in_specs=[pl.BlockSpec((tr, LANE), lambda i: (i, 0)),
                  pl.BlockSpec((tr, 1), lambda i: (i, 0)),
                  full((1, HY_WIDTH)), full((LANE, LANE)), full((1, LANE)),
                  full((HY_INNER, LANE, LANE)), full((HY_INNER, 1, LANE)), full((1, LANE)),
                  full((LANE, 2 * HY_WIDTH))],
        out_specs=pl.BlockSpec((tr, 2 * HY_WIDTH), lambda i: (i, 0)),
        out_shape=jax.ShapeDtypeStruct((n, 2 * HY_WIDTH), F32),
        compiler_params=_cparams(1, 32),
        name="hyena_filter",
    )(zp, t, deltas, w1p, b1p, wip, bip, frp, wlp)


def _dft_tables(n):
    assert n % DFT_T_LO == 0
    f = jnp.arange(n, dtype=jnp.int32)[:, None]
    t_hi = jnp.arange(n // DFT_T_LO, dtype=jnp.int32)[None, :] * DFT_T_LO
    t_lo = jnp.arange(DFT_T_LO, dtype=jnp.int32)[None, :]
    a = ((f * t_hi) % (2 * n)).astype(F32) * (math.pi / n)
    b = ((f * t_lo) % (2 * n)).astype(F32) * (math.pi / n)
    ca, sa = jnp.cos(a)[:, :, None], jnp.sin(a)[:, :, None]
    cb, sb = jnp.cos(b)[:, None, :], jnp.sin(b)[:, None, :]
    cos_t = (ca * cb - sa * sb).reshape(n, n)
    sin_t = (sa * cb + ca * sb).reshape(n, n)
    t = jnp.arange(n, dtype=jnp.int32)[None, :]
    nyq = jnp.where(t % 2 == 0, 1.0, -1.0).astype(F32)
    return jnp.stack([cos_t, jnp.where(f == 0, nyq, sin_t)])


def _hyfwd_kernel(u_ref, f_ref, co_ref, o_ref):
    u = u_ref[0]
    ure = _dot(f_ref[0], u)
    uim = _dot(f_ref[1], u)
    o_ref[0, 0] = (ure * co_ref[0] - uim * co_ref[1]).astype(BF16)
    o_ref[0, 1] = (ure * co_ref[2] + uim * co_ref[3]).astype(BF16)


def _hyinv_kernel(y_ref, ft_ref, ud_ref, x0_ref, o_ref):
    y = _dot(ft_ref[0], y_ref[0, 0]) + _dot(ft_ref[1], y_ref[0, 1])
    o_ref[0] = ((y + ud_ref[0].astype(F32)) * x0_ref[0].astype(F32)).astype(BF16)


def _hyena_conv(u, ud, x0, tables, coef):
    b, n, c = u.shape
    tf = min(n, DFT_ROWS)
    f_bf = tables.astype(BF16)
    ft_bf = jnp.swapaxes(tables, 1, 2).astype(BF16)
    spec = pl.pallas_call(
        _hyfwd_kernel,
        grid=(n // tf, b),
        in_specs=[pl.BlockSpec((1, n, c), lambda j, bi: (bi, 0, 0)),
                  pl.BlockSpec((2, tf, n), lambda j, bi: (0, j, 0)),
                  pl.BlockSpec((4, tf, c), lambda j, bi: (0, j, 0))],
        out_specs=pl.BlockSpec((1, 2, tf, c), lambda j, bi: (bi, 0, j, 0)),
        out_shape=jax.ShapeDtypeStruct((b, 2, n, c), BF16),
        compiler_params=_cparams(2, 56),
        name="hyena_dft",
    )(u, f_bf, coef)
    return pl.pallas_call(
        _hyinv_kernel,
        grid=(b, n // tf),
        in_specs=[pl.BlockSpec((1, 2, n, c), lambda bi, j: (bi, 0, 0, 0)),
                  pl.BlockSpec((2, tf, n), lambda bi, j: (0, j, 0)),
                  pl.BlockSpec((1, tf, c), lambda bi, j: (bi, j, 0)),
                  pl.BlockSpec((1, tf, c), lambda bi, j: (bi, j, 0))],
        out_specs=pl.BlockSpec((1, tf, c), lambda bi, j: (bi, j, 0)),
        out_shape=jax.ShapeDtypeStruct((b, n, c), BF16),
        compiler_params=_cparams(2, 40),
        name="hyena_idft",
    )(spec, ft_bf, ud, x0)


def _hyena_coef(tables, hsum_hdiff):
    n = tables.shape[1]
    c = HY_WIDTH
    r = _matmul_f32(tables.reshape(2 * n, n), hsum_hdiff, min(n, ROW_TILE), "hyena_kernel_dft")
    k_re = r[:n, :c]
    k_im = r[n:, c:]
    k_nyq = r[n:n + 1, :c]
    first = (jnp.arange(n) == 0)[:, None]
    scale = jnp.where(first, 1.0 / (2 * n), 2.0 / (2 * n)).astype(F32)
    zero = jnp.zeros_like(k_im)
    return jnp.stack([k_re * scale,
                      jnp.where(first, zero, k_im * scale),
                      jnp.where(first, zero, k_im * scale),
                      jnp.where(first, k_nyq, k_re) * scale])


def _merge_kernel(*refs, n_ctx_tiles, row_off, has_ctx):
    if has_ctx:
        (of_ref, ob_ref, zg_ref, att_ref, cx_ref, cc_ref, gate_ref, x_ref, m_ref, ghg_ref, gpost_ref,
         gffn_ref, woa_ref, wob_ref, woc_ref, wout_ref, o_ref, h_ref) = refs
    else:
        (of_ref, ob_ref, zg_ref, att_ref, cx_ref, gate_ref, x_ref, m_ref, ghg_ref, gpost_ref,
         gffn_ref, woa_ref, wob_ref, woc_ref, wout_ref, o_ref, h_ref) = refs
    nb, rows, d = x_ref.shape
    flat = lambda ref: ref[...].reshape(nb * rows, ref.shape[-1])
    o = flat(of_ref).astype(F32) + flat(ob_ref).astype(F32)
    ghg = ghg_ref[...]
    a = jnp.concatenate([_rms(o[:, h * HG_DIM:(h + 1) * HG_DIM], ghg) for h in range(HG_HEADS)], axis=1)
    zg = flat(zg_ref)
    a = a * (zg * _sigmoid(zg))
    c = flat(cx_ref)
    if has_ctx:
        c = jnp.where(pl.program_id(1) + row_off < n_ctx_tiles, flat(cc_ref), c)
    ya = _dot(a.astype(BF16), woa_ref[...])
    yb = _dot(flat(att_ref), wob_ref[...])
    yc = _dot(c, woc_ref[...])
    gates = flat(gate_ref)
    m = (_sigmoid(gates[:, 0:d].astype(F32)) * ya
         + _sigmoid(gates[:, d:2 * d].astype(F32)) * yb
         + _sigmoid(gates[:, 2 * d:3 * d].astype(F32)) * yc)
    y = _rms(_dot(m.astype(BF16), wout_ref[...]), gpost_ref[...])
    for b in range(nb):
        x_new = x_ref[b] + m_ref[b, 2:3, :] * y[b * rows:(b + 1) * rows]
        o_ref[b] = x_new
        h_ref[b] = (_rms(x_new, gffn_ref[...]) * (1.0 + m_ref[b, 4:5, :]) + m_ref[b, 3:4, :]).astype(BF16)


def _merge(o_f, o_b, p_hg, att, c_x, c_c, p_rest, xs, mods, g_hg, g_post, g_ffn, w_oa, w_ob, w_oc, w_out,
           row_off, n_ctx_rows):
    b, s, d = xs.shape
    nct = n_ctx_rows // ROW_TILE
    n_tiles = s // ROW_TILE - row_off
    has_ctx = c_c is not None
    nb = MERGE_SAMPLES if b % MERGE_SAMPLES == 0 else 1
    ctx_blk = b // nb

    def stream(width, col=0):
        return pl.BlockSpec((nb, ROW_TILE, width), lambda bi, i: (bi, i + row_off, col))

    def full(shape):
        return pl.BlockSpec(shape, lambda bi, i: (0,) * len(shape))

    in_specs = [stream(HG_WIDTH), stream(HG_WIDTH), stream(HG_WIDTH, 4), stream(AT_WIDTH),
                pl.BlockSpec((nb, ROW_TILE, HY_WIDTH),
                             lambda bi, i: (bi, jnp.maximum(i + row_off - nct, 0), 0))]
    args = [o_f, o_b, p_hg, att, c_x]
    if has_ctx:
        in_specs.append(pl.BlockSpec((nb, ROW_TILE, HY_WIDTH),
                                     lambda bi, i: (bi, jnp.minimum(i + row_off, nct - 1), 0)))
        args.append(c_c)
    in_specs += [stream(3 * d), stream(d),
                 pl.BlockSpec((nb, 6, d), lambda bi, i: (jnp.where(i + row_off < nct, ctx_blk, bi), 0, 0)),
                 full((1, HG_DIM)), full((1, d)), full((1, d)),
                 full((HG_WIDTH, d)), full((AT_WIDTH, d)), full((HY_WIDTH, d)), full((d, d))]
    args += [p_rest, xs, mods, g_hg.reshape(1, HG_DIM), g_post.reshape(1, d), g_ffn.reshape(1, d),
             w_oa, w_ob, w_oc, w_out]
    ospec = pl.BlockSpec((nb, ROW_TILE, d), lambda bi, i: (bi, i, 0))
    return pl.pallas_call(
        functools.partial(_merge_kernel, n_ctx_tiles=nct, row_off=row_off, has_ctx=has_ctx),
        grid=(b // nb, n_tiles),
        in_specs=in_specs,
        out_specs=[ospec, ospec],
        out_shape=[jax.ShapeDtypeStruct((b, n_tiles * ROW_TILE, d), F32),
                   jax.ShapeDtypeStruct((b, n_tiles * ROW_TILE, d), BF16)],
        compiler_params=_cparams(2, 48),
        name="merge",
    )(*args)


FFN_COLS = 256
FFN_DOWN_GROUPS = 2


def _ffn_kernel(*refs, first_tiles, last_tiles, has_next):
    if has_next:
        (h_ref, hp_ref, hn_ref, wu_ref, w_ref, b_ref, x_ref, m_ref, g_ref, wd_ref, mn_ref, gn_ref,
         o_ref, hx_ref, act_ref) = refs
    else:
        h_ref, hp_ref, hn_ref, wu_ref, w_ref, b_ref, x_ref, m_ref, g_ref, wd_ref, o_ref, act_ref = refs
    i = pl.program_id(1)
    is_first = functools.reduce(jnp.logical_or, [i == t for t in first_tiles])
    is_last = functools.reduce(jnp.logical_or, [i == t for t in last_tiles])
    d_ff = wd_ref.shape[0]
    nb, rows, _ = h_ref.shape
    halo = BF16_SUBLANES
    ext = rows + 2 * halo
    pieces = []
    for b in range(nb):
        pieces += [jnp.where(is_first, jnp.zeros_like(hp_ref[b]), hp_ref[b]), h_ref[b],
                   jnp.where(is_last, jnp.zeros_like(hn_ref[b]), hn_ref[b])]
    h_ext = jnp.concatenate(pieces, axis=0)

    def up(j):
        return [_dot(h_ext, wu_ref[:, base + j * FFN_COLS:base + (j + 1) * FFN_COLS]) for base in (0, d_ff)]

    def conv(u, cols):
        w = w_ref[:, cols]
        full = pltpu.roll(u, 1, axis=0) * w[0:1] + u * w[1:2] + pltpu.roll(u, nb * ext - 1, axis=0) * w[2:3]
        kept = [full[b * ext + halo:b * ext + halo + rows] for b in range(nb)]
        return (kept[0] if nb == 1 else jnp.concatenate(kept, axis=0)) + b_ref[:, cols]

    n_chunks = d_ff // FFN_COLS
    per_group = -(-n_chunks // FFN_DOWN_GROUPS)
    acc = None
    u_next = up(0)
    for j in range(n_chunks):
        u = u_next
        if j + 1 < n_chunks:
            u_next = up(j + 1)
        a = conv(u[0], slice(j * FFN_COLS, (j + 1) * FFN_COLS))
        g = conv(u[1], slice(d_ff + j * FFN_COLS, d_ff + (j + 1) * FFN_COLS))
        act_ref[:, j * FFN_COLS:(j + 1) * FFN_COLS] = (a * _sigmoid(a) * g).astype(BF16)
        if (j + 1) % per_group == 0 or j + 1 == n_chunks:
            lo = (j // per_group) * per_group * FFN_COLS
            part = _dot(act_ref[:, lo:(j + 1) * FFN_COLS], wd_ref[lo:(j + 1) * FFN_COLS, :])
            acc = part if acc is None else acc + part
    y = _rms(acc, g_ref[...])
    for b in range(nb):
        x_new = x_ref[b] + m_ref[b, 5:6, :] * y[b * rows:(b + 1) * rows]
        o_ref[b] = x_new
        if has_next:
            hx_ref[b] = (_rms(x_new, gn_ref[...]) * (1.0 + mn_ref[b, 1:2, :])
                         + mn_ref[b, 0:1, :]).astype(BF16)


def _ffn(h, xs, mods, w_up, conv_w, conv_b, g_post, w_down, n_ctx_rows, mods_next=None, g_next=None):
    b, s, d = xs.shape
    d_ff = w_down.shape[0]
    rows = FFN_ROWS if (n_ctx_rows % FFN_ROWS == 0 and s % FFN_ROWS == 0) else ROW_TILE
    nt = s // rows
    nct = n_ctx_rows // rows
    first_tiles = tuple(sorted({0, nct}))
    last_tiles = tuple(sorted({nct - 1, nt - 1} - {-1}))
    nb = FFN_ROWS // rows if b % (FFN_ROWS // rows) == 0 and FFN_ROWS // rows <= MERGE_SAMPLES else 1
    ctx_blk = b // nb
    has_next = mods_next is not None
    full = lambda shape: pl.BlockSpec(shape, lambda bi, i: (0,) * len(shape))
    mspec = pl.BlockSpec((nb, 6, d), lambda bi, i: (jnp.where(i < nct, ctx_blk, bi), 0, 0))
    ospec = pl.BlockSpec((nb, rows, d), lambda bi, i: (bi, i, 0))
    resident = lambda shape: pl.BlockSpec(shape, lambda bi, i: (0,) * len(shape), pipeline_mode=pl.Buffered(1))
    in_specs = _halo_specs(d, 0, 0, s, rows, nb) + [
        resident((d, 2 * d_ff)), full((3, 2 * d_ff)), full((1, 2 * d_ff)),
        ospec, mspec, full((1, d)), resident((d_ff, d))]
    args = [h, h, h, w_up, conv_w, conv_b.reshape(1, 2 * d_ff), xs, mods, g_post.reshape(1, d), w_down]
    out_specs = [ospec]
    out_shape = [jax.ShapeDtypeStruct((b, s, d), F32)]
    if has_next:
        in_specs += [mspec, full((1, d))]
        args += [mods_next, g_next.reshape(1, d)]
        out_specs.append(ospec)
        out_shape.append(jax.ShapeDtypeStruct((b, s, d), BF16))
    return pl.pallas_call(
        functools.partial(_ffn_kernel, first_tiles=first_tiles, last_tiles=last_tiles, has_next=has_next),
        grid=(b // nb, nt),
        in_specs=in_specs,
        out_specs=out_specs,
        out_shape=out_shape,
        scratch_shapes=[pltpu.VMEM((nb * rows, d_ff), BF16)],
        compiler_params=_cparams(2, 56),
        name="ffn",
    )(*args)


def _deinterleave():
    return np.concatenate([np.arange(0, AT_DIM, 2), np.arange(1, AT_DIM, 2)])


def _q_head_order():
    return [h for j in range(AT_GROUP) for h in (j, AT_GROUP + j)]


def _largest_tile(n, cap):
    best = LANE
    for t in range(LANE, cap + 1, LANE):
        if n % t == 0:
            best = t
    return best


def kernel(x, c, ctx, c_ctx, w_ada, b_ada, g_pre_mix, g_post_mix, g_pre_ffn, g_post_ffn, w_in, hg_lower_bounds, hg_norm, q_norm, k_norm, hy_conv_w, hy_conv_b, hy_w1, hy_b1, hy_wi, hy_bi, hy_freq, hy_w_last, hy_bias, w_oa, w_ob, w_oc, w_out, w_up, ffn_conv_w, ffn_conv_b, w_down):
    bsz, n_lat, d = x.shape
    n_ctx = ctx.shape[1]
    depth = w_ada.shape[0]
    d_ff = w_down.shape[1]
    assert AT_KV_HEADS == 2 and AT_GROUP * LANE == AT_WIDTH and AT_KV_WIDTH == LANE
    assert n_ctx % ROW_TILE == 0 and n_lat % ROW_TILE == 0 and n_lat % GRID_W == 0
    assert (bsz * (n_ctx + n_lat)) % MM_ROWS == 0 and (bsz * n_lat) % MM_ROWS == 0

    lbp = jax.nn.softmax(hg_lower_bounds.astype(F32), axis=0)
    lower = jnp.cumsum(lbp, axis=0) - lbp[0]

    rp = -(-(bsz + MERGE_SAMPLES) // 8) * 8
    src = jnp.concatenate([c, jnp.tile(c_ctx[None, :], (MERGE_SAMPLES, 1)),
                           jnp.zeros((rp - bsz - MERGE_SAMPLES, d), F32)], axis=0)
    mods_all = _ada(src, w_ada, b_ada).reshape(depth, rp, 6, d)

    o_q = 5 * HG_WIDTH
    o_k = o_q + AT_WIDTH
    o_v = o_k + AT_KV_WIDTH
    o_hy = o_v + AT_KV_WIDTH
    o_gate = o_hy + 3 * HY_WIDTH
    deint = _deinterleave()
    q_cols = np.concatenate([o_q + h * AT_DIM + deint for h in _q_head_order()])
    k_cols = np.concatenate([o_k + g * AT_DIM + deint for g in range(AT_KV_HEADS)])
    qk_cols = np.concatenate([q_cols, k_cols])
    col_hy = (3 * d) // (3 * HY_WIDTH)
    col_q = (3 * d + 3 * HY_WIDTH) // AT_WIDTH
    col_k = (3 * d + 3 * HY_WIDTH + AT_WIDTH) // AT_KV_WIDTH
    col_v = col_k + 1
    assert (3 * d) % (3 * HY_WIDTH) == 0 and (3 * d + 3 * HY_WIDTH) % AT_WIDTH == 0
    ob_rows = np.concatenate([np.arange(h * AT_DIM, (h + 1) * AT_DIM) for h in _q_head_order()])

    rope_tabs = _rope_tables(n_ctx, n_lat)
    dft_lat = _dft_tables(n_lat)
    dft_ctx = _dft_tables(n_ctx)
    nct = n_ctx // ROW_TILE

    s_all = n_ctx + n_lat
    xs, h = _join_modulate(ctx, x, mods_all[0], g_pre_mix[0], 0, 1)
    for l in range(depth):
        need_ctx = l < depth - 1
        mods = mods_all[l]
        w_hg = w_in[l][:, :5 * HG_WIDTH].astype(BF16)
        w_rest = jnp.concatenate([w_in[l][:, o_gate:o_gate + 3 * d], w_in[l][:, o_hy:o_hy + 3 * HY_WIDTH],
                                  w_in[l][:, qk_cols], w_in[l][:, o_v:o_v + AT_KV_WIDTH]], axis=1).astype(BF16)

        h = h.reshape(bsz * s_all, d)
        p_hg = _matmul(h, w_hg, F32, _largest_tile(5 * HG_WIDTH, 2560), "proj_hgrn").reshape(bsz, s_all, -1)
        p_rest = _matmul(h, w_rest, BF16, _largest_tile(w_rest.shape[1], 1792), "proj_rest").reshape(bsz, s_all, -1)

        o_f, o_b = _hgrn(p_hg, lower[l], n_ctx)

        gq = jnp.tile(q_norm[l][deint], AT_HEADS)[None, :]
        gk = jnp.tile(k_norm[l][deint], AT_KV_HEADS)[None, :]
        row_off = 0 if need_ctx else nct
        att = _attention(p_rest, col_q, col_k, col_v, rope_tabs, gq, gk, row_off, n_ctx)

        filt_args = (hy_w1[l], hy_b1[l], hy_wi[l], hy_bi[l], hy_freq[l], hy_w_last[l])
        coef = _hyena_coef(dft_lat, _hyena_filter_sums(n_lat, *filt_args))
        c_x = _hyena_conv(*_hyena_pre(p_rest, col_hy, nct, n_lat, hy_conv_w[l], hy_conv_b[l], hy_bias[l]),
                          dft_lat, coef)
        c_c = None
        if need_ctx:
            coef_c = _hyena_coef(dft_ctx, _hyena_filter_sums(n_ctx, *filt_args))
            c_c = _hyena_conv(*_hyena_pre(p_rest, col_hy, 0, n_ctx, hy_conv_w[l], hy_conv_b[l], hy_bias[l]),
                              dft_ctx, coef_c)

        xs, h2 = _merge(o_f, o_b, p_hg, att, c_x, c_c, p_rest, xs, mods, hg_norm[l], g_post_mix[l],
                        g_pre_ffn[l], w_oa[l].astype(BF16), w_ob[l][ob_rows].astype(BF16),
                        w_oc[l].astype(BF16), w_out[l].astype(BF16), row_off, n_ctx)
        n_ctx_now = n_ctx if need_ctx else 0
        ffn_args = (h2, xs, mods, w_up[l].astype(BF16), ffn_conv_w[l], ffn_conv_b[l], g_post_ffn[l],
                    w_down[l].astype(BF16), n_ctx_now)
        if need_ctx:
            xs, h = _ffn(*ffn_args, mods_all[l + 1], g_pre_mix[l + 1])
        else:
            xs, = _ffn(*ffn_args)
    return xs
```

```python
import functools
import math

import jax
import jax.numpy as jnp
import numpy as np
from jax import lax
from jax.experimental import pallas as pl
from jax.experimental.pallas import tpu as pltpu

F32 = jnp.float32
BF16 = jnp.bfloat16

NORM_EPS = 1e-6
GRID_W = 64
HG_HEADS = 4
HG_DIM = 128
HG_WIDTH = HG_HEADS * HG_DIM
HG_EXP_CLIP = 30.0
AT_HEADS = 8
AT_KV_HEADS = 2
AT_DIM = 64
AT_GROUP = AT_HEADS // AT_KV_HEADS
AT_WIDTH = AT_HEADS * AT_DIM
AT_KV_WIDTH = AT_KV_HEADS * AT_DIM
ROPE_THETA = 10000.0
HY_WIDTH = 512
HY_EMB_DIM = 33
HY_BANDS = (HY_EMB_DIM - 1) // 2
HY_FILTER_WIDTH = 64
HY_INNER = 2
HY_FAST_DECAY = 0.3
HY_SLOW_DECAY = 1.5
HY_TARGET = 1e-2

LANE = 128
BF16_SUBLANES = 16
ROW_TILE = 256
MERGE_SAMPLES = 2
HG_CHUNK = 128
HG_BLOCK = 256
HG_LEVELS = tuple(HG_CHUNK >> (j + 1) for j in range(int(math.log2(HG_CHUNK))))
MM_ROWS = 1024
DFT_ROWS = 1024
DFT_T_LO = 64
AT_LOOKAHEAD = 2
FFN_ROWS = 512
VMEM_CAP = 56 * 1024 * 1024


def _cparams(n_axes, vmem_mb):
    return pltpu.CompilerParams(
        dimension_semantics=("arbitrary",) * n_axes,
        vmem_limit_bytes=min(int(vmem_mb) * 1024 * 1024, VMEM_CAP))


def _dot(a, b):
    return jnp.dot(a, b, preferred_element_type=F32)


def _dot_nt(a, b):
    return lax.dot_general(a, b, (((1,), (1,)), ((), ())), preferred_element_type=F32)


def _split_bf16(a):
    hi = a.astype(BF16)
    lo = (a - hi.astype(F32)).astype(BF16)
    return hi, lo


def _dot3(a, b):
    ah, al = _split_bf16(a)
    bh, bl = _split_bf16(b)
    return _dot(ah, bh) + (_dot(ah, bl) + _dot(al, bh))


def _rms(x, g):
    return x * lax.rsqrt(jnp.mean(x * x, axis=-1, keepdims=True) + NORM_EPS) * g


def _sigmoid(x):
    return 0.5 * jnp.tanh(0.5 * x) + 0.5


def _ada_kernel(src_ref, w_ref, b_ref, o_ref):
    s = src_ref[...]
    s = s * _sigmoid(s)
    o_ref[0] = _dot3(s, w_ref[0]) + b_ref[0]


def _ada(src, w_ada, b_ada):
    depth, d, d6 = w_ada.shape
    rp = src.shape[0]
    tn = d
    return pl.pallas_call(
        _ada_kernel,
        grid=(depth, d6 // tn),
        in_specs=[pl.BlockSpec((rp, d), lambda l, j: (0, 0)),
                  pl.BlockSpec((1, d, tn), lambda l, j: (l, 0, j)),
                  pl.BlockSpec((1, 1, tn), lambda l, j: (l, 0, j))],
        out_specs=pl.BlockSpec((1, rp, tn), lambda l, j: (l, 0, j)),
        out_shape=jax.ShapeDtypeStruct((depth, rp, d6), F32),
        compiler_params=_cparams(2, 32),
        name="ada",
    )(src, w_ada, b_ada.reshape(depth, 1, d6))


def _mod_kernel(c_ref, x_ref, m_ref, g_ref, xs_ref, o_ref, *, k_shift, k_scale, n_ctx_tiles):
    for b in range(x_ref.shape[0]):
        x = jnp.where(pl.program_id(1) < n_ctx_tiles, c_ref[b], x_ref[b])
        xs_ref[b] = x
        shift = m_ref[b, k_shift:k_shift + 1, :]
        scale = m_ref[b, k_scale:k_scale + 1, :]
        o_ref[b] = (_rms(x, g_ref[...]) * (1.0 + scale) + shift).astype(o_ref.dtype)


def _join_modulate(ctx, x, mods, g, k_shift, k_scale):
    b, n_lat, d = x.shape
    n_ctx_tiles = ctx.shape[1] // ROW_TILE
    nt = n_ctx_tiles + n_lat // ROW_TILE
    nb = MERGE_SAMPLES if b % MERGE_SAMPLES == 0 else 1
    ctx_blk = b // nb
    ospec = pl.BlockSpec((nb, ROW_TILE, d), lambda bi, i: (bi, i, 0))
    return pl.pallas_call(
        functools.partial(_mod_kernel, k_shift=k_shift, k_scale=k_scale, n_ctx_tiles=n_ctx_tiles),
        grid=(b // nb, nt),
        in_specs=[pl.BlockSpec((nb, ROW_TILE, d), lambda bi, i: (bi, jnp.minimum(i, n_ctx_tiles - 1), 0)),
                  pl.BlockSpec((nb, ROW_TILE, d), lambda bi, i: (bi, jnp.maximum(i - n_ctx_tiles, 0), 0)),
                  pl.BlockSpec((nb, 6, d), lambda bi, i: (jnp.where(i < n_ctx_tiles, ctx_blk, bi), 0, 0)),
                  pl.BlockSpec((1, d), lambda bi, i: (0, 0))],
        out_specs=[ospec, ospec],
        out_shape=[jax.ShapeDtypeStruct((b, nt * ROW_TILE, d), F32),
                   jax.ShapeDtypeStruct((b, nt * ROW_TILE, d), BF16)],
        compiler_params=_cparams(2, 16),
        name="modulate",
    )(ctx, x, mods, g.reshape(1, d))


def _mm_kernel(a_ref, b_ref, o_ref):
    o_ref[...] = _dot(a_ref[...], b_ref[...]).astype(o_ref.dtype)


def _matmul(a, w, out_dtype, tn, name):
    m, k = a.shape
    n = w.shape[1]
    tm = min(MM_ROWS, m)
    assert m % tm == 0 and n % tn == 0
    return pl.pallas_call(
        _mm_kernel,
        grid=(m // tm, n // tn),
        in_specs=[pl.BlockSpec((tm, k), lambda i, j: (i, 0)),
                  pl.BlockSpec((k, tn), lambda i, j: (0, j))],
        out_specs=pl.BlockSpec((tm, tn), lambda i, j: (i, j)),
        out_shape=jax.ShapeDtypeStruct((m, n), out_dtype),
        compiler_params=_cparams(2, 56),
        name=name,
    )(a, w)


def _hg_scan_matrix(reverse):
    t_n = HG_CHUNK
    t = np.arange(t_n)[:, None]
    u = np.arange(t_n)[None, :]
    rows = [(u >= t) if reverse else (u <= t)]
    for w in HG_LEVELS:
        base = (t // (2 * w)) * (2 * w)
        mid = base + w
        upper = (t - base) >= w
        if reverse:
            m = np.where(upper, (u >= mid) & (u < t), (u >= t) & (u < mid))
        else:
            m = np.where(upper, (u >= mid) & (u <= t), (u > t) & (u < mid))
        rows.append(m)
    m = np.concatenate(rows, axis=0).astype(np.float32)
    return np.concatenate([m, m], axis=1)


def _hgrn_kernel(qf_ref, zf_ref, vf_ref, qb_ref, zb_ref, vb_ref, lb_ref, pf_ref, pb_ref,
                 of_ref, ob_ref, s_ref):
    @pl.when(pl.program_id(1) == 0)
    def _():
        s_ref[...] = jnp.zeros_like(s_ref)

    t_n = HG_CHUNK
    ti = lax.broadcasted_iota(jnp.int32, (t_n, t_n), 0)
    si = lax.broadcasted_iota(jnp.int32, (t_n, t_n), 1)
    tx = ti ^ si
    dirs = ((qf_ref, zf_ref, vf_ref, pf_ref, of_ref), (qb_ref, zb_ref, vb_ref, pb_ref, ob_ref))
    masks = []
    for reverse in (False, True):
        later = (ti < si) if reverse else (ti > si)
        masks.append([later & (tx >= w) & (tx < 2 * w) for w in HG_LEVELS])
    n_sub = qf_ref.shape[1] // t_n
    heads = range(HG_HEADS)
    units = [(d, c, h) for d in range(2) for c in range(n_sub) for h in heads]
    cols = lambda h: slice(h * HG_DIM, (h + 1) * HG_DIM)
    rows = lambda c: slice(c * t_n, (c + 1) * t_n)

    kk, kb, qb, cat, x, ex = {}, {}, {}, {}, {}, {}
    a = {u: jnp.zeros((t_n, t_n), F32) for u in units}

    def gates(d, c, h):
        z = dirs[d][1][0, rows(c), cols(h)]
        lb = lb_ref[d:d + 1, cols(h)]
        e = jnp.exp(-jnp.abs(z))
        log_num = jnp.log(jnp.where(z >= 0.0, 1.0 + lb * e, e + lb))
        log_clip = z + jnp.log(1.0 + lb * math.exp(HG_EXP_CLIP))
        lf = jnp.where(z < -HG_EXP_CLIP, log_clip, log_num) - jnp.log(1.0 + e)
        half = 0.5 * (1.0 - lb)
        kk[d, c, h] = half - half * jnp.tanh(0.5 * z)
        kb[d, c, h] = kk[d, c, h].astype(BF16)
        qb[d, c, h] = dirs[d][0][0, rows(c), cols(h)].astype(BF16)
        hi, lo = _split_bf16(lf)
        cat[d, c, h] = jnp.concatenate([hi, lo], axis=0)

    def exponents(d):
        rhs = jnp.concatenate([cat[d, c, h] for c in range(n_sub) for h in heads], axis=1)
        xd = _dot(dirs[d][3][...], rhs)
        ed = jnp.exp(xd[t_n:]).astype(BF16)
        for c in range(n_sub):
            for h in heads:
                x[d, c, h] = xd[0:t_n, cols(c * HG_HEADS + h)]
                ex[d, c, h] = ed[:, cols(c * HG_HEADS + h)]

    def level(d, c, j):
        for h in heads:
            u = (d, c, h)
            ew = ex[u][j * t_n:(j + 1) * t_n]
            pw = _dot_nt(qb[u] * ew, kb[u] * ew)
            a[u] = jnp.where(masks[d][j], pw, a[u])

    def finish(d, c, h):
        u = (d, c, h)
        q = dirs[d][0][0, rows(c), cols(h)]
        v = dirs[d][2][0, rows(c), cols(h)]
        g = x[u]
        g_last = g[0:1] if d == 1 else g[t_n - 1:t_n]
        st = s_ref[d * HG_HEADS + h]
        dqk = jnp.sum(q * kk[u], axis=1, keepdims=True)
        o = (_dot(a[u].astype(BF16), v.astype(BF16)) + dqk * v
             + _dot_nt((q * jnp.exp(g)).astype(BF16), st.astype(BF16)))
        kd = (kk[u] * jnp.exp(g_last - g)).astype(BF16)
        dirs[d][4][0, rows(c), cols(h)] = o.astype(BF16)
        s_ref[d * HG_HEADS + h] = st * jnp.exp(g_last) + _dot(v.T.astype(BF16), kd)

    for d in range(2):
        for c in range(n_sub):
            for h in heads:
                gates(d, c, h)
        exponents(d)
        for j in range(len(HG_LEVELS)):
            for c in range(n_sub):
                level(d, c, j)
        for c in (range(n_sub) if d == 0 else reversed(range(n_sub))):
            for h in heads:
                finish(d, c, h)


def _hgrn(p_hg, lb, n_ctx_rows):
    b, s, _ = p_hg.shape
    assert s % HG_BLOCK == 0 and n_ctx_rows % HG_BLOCK == 0
    nb = s // HG_BLOCK
    nc = n_ctx_rows // HG_BLOCK

    def bidx(n):
        return jnp.where(n < nc, nc - 1 - n, nb - 1 - (n - nc))

    blk = (1, HG_BLOCK, HG_WIDTH)
    pf = jnp.asarray(_hg_scan_matrix(False)).astype(BF16)
    pb = jnp.asarray(_hg_scan_matrix(True)).astype(BF16)
    pshape = pf.shape
    return pl.pallas_call(
        _hgrn_kernel,
        grid=(b, nb),
        in_specs=[pl.BlockSpec(blk, lambda bi, n: (bi, n, 0)),
                  pl.BlockSpec(blk, lambda bi, n: (bi, n, 1)),
                  pl.BlockSpec(blk, lambda bi, n: (bi, n, 3)),
                  pl.BlockSpec(blk, lambda bi, n: (bi, bidx(n), 0)),
                  pl.BlockSpec(blk, lambda bi, n: (bi, bidx(n), 2)),
                  pl.BlockSpec(blk, lambda bi, n: (bi, bidx(n), 3)),
                  pl.BlockSpec((2, HG_WIDTH), lambda bi, n: (0, 0)),
                  pl.BlockSpec(pshape, lambda bi, n: (0, 0)),
                  pl.BlockSpec(pshape, lambda bi, n: (0, 0))],
        out_specs=[pl.BlockSpec(blk, lambda bi, n: (bi, n, 0)),
                   pl.BlockSpec(blk, lambda bi, n: (bi, bidx(n), 0))],
        out_shape=[jax.ShapeDtypeStruct((b, s, HG_WIDTH), BF16)] * 2,
        scratch_shapes=[pltpu.VMEM((2 * HG_HEADS, HG_DIM, HG_DIM), F32)],
        compiler_params=_cparams(2, 40),
        name="hgrn",
    )(p_hg, p_hg, p_hg, p_hg, p_hg, p_hg, lb, pf, pb)


def _rope(x, cos, sin_signed, first_half):
    n = x.shape[-1]
    half = AT_DIM // 2
    partner = jnp.where(first_half, pltpu.roll(x, n - half, axis=1), pltpu.roll(x, half, axis=1))
    return x * cos + partner * sin_signed


def _head_norm(x, gain, group_mean):
    ms = _dot((x * x).astype(BF16), group_mean)
    return x * lax.rsqrt(ms + NORM_EPS) * gain


def _attn_kernel(q_ref, k_ref, v_ref, cq_ref, sq_ref, ck_ref, sk_ref, gq_ref, gk_ref, mq_ref, mk_ref,
                 o_ref, k_scr, v_scr, *, q_off, n_ctx_tiles, n_ctx_rows):
    i = pl.program_id(1)

    @pl.when(i == 0)
    def _():
        kr = k_ref[0].astype(F32)
        lane = lax.broadcasted_iota(jnp.int32, kr.shape, 1)
        kn = _head_norm(kr, gk_ref[...], mk_ref[...])
        k_scr[...] = _rope(kn, ck_ref[...], sk_ref[...], (lane % AT_DIM) < AT_DIM // 2).astype(BF16)
        v = v_ref[0]
        one = jnp.ones_like(v)
        v_scr[0] = jnp.where(lane < AT_DIM, v, one)
        v_scr[1] = jnp.where(lane < AT_DIM, one, v)

    qr = q_ref[0].astype(F32)
    lane = lax.broadcasted_iota(jnp.int32, qr.shape, 1)
    qn = _head_norm(qr, gq_ref[...], mq_ref[...])
    qn = _rope(qn, cq_ref[...], sq_ref[...], (lane % AT_DIM) < AT_DIM // 2)
    qn = (qn * (AT_DIM ** -0.5 * math.log2(math.e))).astype(BF16)
    lane_t = lax.broadcasted_iota(jnp.int32, (ROW_TILE, LANE), 1)
    kv0 = lane_t < AT_DIM

    def attend(n_keys):
        keys = k_scr[0:n_keys, :]
        heads = [(j, g) for j in range(AT_WIDTH // LANE) for g in range(AT_KV_HEADS)]

        def scores(j, g):
            qt = qn[:, j * LANE:(j + 1) * LANE]
            return _dot_nt(jnp.where(kv0 if g == 0 else ~kv0, qt, jnp.zeros_like(qt)), keys)

        pending = [scores(*heads[n]) for n in range(AT_LOOKAHEAD)]
        outs = {}
        for n, (j, g) in enumerate(heads):
            s = pending.pop(0)
            if n + AT_LOOKAHEAD < len(heads):
                pending.append(scores(*heads[n + AT_LOOKAHEAD]))
            p = jnp.exp2(s - jnp.max(s, axis=1, keepdims=True))
            outs[g] = _dot(p.astype(BF16), v_scr[g, 0:n_keys, :])
            if g == AT_KV_HEADS - 1:
                num = jnp.where(kv0, outs[0], outs[1])
                den = pltpu.roll(jnp.where(kv0, outs[1], outs[0]), AT_DIM, axis=1)
                o_ref[0, :, j * LANE:(j + 1) * LANE] = (num / den).astype(o_ref.dtype)

    n_all = k_scr.shape[0]
    if q_off < n_ctx_tiles:
        @pl.when(i + q_off < n_ctx_tiles)
        def _():
            attend(n_ctx_rows)

        @pl.when(i + q_off >= n_ctx_tiles)
        def _():
            attend(n_all)
    else:
        attend(n_all)


def _attention(p_rest, col_q, col_k, col_v, tabs, gq, gk, q_off, n_ctx_rows):
    b, s, _ = p_rest.shape
    nt = s // ROW_TILE
    cq, sq, ck, sk = tabs
    mq = jnp.asarray(np.kron(np.eye(AT_HEADS), np.full((AT_DIM, AT_DIM), 1.0 / AT_DIM)), BF16)
    mk = jnp.asarray(np.kron(np.eye(AT_KV_HEADS), np.full((AT_DIM, AT_DIM), 1.0 / AT_DIM)), BF16)
    kern = functools.partial(_attn_kernel, q_off=q_off, n_ctx_tiles=n_ctx_rows // ROW_TILE,
                             n_ctx_rows=n_ctx_rows)
    return pl.pallas_call(
        kern,
        grid=(b, nt - q_off),
        in_specs=[pl.BlockSpec((1, ROW_TILE, AT_WIDTH), lambda bi, i: (bi, i + q_off, col_q)),
                  pl.BlockSpec((1, s, AT_KV_WIDTH), lambda bi, i: (bi, 0, col_k)),
                  pl.BlockSpec((1, s, AT_KV_WIDTH), lambda bi, i: (bi, 0, col_v)),
                  pl.BlockSpec((ROW_TILE, AT_WIDTH), lambda bi, i: (i + q_off, 0)),
                  pl.BlockSpec((ROW_TILE, AT_WIDTH), lambda bi, i: (i + q_off, 0)),
                  pl.BlockSpec((s, AT_KV_WIDTH), lambda bi, i: (0, 0)),
                  pl.BlockSpec((s, AT_KV_WIDTH), lambda bi, i: (0, 0)),
                  pl.BlockSpec((1, AT_WIDTH), lambda bi, i: (0, 0)),
                  pl.BlockSpec((1, AT_KV_WIDTH), lambda bi, i: (0, 0)),
                  pl.BlockSpec((AT_WIDTH, AT_WIDTH), lambda bi, i: (0, 0)),
                  pl.BlockSpec((AT_KV_WIDTH, AT_KV_WIDTH), lambda bi, i: (0, 0))],
        out_specs=pl.BlockSpec((1, ROW_TILE, AT_WIDTH), lambda bi, i: (bi, i + q_off, 0)),
        out_shape=jax.ShapeDtypeStruct((b, s, AT_WIDTH), BF16),
        scratch_shapes=[pltpu.VMEM((s, AT_KV_WIDTH), BF16),
                        pltpu.VMEM((AT_KV_HEADS, s, AT_KV_WIDTH), BF16)],
        compiler_params=_cparams(2, 48),
        name="attention",
    )(p_rest, p_rest, p_rest, cq, sq, ck, sk, gq, gk, mq, mk)


def _rope_tables(n_ctx_rows, n_lat_rows):
    rows = n_lat_rows // GRID_W
    row = jnp.repeat(jnp.arange(rows), GRID_W).astype(F32)
    col = jnp.tile(jnp.arange(GRID_W), rows).astype(F32)
    n_freq = AT_DIM // 4
    inv = ROPE_THETA ** (-jnp.arange(n_freq, dtype=F32) / n_freq)
    ang = jnp.concatenate([row[:, None] * inv, col[:, None] * inv], axis=-1)
    cos = jnp.concatenate([jnp.cos(ang), jnp.cos(ang)], axis=-1)
    sin = jnp.concatenate([-jnp.sin(ang), jnp.sin(ang)], axis=-1)
    cos = jnp.concatenate([jnp.ones((n_ctx_rows, AT_DIM), F32), cos], axis=0)
    sin = jnp.concatenate([jnp.zeros((n_ctx_rows, AT_DIM), F32), sin], axis=0)
    return (jnp.tile(cos, (1, AT_HEADS)), jnp.tile(sin, (1, AT_HEADS)),
            jnp.tile(cos, (1, AT_KV_HEADS)), jnp.tile(sin, (1, AT_KV_HEADS)))


def _shift_matrices(n):
    i = np.arange(n)
    down = i[:, None] - 1 == i[None, :]
    up = i[:, None] + 1 == i[None, :]
    return jnp.asarray(np.stack([down, up]).astype(np.float32), BF16)


def _conv3(xb, shift_ref, prev_row, next_row, w, bias):
    n, c = xb.shape
    sub = 8
    r = lax.broadcasted_iota(jnp.int32, (sub, c), 0)
    x = xb.astype(F32)
    if shift_ref is None:
        xm = pltpu.roll(x, 1, axis=0)
        xp = pltpu.roll(x, n - 1, axis=0)
    else:
        xm = _dot(shift_ref[0], xb)
        xp = _dot(shift_ref[1], xb)
    xm = jnp.concatenate([jnp.where(r == 0, prev_row, xm[0:sub]), xm[sub:]], axis=0)
    xp = jnp.concatenate([xp[:n - sub], jnp.where(r == sub - 1, next_row, xp[n - sub:])], axis=0)
    return xm * w[0:1] + x * w[1:2] + xp * w[2:3] + bias


def _halo_specs(width, col, row_off, n_rows, rows=ROW_TILE, samples=1):
    per = rows // BF16_SUBLANES
    last = n_rows // BF16_SUBLANES - 1
    return [
        pl.BlockSpec((samples, rows, width), lambda bi, i: (bi, i + row_off, col)),
        pl.BlockSpec((samples, BF16_SUBLANES, width),
                     lambda bi, i: (bi, jnp.maximum((i + row_off) * per - 1, 0), col)),
        pl.BlockSpec((samples, BF16_SUBLANES, width),
                     lambda bi, i: (bi, jnp.minimum((i + row_off + 1) * per, last), col)),
    ]


def _halo_rows(prev_ref, next_ref, is_first, is_last, b):
    prev_row = prev_ref[b, BF16_SUBLANES - 1:BF16_SUBLANES, :].astype(F32)
    next_row = next_ref[b, 0:1, :].astype(F32)
    prev_row = jnp.where(is_first, 0.0, prev_row)
    next_row = jnp.where(is_last, 0.0, next_row)
    return prev_row, next_row


HY_PRE_SAMPLES = 4


def _hypre_kernel(z_ref, zp_ref, zn_ref, sh_ref, w_ref, b_ref, db_ref, u_ref, ud_ref, x0_ref):
    i = pl.program_id(1)
    for b in range(z_ref.shape[0]):
        prev_row, next_row = _halo_rows(zp_ref, zn_ref, i == 0, i == pl.num_programs(1) - 1, b)
        zc = _conv3(z_ref[b], sh_ref, prev_row, next_row, w_ref[...], b_ref[...])
        x0 = zc[:, :HY_WIDTH]
        x1 = zc[:, HY_WIDTH:2 * HY_WIDTH]
        v = zc[:, 2 * HY_WIDTH:]
        u = v * x1
        u_ref[b] = u.astype(BF16)
        ud_ref[b] = (u * db_ref[...]).astype(BF16)
        x0_ref[b] = x0.astype(BF16)


def _hyena_pre(p_rest, col, row_off, n_rows, conv_w, conv_b, d_bias):
    b, s, _ = p_rest.shape
    width = 3 * HY_WIDTH
    nb = HY_PRE_SAMPLES if b % HY_PRE_SAMPLES == 0 else 1
    out = jax.ShapeDtypeStruct((b, n_rows, HY_WIDTH), BF16)
    ospec = pl.BlockSpec((nb, ROW_TILE, HY_WIDTH), lambda bi, i: (bi, i, 0))
    return pl.pallas_call(
        _hypre_kernel,
        grid=(b // nb, n_rows // ROW_TILE),
        in_specs=_halo_specs(width, col, row_off, s, samples=nb) + [
            pl.BlockSpec((2, ROW_TILE, ROW_TILE), lambda bi, i: (0, 0, 0)),
            pl.BlockSpec((3, width), lambda bi, i: (0, 0)),
            pl.BlockSpec((1, width), lambda bi, i: (0, 0)),
            pl.BlockSpec((1, HY_WIDTH), lambda bi, i: (0, 0))],
        out_specs=[ospec, ospec, ospec],
        out_shape=[out, out, out],
        compiler_params=_cparams(2, 32),
        name="hyena_pre",
    )(p_rest, p_rest, p_rest, _shift_matrices(ROW_TILE), conv_w, conv_b.reshape(1, width),
      d_bias.reshape(1, HY_WIDTH))


def _hyfilt_kernel(z_ref, t_ref, dl_ref, w1_ref, b1_ref, wi_ref, bi_ref, fr_ref, wl_ref, o_ref):
    fr = fr_ref[...]
    h = jnp.sin(fr * (_dot3(z_ref[...], w1_ref[...]) + b1_ref[...]))
    for j in range(HY_INNER):
        h = jnp.sin(fr * (_dot3(h, wi_ref[j]) + bi_ref[j]))
    h = _dot3(h, wl_ref[...])
    decay = jnp.exp(-t_ref[...] * dl_ref[...])
    hf = h[:, :HY_WIDTH] * decay
    hb = h[:, HY_WIDTH:] * decay
    o_ref[...] = jnp.concatenate([hf + hb, hf - hb], axis=1)


def _pad2(a, rows, cols):
    return jnp.pad(a, ((0, rows - a.shape[0]), (0, cols - a.shape[1])))


def _hyena_filter_sums(n, w1, b1, wi, bi, freq, w_last):
    t = jnp.linspace(0.0, 1.0, n, dtype=F32)[:, None]
    w = 2.0 * math.pi * jnp.arange(n, dtype=F32)[:, None] / n
    f = jnp.linspace(1e-4, HY_BANDS - 1, HY_BANDS, dtype=F32)[None, :]
    z = jnp.concatenate([t, jnp.cos(f * w), -jnp.sin(f * w)], axis=-1)
    max_decay = math.log(HY_TARGET) / HY_FAST_DECAY
    min_decay = math.log(HY_TARGET) / HY_SLOW_DECAY
    deltas = jnp.abs(jnp.linspace(min_decay, max_decay, HY_WIDTH, dtype=F32))[None, :]
    zp = _pad2(z, n, LANE)
    w1p = _pad2(w1, LANE, LANE)
    b1p = _pad2(b1[None, :], 1, LANE)
    wip = jnp.stack([_pad2(wi[j], LANE, LANE) for j in range(HY_INNER)])
    bip = jnp.stack([_pad2(bi[j][None, :], 1, LANE) for j in range(HY_INNER)])
    frp = _pad2(freq[None, :], 1, LANE)
    wlp = _pad2(w_last, LANE, 2 * HY_WIDTH)
    tr = min(n, ROW_TILE)
    full = lambda shape: pl.BlockSpec(shape, lambda i: (0,) * len(shape))
    return pl.pallas_call(
        _hyfilt_kernel,
        grid=(n // tr,),
        in_specs=[pl.BlockSpec((tr, LANE), lambda i: (i, 0)),
                  pl.BlockSpec((tr, 1), lambda i: (i, 0)),
                  full((1, HY_WIDTH)), full((LANE, LANE)), full((1, LANE)),
                  full((HY_INNER, LANE, LANE)), full((HY_INNER, 1, LANE)), full((1, LANE)),
                  full((LANE, 2 * HY_WIDTH))],
        out_specs=pl.BlockSpec((tr, 2 * HY_WIDTH), lambda i: (i, 0)),
        out_shape=jax.ShapeDtypeStruct((n, 2 * HY_WIDTH), F32),
        compiler_params=_cparams(1, 32),
        name="hyena_filter",
    )(zp, t, deltas, w1p, b1p, wip, bip, frp, wlp)


def _dft_tables(n):
    assert n % DFT_T_LO == 0
    f = jnp.arange(n, dtype=jnp.int32)[:, None]
    t_hi = jnp.arange(n // DFT_T_LO, dtype=jnp.int32)[None, :] * DFT_T_LO
    t_lo = jnp.arange(DFT_T_LO, dtype=jnp.int32)[None, :]
    a = ((f * t_hi) % (2 * n)).astype(F32) * (math.pi / n)
    b = ((f * t_lo) % (2 * n)).astype(F32) * (math.pi / n)
    ca, sa = jnp.cos(a)[:, :, None], jnp.sin(a)[:, :, None]
    cb, sb = jnp.cos(b)[:, None, :], jnp.sin(b)[:, None, :]
    cos_t = (ca * cb - sa * sb).reshape(n, n)
    sin_t = (sa * cb + ca * sb).reshape(n, n)
    t = jnp.arange(n, dtype=jnp.int32)[None, :]
    nyq = jnp.where(t % 2 == 0, 1.0, -1.0).astype(F32)
    fwd = jnp.stack([cos_t, jnp.where(f == 0, nyq, sin_t)])
    inv = jnp.stack([cos_t, jnp.where(t == 0, nyq.T, sin_t)])
    return fwd.astype(BF16), inv.astype(BF16)


def _hyfwd_kernel(u_ref, f_ref, co_ref, o_ref):
    u = u_ref[0]
    ure = _dot(f_ref[0], u)
    uim = _dot(f_ref[1], u)
    o_ref[0, 0] = (ure * co_ref[0] - uim * co_ref[1]).astype(BF16)
    o_ref[0, 1] = (ure * co_ref[2] + uim * co_ref[3]).astype(BF16)


def _hyinv_kernel(y_ref, ft_ref, ud_ref, x0_ref, o_ref):
    y = _dot(ft_ref[0], y_ref[0, 0]) + _dot(ft_ref[1], y_ref[0, 1])
    o_ref[0] = ((y + ud_ref[0].astype(F32)) * x0_ref[0].astype(F32)).astype(BF16)


def _hyena_conv(u, ud, x0, tables, coef):
    b, n, c = u.shape
    tf = min(n, DFT_ROWS)
    f_bf, ft_bf = tables
    spec = pl.pallas_call(
        _hyfwd_kernel,
        grid=(n // tf, b),
        in_specs=[pl.BlockSpec((1, n, c), lambda j, bi: (bi, 0, 0)),
                  pl.BlockSpec((2, tf, n), lambda j, bi: (0, j, 0)),
                  pl.BlockSpec((4, tf, c), lambda j, bi: (0, j, 0))],
        out_specs=pl.BlockSpec((1, 2, tf, c), lambda j, bi: (bi, 0, j, 0)),
        out_shape=jax.ShapeDtypeStruct((b, 2, n, c), BF16),
        compiler_params=_cparams(2, 56),
        name="hyena_dft",
    )(u, f_bf, coef)
    return pl.pallas_call(
        _hyinv_kernel,
        grid=(b, n // tf),
        in_specs=[pl.BlockSpec((1, 2, n, c), lambda bi, j: (bi, 0, 0, 0)),
                  pl.BlockSpec((2, tf, n), lambda bi, j: (0, j, 0)),
                  pl.BlockSpec((1, tf, c), lambda bi, j: (bi, j, 0)),
                  pl.BlockSpec((1, tf, c), lambda bi, j: (bi, j, 0))],
        out_specs=pl.BlockSpec((1, tf, c), lambda bi, j: (bi, j, 0)),
        out_shape=jax.ShapeDtypeStruct((b, n, c), BF16),
        compiler_params=_cparams(2, 40),
        name="hyena_idft",
    )(spec, ft_bf, ud, x0)


def _hyena_coef(tables, hsum_hdiff):
    fwd = tables[0]
    n = fwd.shape[1]
    c = HY_WIDTH
    r = _matmul(fwd.reshape(2 * n, n), hsum_hdiff.astype(BF16), F32, 2 * c, "hyena_kernel_dft")
    k_re = r[:n, :c]
    k_im = r[n:, c:]
    k_nyq = r[n:n + 1, :c]
    first = (jnp.arange(n) == 0)[:, None]
    scale = jnp.where(first, 1.0 / (2 * n), 2.0 / (2 * n)).astype(F32)
    zero = jnp.zeros_like(k_im)
    return jnp.stack([k_re * scale,
                      jnp.where(first, zero, k_im * scale),
                      jnp.where(first, zero, k_im * scale),
                      jnp.where(first, k_nyq, k_re) * scale])


def _merge_kernel(*refs, n_ctx_tiles, row_off, has_ctx):
    if has_ctx:
        (of_ref, ob_ref, zg_ref, att_ref, cx_ref, cc_ref, gate_ref, x_ref, m_ref, ghg_ref, gpost_ref,
         gffn_ref, woa_ref, wob_ref, woc_ref, wout_ref, o_ref, h_ref) = refs
    else:
        (of_ref, ob_ref, zg_ref, att_ref, cx_ref, gate_ref, x_ref, m_ref, ghg_ref, gpost_ref,
         gffn_ref, woa_ref, wob_ref, woc_ref, wout_ref, o_ref, h_ref) = refs
    nb, rows, d = x_ref.shape
    flat = lambda ref: ref[...].reshape(nb * rows, ref.shape[-1])
    o = flat(of_ref).astype(F32) + flat(ob_ref).astype(F32)
    ghg = ghg_ref[...]
    a = jnp.concatenate([_rms(o[:, h * HG_DIM:(h + 1) * HG_DIM], ghg) for h in range(HG_HEADS)], axis=1)
    zg = flat(zg_ref)
    a = a * (zg * _sigmoid(zg))
    c = flat(cx_ref)
    if has_ctx:
        c = jnp.where(pl.program_id(1) + row_off < n_ctx_tiles, flat(cc_ref), c)
    ya = _dot(a.astype(BF16), woa_ref[...])
    yb = _dot(flat(att_ref), wob_ref[...])
    yc = _dot(c, woc_ref[...])
    gates = flat(gate_ref)
    m = (_sigmoid(gates[:, 0:d].astype(F32)) * ya
         + _sigmoid(gates[:, d:2 * d].astype(F32)) * yb
         + _sigmoid(gates[:, 2 * d:3 * d].astype(F32)) * yc)
    y = _rms(_dot(m.astype(BF16), wout_ref[...]), gpost_ref[...])
    for b in range(nb):
        x_new = x_ref[b] + m_ref[b, 2:3, :] * y[b * rows:(b + 1) * rows]
        o_ref[b] = x_new
        h_ref[b] = (_rms(x_new, gffn_ref[...]) * (1.0 + m_ref[b, 4:5, :]) + m_ref[b, 3:4, :]).astype(BF16)


def _merge(o_f, o_b, p_hg, att, c_x, c_c, p_rest, xs, mods, g_hg, g_post, g_ffn, w_oa, w_ob, w_oc, w_out,
           row_off, n_ctx_rows):
    b, s, d = xs.shape
    nct = n_ctx_rows // ROW_TILE
    n_tiles = s // ROW_TILE - row_off
    has_ctx = c_c is not None
    nb = MERGE_SAMPLES if b % MERGE_SAMPLES == 0 else 1
    ctx_blk = b // nb

    def stream(width, col=0):
        return pl.BlockSpec((nb, ROW_TILE, width), lambda bi, i: (bi, i + row_off, col))

    def full(shape):
        return pl.BlockSpec(shape, lambda bi, i: (0,) * len(shape))

    in_specs = [stream(HG_WIDTH), stream(HG_WIDTH), stream(HG_WIDTH, 4), stream(AT_WIDTH),
                pl.BlockSpec((nb, ROW_TILE, HY_WIDTH),
                             lambda bi, i: (bi, jnp.maximum(i + row_off - nct, 0), 0))]
    args = [o_f, o_b, p_hg, att, c_x]
    if has_ctx:
        in_specs.append(pl.BlockSpec((nb, ROW_TILE, HY_WIDTH),
                                     lambda bi, i: (bi, jnp.minimum(i + row_off, nct - 1), 0)))
        args.append(c_c)
    in_specs += [stream(3 * d), stream(d),
                 pl.BlockSpec((nb, 6, d), lambda bi, i: (jnp.where(i + row_off < nct, ctx_blk, bi), 0, 0)),
                 full((1, HG_DIM)), full((1, d)), full((1, d)),
                 full((HG_WIDTH, d)), full((AT_WIDTH, d)), full((HY_WIDTH, d)), full((d, d))]
    args += [p_rest, xs, mods, g_hg.reshape(1, HG_DIM), g_post.reshape(1, d), g_ffn.reshape(1, d),
             w_oa, w_ob, w_oc, w_out]
    ospec = pl.BlockSpec((nb, ROW_TILE, d), lambda bi, i: (bi, i, 0))
    return pl.pallas_call(
        functools.partial(_merge_kernel, n_ctx_tiles=nct, row_off=row_off, has_ctx=has_ctx),
        grid=(b // nb, n_tiles),
        in_specs=in_specs,
        out_specs=[ospec, ospec],
        out_shape=[jax.ShapeDtypeStruct((b, n_tiles * ROW_TILE, d), F32),
                   jax.ShapeDtypeStruct((b, n_tiles * ROW_TILE, d), BF16)],
        compiler_params=_cparams(2, 48),
        name="merge",
    )(*args)


FFN_COLS = 256
FFN_DOWN_GROUPS = 2


def _ffn_kernel(*refs, first_tiles, last_tiles, has_next):
    if has_next:
        (h_ref, hp_ref, hn_ref, wu_ref, w_ref, b_ref, x_ref, m_ref, g_ref, wd_ref, mn_ref, gn_ref,
         o_ref, hx_ref, act_ref) = refs
    else:
        h_ref, hp_ref, hn_ref, wu_ref, w_ref, b_ref, x_ref, m_ref, g_ref, wd_ref, o_ref, act_ref = refs
    i = pl.program_id(1)
    is_first = functools.reduce(jnp.logical_or, [i == t for t in first_tiles])
    is_last = functools.reduce(jnp.logical_or, [i == t for t in last_tiles])
    d_ff = wd_ref.shape[0]
    nb, rows, _ = h_ref.shape
    halo = BF16_SUBLANES
    ext = rows + 2 * halo
    pieces = []
    for b in range(nb):
        pieces += [jnp.where(is_first, jnp.zeros_like(hp_ref[b]), hp_ref[b]), h_ref[b],
                   jnp.where(is_last, jnp.zeros_like(hn_ref[b]), hn_ref[b])]
    h_ext = jnp.concatenate(pieces, axis=0)

    def up(j):
        return [_dot(h_ext, wu_ref[:, base + j * FFN_COLS:base + (j + 1) * FFN_COLS]) for base in (0, d_ff)]

    def conv(u, cols):
        w = w_ref[:, cols]
        full = pltpu.roll(u, 1, axis=0) * w[0:1] + u * w[1:2] + pltpu.roll(u, nb * ext - 1, axis=0) * w[2:3]
        kept = [full[b * ext + halo:b * ext + halo + rows] for b in range(nb)]
        return (kept[0] if nb == 1 else jnp.concatenate(kept, axis=0)) + b_ref[:, cols]

    n_chunks = d_ff // FFN_COLS
    per_group = -(-n_chunks // FFN_DOWN_GROUPS)
    acc = None
    u_next = up(0)
    for j in range(n_chunks):
        u = u_next
        if j + 1 < n_chunks:
            u_next = up(j + 1)
        a = conv(u[0], slice(j * FFN_COLS, (j + 1) * FFN_COLS))
        g = conv(u[1], slice(d_ff + j * FFN_COLS, d_ff + (j + 1) * FFN_COLS))
        act_ref[:, j * FFN_COLS:(j + 1) * FFN_COLS] = (a * _sigmoid(a) * g).astype(BF16)
        if (j + 1) % per_group == 0 or j + 1 == n_chunks:
            lo = (j // per_group) * per_group * FFN_COLS
            part = _dot(act_ref[:, lo:(j + 1) * FFN_COLS], wd_ref[lo:(j + 1) * FFN_COLS, :])
            acc = part if acc is None else acc + part
    y = _rms(acc, g_ref[...])
    for b in range(nb):
        x_new = x_ref[b] + m_ref[b, 5:6, :] * y[b * rows:(b + 1) * rows]
        o_ref[b] = x_new
        if has_next:
            hx_ref[b] = (_rms(x_new, gn_ref[...]) * (1.0 + mn_ref[b, 1:2, :])
                         + mn_ref[b, 0:1, :]).astype(BF16)


def _ffn(h, xs, mods, w_up, conv_w, conv_b, g_post, w_down, n_ctx_rows, mods_next=None, g_next=None):
    b, s, d = xs.shape
    d_ff = w_down.shape[0]
    rows = FFN_ROWS if (n_ctx_rows % FFN_ROWS == 0 and s % FFN_ROWS == 0) else ROW_TILE
    nt = s // rows
    nct = n_ctx_rows // rows
    first_tiles = tuple(sorted({0, nct}))
    last_tiles = tuple(sorted({nct - 1, nt - 1} - {-1}))
    nb = FFN_ROWS // rows if b % (FFN_ROWS // rows) == 0 and FFN_ROWS // rows <= MERGE_SAMPLES else 1
    ctx_blk = b // nb
    has_next = mods_next is not None
    full = lambda shape: pl.BlockSpec(shape, lambda bi, i: (0,) * len(shape))
    mspec = pl.BlockSpec((nb, 6, d), lambda bi, i: (jnp.where(i < nct, ctx_blk, bi), 0, 0))
    ospec = pl.BlockSpec((nb, rows, d), lambda bi, i: (bi, i, 0))
    resident = lambda shape: pl.BlockSpec(shape, lambda bi, i: (0,) * len(shape), pipeline_mode=pl.Buffered(1))
    in_specs = _halo_specs(d, 0, 0, s, rows, nb) + [
        resident((d, 2 * d_ff)), full((3, 2 * d_ff)), full((1, 2 * d_ff)),
        ospec, mspec, full((1, d)), resident((d_ff, d))]
    args = [h, h, h, w_up, conv_w, conv_b.reshape(1, 2 * d_ff), xs, mods, g_post.reshape(1, d), w_down]
    out_specs = [ospec]
    out_shape = [jax.ShapeDtypeStruct((b, s, d), F32)]
    if has_next:
        in_specs += [mspec, full((1, d))]
        args += [mods_next, g_next.reshape(1, d)]
        out_specs.append(ospec)
        out_shape.append(jax.ShapeDtypeStruct((b, s, d), BF16))
    return pl.pallas_call(
        functools.partial(_ffn_kernel, first_tiles=first_tiles, last_tiles=last_tiles, has_next=has_next),
        grid=(b // nb, nt),
        in_specs=in_specs,
        out_specs=out_specs,
        out_shape=out_shape,
        scratch_shapes=[pltpu.VMEM((nb * rows, d_ff), BF16)],
        compiler_params=_cparams(2, 56),
        name="ffn",
    )(*args)


def _deinterleave():
    return np.concatenate([np.arange(0, AT_DIM, 2), np.arange(1, AT_DIM, 2)])


def _q_head_order():
    return [h for j in range(AT_GROUP) for h in (j, AT_GROUP + j)]


def _largest_tile(n, cap):
    best = LANE
    for t in range(LANE, cap + 1, LANE):
        if n % t == 0:
            best = t
    return best


def kernel(x, c, ctx, c_ctx, w_ada, b_ada, g_pre_mix, g_post_mix, g_pre_ffn, g_post_ffn, w_in, hg_lower_bounds, hg_norm, q_norm, k_norm, hy_conv_w, hy_conv_b, hy_w1, hy_b1, hy_wi, hy_bi, hy_freq, hy_w_last, hy_bias, w_oa, w_ob, w_oc, w_out, w_up, ffn_conv_w, ffn_conv_b, w_down):
    bsz, n_lat, d = x.shape
    n_ctx = ctx.shape[1]
    depth = w_ada.shape[0]
    d_ff = w_down.shape[1]
    assert AT_KV_HEADS == 2 and AT_GROUP * LANE == AT_WIDTH and AT_KV_WIDTH == LANE
    assert n_ctx % ROW_TILE == 0 and n_lat % ROW_TILE == 0 and n_lat % GRID_W == 0
    assert (bsz * (n_ctx + n_lat)) % MM_ROWS == 0 and (bsz * n_lat) % MM_ROWS == 0

    lbp = jax.nn.softmax(hg_lower_bounds.astype(F32), axis=0)
    lower = jnp.cumsum(lbp, axis=0) - lbp[0]

    rp = -(-(bsz + MERGE_SAMPLES) // 8) * 8
    src = jnp.concatenate([c, jnp.tile(c_ctx[None, :], (MERGE_SAMPLES, 1)),
                           jnp.zeros((rp - bsz - MERGE_SAMPLES, d), F32)], axis=0)
    mods_all = _ada(src, w_ada, b_ada).reshape(depth, rp, 6, d)

    o_q = 5 * HG_WIDTH
    o_k = o_q + AT_WIDTH
    o_v = o_k + AT_KV_WIDTH
    o_hy = o_v + AT_KV_WIDTH
    o_gate = o_hy + 3 * HY_WIDTH
    deint = _deinterleave()
    q_cols = np.concatenate([o_q + h * AT_DIM + deint for h in _q_head_order()])
    k_cols = np.concatenate([o_k + g * AT_DIM + deint for g in range(AT_KV_HEADS)])
    qk_cols = np.concatenate([q_cols, k_cols])
    col_hy = (3 * d) // (3 * HY_WIDTH)
    col_q = (3 * d + 3 * HY_WIDTH) // AT_WIDTH
    col_k = (3 * d + 3 * HY_WIDTH + AT_WIDTH) // AT_KV_WIDTH
    col_v = col_k + 1
    assert (3 * d) % (3 * HY_WIDTH) == 0 and (3 * d + 3 * HY_WIDTH) % AT_WIDTH == 0
    ob_rows = np.concatenate([np.arange(h * AT_DIM, (h + 1) * AT_DIM) for h in _q_head_order()])

    rope_tabs = _rope_tables(n_ctx, n_lat)
    dft_lat = _dft_tables(n_lat)
    dft_ctx = _dft_tables(n_ctx)
    nct = n_ctx // ROW_TILE

    s_all = n_ctx + n_lat
    xs, h = _join_modulate(ctx, x, mods_all[0], g_pre_mix[0], 0, 1)
    for l in range(depth):
        need_ctx = l < depth - 1
        mods = mods_all[l]
        w_hg = w_in[l][:, :5 * HG_WIDTH].astype(BF16)
        w_rest = jnp.concatenate([w_in[l][:, o_gate:o_gate + 3 * d], w_in[l][:, o_hy:o_hy + 3 * HY_WIDTH],
                                  w_in[l][:, qk_cols], w_in[l][:, o_v:o_v + AT_KV_WIDTH]], axis=1).astype(BF16)

        h = h.reshape(bsz * s_all, d)
        p_hg = _matmul(h, w_hg, F32, _largest_tile(5 * HG_WIDTH, 2560), "proj_hgrn").reshape(bsz, s_all, -1)
        p_rest = _matmul(h, w_rest, BF16, _largest_tile(w_rest.shape[1], 1792), "proj_rest").reshape(bsz, s_all, -1)

        o_f, o_b = _hgrn(p_hg, lower[l], n_ctx)

        gq = jnp.tile(q_norm[l][deint], AT_HEADS)[None, :]
        gk = jnp.tile(k_norm[l][deint], AT_KV_HEADS)[None, :]
        row_off = 0 if need_ctx else nct
        att = _attention(p_rest, col_q, col_k, col_v, rope_tabs, gq, gk, row_off, n_ctx)

        filt_args = (hy_w1[l], hy_b1[l], hy_wi[l], hy_bi[l], hy_freq[l], hy_w_last[l])
        coef = _hyena_coef(dft_lat, _hyena_filter_sums(n_lat, *filt_args))
        c_x = _hyena_conv(*_hyena_pre(p_rest, col_hy, nct, n_lat, hy_conv_w[l], hy_conv_b[l], hy_bias[l]),
                          dft_lat, coef)
        c_c = None
        if need_ctx:
            coef_c = _hyena_coef(dft_ctx, _hyena_filter_sums(n_ctx, *filt_args))
            c_c = _hyena_conv(*_hyena_pre(p_rest, col_hy, 0, n_ctx, hy_conv_w[l], hy_conv_b[l], hy_bias[l]),
                              dft_ctx, coef_c)

        xs, h2 = _merge(o_f, o_b, p_hg, att, c_x, c_c, p_rest, xs, mods, hg_norm[l], g_post_mix[l],
                        g_pre_ffn[l], w_oa[l].astype(BF16), w_ob[l][ob_rows].astype(BF16),
                        w_oc[l].astype(BF16), w_out[l].astype(BF16), row_off, n_ctx)
        n_ctx_now = n_ctx if need_ctx else 0
        ffn_args = (h2, xs, mods, w_up[l].astype(BF16), ffn_conv_w[l], ffn_conv_b[l], g_post_ffn[l],
                    w_down[l].astype(BF16), n_ctx_now)
        if need_ctx:
            xs, h = _ffn(*ffn_args, mods_all[l + 1], g_pre_mix[l + 1])
        else:
            xs, = _ffn(*ffn_args)
    return xs
```

```python
import functools
import math

import jax
import jax.numpy as jnp
import numpy as np
from jax import lax
from jax.experimental import pallas as pl
from jax.experimental.pallas import tpu as pltpu

F32 = jnp.float32
BF16 = jnp.bfloat16

NORM_EPS = 1e-6
GRID_W = 64
HG_HEADS = 4
HG_DIM = 128
HG_WIDTH = HG_HEADS * HG_DIM
HG_EXP_CLIP = 30.0
AT_HEADS = 8
AT_KV_HEADS = 2
AT_DIM = 64
AT_GROUP = AT_HEADS // AT_KV_HEADS
AT_WIDTH = AT_HEADS * AT_DIM
AT_KV_WIDTH = AT_KV_HEADS * AT_DIM
ROPE_THETA = 10000.0
HY_WIDTH = 512
HY_EMB_DIM = 33
HY_BANDS = (HY_EMB_DIM - 1) // 2
HY_FILTER_WIDTH = 64
HY_INNER = 2
HY_FAST_DECAY = 0.3
HY_SLOW_DECAY = 1.5
HY_TARGET = 1e-2

LANE = 128
BF16_SUBLANES = 16
ROW_TILE = 256
MERGE_SAMPLES = 2
HG_CHUNK = 128
HG_BLOCK = 256
HG_LEVELS = tuple(HG_CHUNK >> (j + 1) for j in range(int(math.log2(HG_CHUNK))))
MM_ROWS = 1024
DFT_ROWS = 1024
DFT_T_LO = 64
AT_LOOKAHEAD = 2
FFN_ROWS = 512
VMEM_CAP = 56 * 1024 * 1024


def _cparams(n_axes, vmem_mb):
    return pltpu.CompilerParams(
        dimension_semantics=("arbitrary",) * n_axes,
        vmem_limit_bytes=min(int(vmem_mb) * 1024 * 1024, VMEM_CAP))


def _dot(a, b):
    return jnp.dot(a, b, preferred_element_type=F32)


def _dot_nt(a, b):
    return lax.dot_general(a, b, (((1,), (1,)), ((), ())), preferred_element_type=F32)


def _split_bf16(a):
    hi = a.astype(BF16)
    lo = (a - hi.astype(F32)).astype(BF16)
    return hi, lo


def _dot3(a, b):
    ah, al = _split_bf16(a)
    bh, bl = _split_bf16(b)
    return _dot(ah, bh) + (_dot(ah, bl) + _dot(al, bh))


def _rms(x, g):
    return x * lax.rsqrt(jnp.mean(x * x, axis=-1, keepdims=True) + NORM_EPS) * g


def _sigmoid(x):
    return 0.5 * jnp.tanh(0.5 * x) + 0.5


def _ada_kernel(src_ref, w_ref, b_ref, o_ref):
    s = src_ref[...]
    s = s * _sigmoid(s)
    o_ref[0] = _dot3(s, w_ref[0]) + b_ref[0]


def _ada(src, w_ada, b_ada):
    depth, d, d6 = w_ada.shape
    rp = src.shape[0]
    tn = d
    return pl.pallas_call(
        _ada_kernel,
        grid=(depth, d6 // tn),
        in_specs=[pl.BlockSpec((rp, d), lambda l, j: (0, 0)),
                  pl.BlockSpec((1, d, tn), lambda l, j: (l, 0, j)),
                  pl.BlockSpec((1, 1, tn), lambda l, j: (l, 0, j))],
        out_specs=pl.BlockSpec((1, rp, tn), lambda l, j: (l, 0, j)),
        out_shape=jax.ShapeDtypeStruct((depth, rp, d6), F32),
        compiler_params=_cparams(2, 32),
        name="ada",
    )(src, w_ada, b_ada.reshape(depth, 1, d6))


def _mod_kernel(c_ref, x_ref, m_ref, g_ref, xs_ref, o_ref, *, k_shift, k_scale, n_ctx_tiles):
    for b in range(x_ref.shape[0]):
        x = jnp.where(pl.program_id(1) < n_ctx_tiles, c_ref[b], x_ref[b])
        xs_ref[b] = x
        shift = m_ref[b, k_shift:k_shift + 1, :]
        scale = m_ref[b, k_scale:k_scale + 1, :]
        o_ref[b] = (_rms(x, g_ref[...]) * (1.0 + scale) + shift).astype(o_ref.dtype)


def _join_modulate(ctx, x, mods, g, k_shift, k_scale):
    b, n_lat, d = x.shape
    n_ctx_tiles = ctx.shape[1] // ROW_TILE
    nt = n_ctx_tiles + n_lat // ROW_TILE
    nb = MERGE_SAMPLES if b % MERGE_SAMPLES == 0 else 1
    ctx_blk = b // nb
    ospec = pl.BlockSpec((nb, ROW_TILE, d), lambda bi, i: (bi, i, 0))
    return pl.pallas_call(
        functools.partial(_mod_kernel, k_shift=k_shift, k_scale=k_scale, n_ctx_tiles=n_ctx_tiles),
        grid=(b // nb, nt),
        in_specs=[pl.BlockSpec((nb, ROW_TILE, d), lambda bi, i: (bi, jnp.minimum(i, n_ctx_tiles - 1), 0)),
                  pl.BlockSpec((nb, ROW_TILE, d), lambda bi, i: (bi, jnp.maximum(i - n_ctx_tiles, 0), 0)),
                  pl.BlockSpec((nb, 6, d), lambda bi, i: (jnp.where(i < n_ctx_tiles, ctx_blk, bi), 0, 0)),
                  pl.BlockSpec((1, d), lambda bi, i: (0, 0))],
        out_specs=[ospec, ospec],
        out_shape=[jax.ShapeDtypeStruct((b, nt * ROW_TILE, d), F32),
                   jax.ShapeDtypeStruct((b, nt * ROW_TILE, d), BF16)],
        compiler_params=_cparams(2, 16),
        name="modulate",
    )(ctx, x, mods, g.reshape(1, d))


def _mm_kernel(a_ref, b_ref, o_ref):
    o_ref[...] = _dot(a_ref[...], b_ref[...]).astype(o_ref.dtype)


def _matmul(a, w, out_dtype, tn, name):
    m, k = a.shape
    n = w.shape[1]
    tm = min(MM_ROWS, m)
    assert m % tm == 0 and n % tn == 0
    return pl.pallas_call(
        _mm_kernel,
        grid=(m // tm, n // tn),
        in_specs=[pl.BlockSpec((tm, k), lambda i, j: (i, 0)),
                  pl.BlockSpec((k, tn), lambda i, j: (0, j))],
        out_specs=pl.BlockSpec((tm, tn), lambda i, j: (i, j)),
        out_shape=jax.ShapeDtypeStruct((m, n), out_dtype),
        compiler_params=_cparams(2, 56),
        name=name,
    )(a, w)


def _hg_scan_matrix(reverse):
    t_n = HG_CHUNK
    t = np.arange(t_n)[:, None]
    u = np.arange(t_n)[None, :]
    rows = [(u >= t) if reverse else (u <= t)]
    for w in HG_LEVELS:
        base = (t // (2 * w)) * (2 * w)
        mid = base + w
        upper = (t - base) >= w
        if reverse:
            m = np.where(upper, (u >= mid) & (u < t), (u >= t) & (u < mid))
        else:
            m = np.where(upper, (u >= mid) & (u <= t), (u > t) & (u < mid))
        rows.append(m)
    m = np.concatenate(rows, axis=0).astype(np.float32)
    return np.concatenate([m, m], axis=1)


def _hgrn_kernel(qf_ref, zf_ref, vf_ref, qb_ref, zb_ref, vb_ref, lb_ref, pf_ref, pb_ref,
                 of_ref, ob_ref, s_ref):
    @pl.when(pl.program_id(1) == 0)
    def _():
        s_ref[...] = jnp.zeros_like(s_ref)

    t_n = HG_CHUNK
    ti = lax.broadcasted_iota(jnp.int32, (t_n, t_n), 0)
    si = lax.broadcasted_iota(jnp.int32, (t_n, t_n), 1)
    tx = ti ^ si
    dirs = ((qf_ref, zf_ref, vf_ref, pf_ref, of_ref), (qb_ref, zb_ref, vb_ref, pb_ref, ob_ref))
    masks = []
    for reverse in (False, True):
        later = (ti < si) if reverse else (ti > si)
        masks.append([later & (tx >= w) & (tx < 2 * w) for w in HG_LEVELS])
    n_sub = qf_ref.shape[1] // t_n
    heads = range(HG_HEADS)
    units = [(d, c, h) for d in range(2) for c in range(n_sub) for h in heads]
    cols = lambda h: slice(h * HG_DIM, (h + 1) * HG_DIM)
    rows = lambda c: slice(c * t_n, (c + 1) * t_n)

    kk, kb, qb, cat, x, ex = {}, {}, {}, {}, {}, {}
    a = {u: jnp.zeros((t_n, t_n), F32) for u in units}

    def gates(d, c, h):
        z = dirs[d][1][0, rows(c), cols(h)]
        lb = lb_ref[d:d + 1, cols(h)]
        e = jnp.exp(-jnp.abs(z))
        log_num = jnp.log(jnp.where(z >= 0.0, 1.0 + lb * e, e + lb))
        log_clip = z + jnp.log(1.0 + lb * math.exp(HG_EXP_CLIP))
        lf = jnp.where(z < -HG_EXP_CLIP, log_clip, log_num) - jnp.log(1.0 + e)
        half = 0.5 * (1.0 - lb)
        kk[d, c, h] = half - half * jnp.tanh(0.5 * z)
        kb[d, c, h] = kk[d, c, h].astype(BF16)
        qb[d, c, h] = dirs[d][0][0, rows(c), cols(h)].astype(BF16)
        hi, lo = _split_bf16(lf)
        cat[d, c, h] = jnp.concatenate([hi, lo], axis=0)

    def exponents(d):
        rhs = jnp.concatenate([cat[d, c, h] for c in range(n_sub) for h in heads], axis=1)
        xd = _dot(dirs[d][3][...], rhs)
        ed = jnp.exp(xd[t_n:]).astype(BF16)
        for c in range(n_sub):
            for h in heads:
                x[d, c, h] = xd[0:t_n, cols(c * HG_HEADS + h)]
                ex[d, c, h] = ed[:, cols(c * HG_HEADS + h)]

    def level(d, c, j):
        for h in heads:
            u = (d, c, h)
            ew = ex[u][j * t_n:(j + 1) * t_n]
            pw = _dot_nt(qb[u] * ew, kb[u] * ew)
            a[u] = jnp.where(masks[d][j], pw, a[u])

    def finish(d, c, h):
        u = (d, c, h)
        q = dirs[d][0][0, rows(c), cols(h)]
        v = dirs[d][2][0, rows(c), cols(h)]
        g = x[u]
        g_last = g[0:1] if d == 1 else g[t_n - 1:t_n]
        st = s_ref[d * HG_HEADS + h]
        dqk = jnp.sum(q * kk[u], axis=1, keepdims=True)
        o = (_dot(a[u].astype(BF16), v.astype(BF16)) + dqk * v
             + _dot_nt((q * jnp.exp(g)).astype(BF16), st.astype(BF16)))
        kd = (kk[u] * jnp.exp(g_last - g)).astype(BF16)
        dirs[d][4][0, rows(c), cols(h)] = o.astype(BF16)
        s_ref[d * HG_HEADS + h] = st * jnp.exp(g_last) + _dot(v.T.astype(BF16), kd)

    for d in range(2):
        for c in range(n_sub):
            for h in heads:
                gates(d, c, h)
        exponents(d)
        for j in range(len(HG_LEVELS)):
            for c in range(n_sub):
                level(d, c, j)
        for c in (range(n_sub) if d == 0 else reversed(range(n_sub))):
            for h in heads:
                finish(d, c, h)


def _hgrn(p_hg, lb, n_ctx_rows):
    b, s, _ = p_hg.shape
    assert s % HG_BLOCK == 0 and n_ctx_rows % HG_BLOCK == 0
    nb = s // HG_BLOCK
    nc = n_ctx_rows // HG_BLOCK

    def bidx(n):
        return jnp.where(n < nc, nc - 1 - n, nb - 1 - (n - nc))

    blk = (1, HG_BLOCK, HG_WIDTH)
    pf = jnp.asarray(_hg_scan_matrix(False)).astype(BF16)
    pb = jnp.asarray(_hg_scan_matrix(True)).astype(BF16)
    pshape = pf.shape
    return pl.pallas_call(
        _hgrn_kernel,
        grid=(b, nb),
        in_specs=[pl.BlockSpec(blk, lambda bi, n: (bi, n, 0)),
                  pl.BlockSpec(blk, lambda bi, n: (bi, n, 1)),
                  pl.BlockSpec(blk, lambda bi, n: (bi, n, 3)),
                  pl.BlockSpec(blk, lambda bi, n: (bi, bidx(n), 0)),
                  pl.BlockSpec(blk, lambda bi, n: (bi, bidx(n), 2)),
                  pl.BlockSpec(blk, lambda bi, n: (bi, bidx(n), 3)),
                  pl.BlockSpec((2, HG_WIDTH), lambda bi, n: (0, 0)),
                  pl.BlockSpec(pshape, lambda bi, n: (0, 0)),
                  pl.BlockSpec(pshape, lambda bi, n: (0, 0))],
        out_specs=[pl.BlockSpec(blk, lambda bi, n: (bi, n, 0)),
                   pl.BlockSpec(blk, lambda bi, n: (bi, bidx(n), 0))],
        out_shape=[jax.ShapeDtypeStruct((b, s, HG_WIDTH), BF16)] * 2,
        scratch_shapes=[pltpu.VMEM((2 * HG_HEADS, HG_DIM, HG_DIM), F32)],
        compiler_params=_cparams(2, 40),
        name="hgrn",
    )(p_hg, p_hg, p_hg, p_hg, p_hg, p_hg, lb, pf, pb)


def _rope(x, cos, sin_signed, first_half):
    n = x.shape[-1]
    half = AT_DIM // 2
    partner = jnp.where(first_half, pltpu.roll(x, n - half, axis=1), pltpu.roll(x, half, axis=1))
    return x * cos + partner * sin_signed


def _head_norm(x, gain, group_mean):
    ms = _dot((x * x).astype(BF16), group_mean)
    return x * lax.rsqrt(ms + NORM_EPS) * gain


def _attn_kernel(q_ref, k_ref, v_ref, cq_ref, sq_ref, ck_ref, sk_ref, gq_ref, gk_ref, mq_ref, mk_ref,
                 o_ref, k_scr, v_scr, *, q_off, n_ctx_tiles, n_ctx_rows):
    i = pl.program_id(1)

    @pl.when(i == 0)
    def _():
        kr = k_ref[0].astype(F32)
        lane = lax.broadcasted_iota(jnp.int32, kr.shape, 1)
        kn = _head_norm(kr, gk_ref[...], mk_ref[...])
        k_scr[...] = _rope(kn, ck_ref[...], sk_ref[...], (lane % AT_DIM) < AT_DIM // 2).astype(BF16)
        v = v_ref[0]
        one = jnp.ones_like(v)
        v_scr[0] = jnp.where(lane < AT_DIM, v, one)
        v_scr[1] = jnp.where(lane < AT_DIM, one, v)

    qr = q_ref[0].astype(F32)
    lane = lax.broadcasted_iota(jnp.int32, qr.shape, 1)
    qn = _head_norm(qr, gq_ref[...], mq_ref[...])
    qn = _rope(qn, cq_ref[...], sq_ref[...], (lane % AT_DIM) < AT_DIM // 2)
    qn = (qn * (AT_DIM ** -0.5 * math.log2(math.e))).astype(BF16)
    lane_t = lax.broadcasted_iota(jnp.int32, (ROW_TILE, LANE), 1)
    kv0 = lane_t < AT_DIM

    def attend(n_keys):
        keys = k_scr[0:n_keys, :]
        heads = [(j, g) for j in range(AT_WIDTH // LANE) for g in range(AT_KV_HEADS)]

        def scores(j, g):
            qt = qn[:, j * LANE:(j + 1) * LANE]
            return _dot_nt(jnp.where(kv0 if g == 0 else ~kv0, qt, jnp.zeros_like(qt)), keys)

        pending = [scores(*heads[n]) for n in range(AT_LOOKAHEAD)]
        outs = {}
        for n, (j, g) in enumerate(heads):
            s = pending.pop(0)
            if n + AT_LOOKAHEAD < len(heads):
                pending.append(scores(*heads[n + AT_LOOKAHEAD]))
            p = jnp.exp2(s - jnp.max(s, axis=1, keepdims=True))
            outs[g] = _dot(p.astype(BF16), v_scr[g, 0:n_keys, :])
            if g == AT_KV_HEADS - 1:
                num = jnp.where(kv0, outs[0], outs[1])
                den = pltpu.roll(jnp.where(kv0, outs[1], outs[0]), AT_DIM, axis=1)
                o_ref[0, :, j * LANE:(j + 1) * LANE] = (num / den).astype(o_ref.dtype)

    n_all = k_scr.shape[0]
    if q_off < n_ctx_tiles:
        @pl.when(i + q_off < n_ctx_tiles)
        def _():
            attend(n_ctx_rows)

        @pl.when(i + q_off >= n_ctx_tiles)
        def _():
            attend(n_all)
    else:
        attend(n_all)


def _attention(p_rest, col_q, col_k, col_v, tabs, gq, gk, q_off, n_ctx_rows):
    b, s, _ = p_rest.shape
    nt = s // ROW_TILE
    cq, sq, ck, sk = tabs
    mq = jnp.asarray(np.kron(np.eye(AT_HEADS), np.full((AT_DIM, AT_DIM), 1.0 / AT_DIM)), BF16)
    mk = jnp.asarray(np.kron(np.eye(AT_KV_HEADS), np.full((AT_DIM, AT_DIM), 1.0 / AT_DIM)), BF16)
    kern = functools.partial(_attn_kernel, q_off=q_off, n_ctx_tiles=n_ctx_rows // ROW_TILE,
                             n_ctx_rows=n_ctx_rows)
    return pl.pallas_call(
        kern,
        grid=(b, nt - q_off),
        in_specs=[pl.BlockSpec((1, ROW_TILE, AT_WIDTH), lambda bi, i: (bi, i + q_off, col_q)),
                  pl.BlockSpec((1, s, AT_KV_WIDTH), lambda bi, i: (bi, 0, col_k)),
                  pl.BlockSpec((1, s, AT_KV_WIDTH), lambda bi, i: (bi, 0, col_v)),
                  pl.BlockSpec((ROW_TILE, AT_WIDTH), lambda bi, i: (i + q_off, 0)),
                  pl.BlockSpec((ROW_TILE, AT_WIDTH), lambda bi, i: (i + q_off, 0)),
                  pl.BlockSpec((s, AT_KV_WIDTH), lambda bi, i: (0, 0)),
                  pl.BlockSpec((s, AT_KV_WIDTH), lambda bi, i: (0, 0)),
                  pl.BlockSpec((1, AT_WIDTH), lambda bi, i: (0, 0)),
                  pl.BlockSpec((1, AT_KV_WIDTH), lambda bi, i: (0, 0)),
                  pl.BlockSpec((AT_WIDTH, AT_WIDTH), lambda bi, i: (0, 0)),
                  pl.BlockSpec((AT_KV_WIDTH, AT_KV_WIDTH), lambda bi, i: (0, 0))],
        out_specs=pl.BlockSpec((1, ROW_TILE, AT_WIDTH), lambda bi, i: (bi, i + q_off, 0)),
        out_shape=jax.ShapeDtypeStruct((b, s, AT_WIDTH), BF16),
        scratch_shapes=[pltpu.VMEM((s, AT_KV_WIDTH), BF16),
                        pltpu.VMEM((AT_KV_HEADS, s, AT_KV_WIDTH), BF16)],
        compiler_params=_cparams(2, 48),
        name="attention",
    )(p_rest, p_rest, p_rest, cq, sq, ck, sk, gq, gk, mq, mk)


def _rope_tables(n_ctx_rows, n_lat_rows):
    rows = n_lat_rows // GRID_W
    row = jnp.repeat(jnp.arange(rows), GRID_W).astype(F32)
    col = jnp.tile(jnp.arange(GRID_W), rows).astype(F32)
    n_freq = AT_DIM // 4
    inv = ROPE_THETA ** (-jnp.arange(n_freq, dtype=F32) / n_freq)
    ang = jnp.concatenate([row[:, None] * inv, col[:, None] * inv], axis=-1)
    cos = jnp.concatenate([jnp.cos(ang), jnp.cos(ang)], axis=-1)
    sin = jnp.concatenate([-jnp.sin(ang), jnp.sin(ang)], axis=-1)
    cos = jnp.concatenate([jnp.ones((n_ctx_rows, AT_DIM), F32), cos], axis=0)
    sin = jnp.concatenate([jnp.zeros((n_ctx_rows, AT_DIM), F32), sin], axis=0)
    return (jnp.tile(cos, (1, AT_HEADS)), jnp.tile(sin, (1, AT_HEADS)),
            jnp.tile(cos, (1, AT_KV_HEADS)), jnp.tile(sin, (1, AT_KV_HEADS)))


def _shift_matrices(n):
    i = np.arange(n)
    down = i[:, None] - 1 == i[None, :]
    up = i[:, None] + 1 == i[None, :]
    return jnp.asarray(np.stack([down, up]).astype(np.float32), BF16)


def _conv3(xb, shift_ref, prev_row, next_row, w, bias):
    n, c = xb.shape
    sub = 8
    r = lax.broadcasted_iota(jnp.int32, (sub, c), 0)
    x = xb.astype(F32)
    if shift_ref is None:
        xm = pltpu.roll(x, 1, axis=0)
        xp = pltpu.roll(x, n - 1, axis=0)
    else:
        xm = _dot(shift_ref[0], xb)
        xp = _dot(shift_ref[1], xb)
    xm = jnp.concatenate([jnp.where(r == 0, prev_row, xm[0:sub]), xm[sub:]], axis=0)
    xp = jnp.concatenate([xp[:n - sub], jnp.where(r == sub - 1, next_row, xp[n - sub:])], axis=0)
    return xm * w[0:1] + x * w[1:2] + xp * w[2:3] + bias


def _halo_specs(width, col, row_off, n_rows, rows=ROW_TILE, samples=1):
    per = rows // BF16_SUBLANES
    last = n_rows // BF16_SUBLANES - 1
    return [
        pl.BlockSpec((samples, rows, width), lambda bi, i: (bi, i + row_off, col)),
        pl.BlockSpec((samples, BF16_SUBLANES, width),
                     lambda bi, i: (bi, jnp.maximum((i + row_off) * per - 1, 0), col)),
        pl.BlockSpec((samples, BF16_SUBLANES, width),
                     lambda bi, i: (bi, jnp.minimum((i + row_off + 1) * per, last), col)),
    ]


def _halo_rows(prev_ref, next_ref, is_first, is_last, b):
    prev_row = prev_ref[b, BF16_SUBLANES - 1:BF16_SUBLANES, :].astype(F32)
    next_row = next_ref[b, 0:1, :].astype(F32)
    prev_row = jnp.where(is_first, 0.0, prev_row)
    next_row = jnp.where(is_last, 0.0, next_row)
    return prev_row, next_row


HY_PRE_SAMPLES = 4


def _hypre_kernel(z_ref, zp_ref, zn_ref, sh_ref, w_ref, b_ref, db_ref, u_ref, ud_ref, x0_ref):
    i = pl.program_id(1)
    for b in range(z_ref.shape[0]):
        prev_row, next_row = _halo_rows(zp_ref, zn_ref, i == 0, i == pl.num_programs(1) - 1, b)
        zc = _conv3(z_ref[b], sh_ref, prev_row, next_row, w_ref[...], b_ref[...])
        x0 = zc[:, :HY_WIDTH]
        x1 = zc[:, HY_WIDTH:2 * HY_WIDTH]
        v = zc[:, 2 * HY_WIDTH:]
        u = v * x1
        u_ref[b] = u.astype(BF16)
        ud_ref[b] = (u * db_ref[...]).astype(BF16)
        x0_ref[b] = x0.astype(BF16)


def _hyena_pre(p_rest, col, row_off, n_rows, conv_w, conv_b, d_bias):
    b, s, _ = p_rest.shape
    width = 3 * HY_WIDTH
    nb = HY_PRE_SAMPLES if b % HY_PRE_SAMPLES == 0 else 1
    out = jax.ShapeDtypeStruct((b, n_rows, HY_WIDTH), BF16)
    ospec = pl.BlockSpec((nb, ROW_TILE, HY_WIDTH), lambda bi, i: (bi, i, 0))
    return pl.pallas_call(
        _hypre_kernel,
        grid=(b // nb, n_rows // ROW_TILE),
        in_specs=_halo_specs(width, col, row_off, s, samples=nb) + [
            pl.BlockSpec((2, ROW_TILE, ROW_TILE), lambda bi, i: (0, 0, 0)),
            pl.BlockSpec((3, width), lambda bi, i: (0, 0)),
            pl.BlockSpec((1, width), lambda bi, i: (0, 0)),
            pl.BlockSpec((1, HY_WIDTH), lambda bi, i: (0, 0))],
        out_specs=[ospec, ospec, ospec],
        out_shape=[out, out, out],
        compiler_params=_cparams(2, 32),
        name="hyena_pre",
    )(p_rest, p_rest, p_rest, _shift_matrices(ROW_TILE), conv_w, conv_b.reshape(1, width),
      d_bias.reshape(1, HY_WIDTH))


def _hyfilt_kernel(z_ref, t_ref, dl_ref, w1_ref, b1_ref, wi_ref, bi_ref, fr_ref, wl_ref, o_ref):
    fr = fr_ref[...]
    h = jnp.sin(fr * (_dot3(z_ref[...], w1_ref[...]) + b1_ref[...]))
    for j in range(HY_INNER):
        h = jnp.sin(fr * (_dot3(h, wi_ref[j]) + bi_ref[j]))
    h = _dot3(h, wl_ref[...])
    decay = jnp.exp(-t_ref[...] * dl_ref[...])
    hf = h[:, :HY_WIDTH] * decay
    hb = h[:, HY_WIDTH:] * decay
    o_ref[...] = jnp.concatenate([hf + hb, hf - hb], axis=1)


def _pad2(a, rows, cols):
    return jnp.pad(a, ((0, rows - a.shape[0]), (0, cols - a.shape[1])))


def _hyena_filter_sums(n, w1, b1, wi, bi, freq, w_last):
    t = jnp.linspace(0.0, 1.0, n, dtype=F32)[:, None]
    w = 2.0 * math.pi * jnp.arange(n, dtype=F32)[:, None] / n
    f = jnp.linspace(1e-4, HY_BANDS - 1, HY_BANDS, dtype=F32)[None, :]
    z = jnp.concatenate([t, jnp.cos(f * w), -jnp.sin(f * w)], axis=-1)
    max_decay = math.log(HY_TARGET) / HY_FAST_DECAY
    min_decay = math.log(HY_TARGET) / HY_SLOW_DECAY
    deltas = jnp.abs(jnp.linspace(min_decay, max_decay, HY_WIDTH, dtype=F32))[None, :]
    zp = _pad2(z, n, LANE)
    w1p = _pad2(w1, LANE, LANE)
    b1p = _pad2(b1[None, :], 1, LANE)
    wip = jnp.stack([_pad2(wi[j], LANE, LANE) for j in range(HY_INNER)])
    bip = jnp.stack([_pad2(bi[j][None, :], 1, LANE) for j in range(HY_INNER)])
    frp = _pad2(freq[None, :], 1, LANE)
    wlp = _pad2(w_last, LANE, 2 * HY_WIDTH)
    tr = min(n, ROW_TILE)
    full = lambda shape: pl.BlockSpec(shape, lambda i: (0,) * len(shape))
    return pl.pallas_call(
        _hyfilt_kernel,
        grid=(n // tr,),
        in_specs=[pl.BlockSpec((tr, LANE), lambda i: (i, 0)),
                  pl.BlockSpec((tr, 1), lambda i: (i, 0)),
                  full((1, HY_WIDTH)), full((LANE, LANE)), full((1, LANE)),
                  full((HY_INNER, LANE, LANE)), full((HY_INNER, 1, LANE)), full((1, LANE)),
                  full((LANE, 2 * HY_WIDTH))],
        out_specs=pl.BlockSpec((tr, 2 * HY_WIDTH), lambda i: (i, 0)),
        out_shape=jax.ShapeDtypeStruct((n, 2 * HY_WIDTH), F32),
        compiler_params=_cparams(1, 32),
        name="hyena_filter",
    )(zp, t, deltas, w1p, b1p, wip, bip, frp, wlp)


def _dft_tables(n):
    assert n % DFT_T_LO == 0
    f = jnp.arange(n, dtype=jnp.int32)[:, None]
    t_hi = jnp.arange(n // DFT_T_LO, dtype=jnp.int32)[None, :] * DFT_T_LO
    t_lo = jnp.arange(DFT_T_LO, dtype=jnp.int32)[None, :]
    a = ((f * t_hi) % (2 * n)).astype(F32) * (math.pi / n)
    b = ((f * t_lo) % (2 * n)).astype(F32) * (math.pi / n)
    ca, sa = jnp.cos(a)[:, :, None], jnp.sin(a)[:, :, None]
    cb, sb = jnp.cos(b)[:, None, :], jnp.sin(b)[:, None, :]
    cos_t = (ca * cb - sa * sb).reshape(n, n)
    sin_t = (sa * cb + ca * sb).reshape(n, n)
    t = jnp.arange(n, dtype=jnp.int32)[None, :]
    nyq = jnp.where(t % 2 == 0, 1.0, -1.0).astype(F32)
    fwd = jnp.stack([cos_t, jnp.where(f == 0, nyq, sin_t)])
    inv = jnp.stack([cos_t, jnp.where(t == 0, nyq.T, sin_t)])
    return fwd.astype(BF16), inv.astype(BF16)


def _hyfwd_kernel(u_ref, f_ref, co_ref, o_ref):
    u = u_ref[0]
    ure = _dot(f_ref[0], u)
    uim = _dot(f_ref[1], u)
    o_ref[0, 0] = (ure * co_ref[0] - uim * co_ref[1]).astype(BF16)
    o_ref[0, 1] = (ure * co_ref[2] + uim * co_ref[3]).astype(BF16)


def _hyinv_kernel(y_ref, ft_ref, ud_ref, x0_ref, o_ref):
    y = _dot(ft_ref[0], y_ref[0, 0]) + _dot(ft_ref[1], y_ref[0, 1])
    o_ref[0] = ((y + ud_ref[0].astype(F32)) * x0_ref[0].astype(F32)).astype(BF16)


def _hyena_conv(u, ud, x0, tables, coef):
    b, n, c = u.shape
    tf = min(n, DFT_ROWS)
    f_bf, ft_bf = tables
    spec = pl.pallas_call(
        _hyfwd_kernel,
        grid=(n // tf, b),
        in_specs=[pl.BlockSpec((1, n, c), lambda j, bi: (bi, 0, 0)),
                  pl.BlockSpec((2, tf, n), lambda j, bi: (0, j, 0)),
                  pl.BlockSpec((4, tf, c), lambda j, bi: (0, j, 0))],
        out_specs=pl.BlockSpec((1, 2, tf, c), lambda j, bi: (bi, 0, j, 0)),
        out_shape=jax.ShapeDtypeStruct((b, 2, n, c), BF16),
        compiler_params=_cparams(2, 56),
        name="hyena_dft",
    )(u, f_bf, coef)
    return pl.pallas_call(
        _hyinv_kernel,
        grid=(n // tf, b),
        in_specs=[pl.BlockSpec((1, 2, n, c), lambda j, bi: (bi, 0, 0, 0)),
                  pl.BlockSpec((2, tf, n), lambda j, bi: (0, j, 0)),
                  pl.BlockSpec((1, tf, c), lambda j, bi: (bi, j, 0)),
                  pl.BlockSpec((1, tf, c), lambda j, bi: (bi, j, 0))],
        out_specs=pl.BlockSpec((1, tf, c), lambda j, bi: (bi, j, 0)),
        out_shape=jax.ShapeDtypeStruct((b, n, c), BF16),
        compiler_params=_cparams(2, 40),
        name="hyena_idft",
    )(spec, ft_bf, ud, x0)


def _hyena_coef(tables, hsum_hdiff):
    fwd = tables[0]
    n = fwd.shape[1]
    c = HY_WIDTH
    r = _matmul(fwd.reshape(2 * n, n), hsum_hdiff.astype(BF16), F32, 2 * c, "hyena_kernel_dft")
    k_re = r[:n, :c]
    k_im = r[n:, c:]
    k_nyq = r[n:n + 1, :c]
    first = (jnp.arange(n) == 0)[:, None]
    scale = jnp.where(first, 1.0 / (2 * n), 2.0 / (2 * n)).astype(F32)
    zero = jnp.zeros_like(k_im)
    return jnp.stack([k_re * scale,
                      jnp.where(first, zero, k_im * scale),
                      jnp.where(first, zero, k_im * scale),
                      jnp.where(first, k_nyq, k_re) * scale])


def _merge_kernel(*refs, n_ctx_tiles, row_off, has_ctx):
    if has_ctx:
        (of_ref, ob_ref, zg_ref, att_ref, cx_ref, cc_ref, gate_ref, x_ref, m_ref, ghg_ref, gpost_ref,
         gffn_ref, woa_ref, wob_ref, woc_ref, wout_ref, o_ref, h_ref) = refs
    else:
        (of_ref, ob_ref, zg_ref, att_ref, cx_ref, gate_ref, x_ref, m_ref, ghg_ref, gpost_ref,
         gffn_ref, woa_ref, wob_ref, woc_ref, wout_ref, o_ref, h_ref) = refs
    nb, rows, d = x_ref.shape
    flat = lambda ref: ref[...].reshape(nb * rows, ref.shape[-1])
    o = flat(of_ref).astype(F32) + flat(ob_ref).astype(F32)
    ghg = ghg_ref[...]
    a = jnp.concatenate([_rms(o[:, h * HG_DIM:(h + 1) * HG_DIM], ghg) for h in range(HG_HEADS)], axis=1)
    zg = flat(zg_ref)
    a = a * (zg * _sigmoid(zg))
    c = flat(cx_ref)
    if has_ctx:
        c = jnp.where(pl.program_id(1) + row_off < n_ctx_tiles, flat(cc_ref), c)
    ya = _dot(a.astype(BF16), woa_ref[...])
    yb = _dot(flat(att_ref), wob_ref[...])
    yc = _dot(c, woc_ref[...])
    gates = flat(gate_ref)
    m = (_sigmoid(gates[:, 0:d].astype(F32)) * ya
         + _sigmoid(gates[:, d:2 * d].astype(F32)) * yb
         + _sigmoid(gates[:, 2 * d:3 * d].astype(F32)) * yc)
    y = _rms(_dot(m.astype(BF16), wout_ref[...]), gpost_ref[...])
    for b in range(nb):
        x_new = x_ref[b] + m_ref[b, 2:3, :] * y[b * rows:(b + 1) * rows]
        o_ref[b] = x_new
        h_ref[b] = (_rms(x_new, gffn_ref[...]) * (1.0 + m_ref[b, 4:5, :]) + m_ref[b, 3:4, :]).astype(BF16)


def _merge(o_f, o_b, p_hg, att, c_x, c_c, p_rest, xs, mods, g_hg, g_post, g_ffn, w_oa, w_ob, w_oc, w_out,
           row_off, n_ctx_rows):
    b, s, d = xs.shape
    nct = n_ctx_rows // ROW_TILE
    n_tiles = s // ROW_TILE - row_off
    has_ctx = c_c is not None
    nb = MERGE_SAMPLES if b % MERGE_SAMPLES == 0 else 1
    ctx_blk = b // nb

    def stream(width, col=0):
        return pl.BlockSpec((nb, ROW_TILE, width), lambda bi, i: (bi, i + row_off, col))

    def full(shape):
        return pl.BlockSpec(shape, lambda bi, i: (0,) * len(shape))

    in_specs = [stream(HG_WIDTH), stream(HG_WIDTH), stream(HG_WIDTH, 4), stream(AT_WIDTH),
                pl.BlockSpec((nb, ROW_TILE, HY_WIDTH),
                             lambda bi, i: (bi, jnp.maximum(i + row_off - nct, 0), 0))]
    args = [o_f, o_b, p_hg, att, c_x]
    if has_ctx:
        in_specs.append(pl.BlockSpec((nb, ROW_TILE, HY_WIDTH),
                                     lambda bi, i: (bi, jnp.minimum(i + row_off, nct - 1), 0)))
        args.append(c_c)
    in_specs += [stream(3 * d), stream(d),
                 pl.BlockSpec((nb, 6, d), lambda bi, i: (jnp.where(i + row_off < nct, ctx_blk, bi), 0, 0)),
                 full((1, HG_DIM)), full((1, d)), full((1, d)),
                 full((HG_WIDTH, d)), full((AT_WIDTH, d)), full((HY_WIDTH, d)), full((d, d))]
    args += [p_rest, xs, mods, g_hg.reshape(1, HG_DIM), g_post.reshape(1, d), g_ffn.reshape(1, d),
             w_oa, w_ob, w_oc, w_out]
    ospec = pl.BlockSpec((nb, ROW_TILE, d), lambda bi, i: (bi, i, 0))
    return pl.pallas_call(
        functools.partial(_merge_kernel, n_ctx_tiles=nct, row_off=row_off, has_ctx=has_ctx),
        grid=(b // nb, n_tiles),
        in_specs=in_specs,
        out_specs=[ospec, ospec],
        out_shape=[jax.ShapeDtypeStruct((b, n_tiles * ROW_TILE, d), F32),
                   jax.ShapeDtypeStruct((b, n_tiles * ROW_TILE, d), BF16)],
        compiler_params=_cparams(2, 48),
        name="merge",
    )(*args)


FFN_COLS = 256
FFN_DOWN_GROUPS = 2


def _ffn_kernel(*refs, first_tiles, last_tiles, has_next):
    if has_next:
        (h_ref, hp_ref, hn_ref, wu_ref, w_ref, b_ref, x_ref, m_ref, g_ref, wd_ref, mn_ref, gn_ref,
         o_ref, hx_ref, act_ref) = refs
    else:
        h_ref, hp_ref, hn_ref, wu_ref, w_ref, b_ref, x_ref, m_ref, g_ref, wd_ref, o_ref, act_ref = refs
    i = pl.program_id(1)
    is_first = functools.reduce(jnp.logical_or, [i == t for t in first_tiles])
    is_last = functools.reduce(jnp.logical_or, [i == t for t in last_tiles])
    d_ff = wd_ref.shape[0]
    nb, rows, _ = h_ref.shape
    halo = BF16_SUBLANES
    ext = rows + 2 * halo
    pieces = []
    for b in range(nb):
        pieces += [jnp.where(is_first, jnp.zeros_like(hp_ref[b]), hp_ref[b]), h_ref[b],
                   jnp.where(is_last, jnp.zeros_like(hn_ref[b]), hn_ref[b])]
    h_ext = jnp.concatenate(pieces, axis=0)

    def up(j):
        return [_dot(h_ext, wu_ref[:, base + j * FFN_COLS:base + (j + 1) * FFN_COLS]) for base in (0, d_ff)]

    def conv(u, cols):
        w = w_ref[:, cols]
        full = pltpu.roll(u, 1, axis=0) * w[0:1] + u * w[1:2] + pltpu.roll(u, nb * ext - 1, axis=0) * w[2:3]
        kept = [full[b * ext + halo:b * ext + halo + rows] for b in range(nb)]
        return (kept[0] if nb == 1 else jnp.concatenate(kept, axis=0)) + b_ref[:, cols]

    n_chunks = d_ff // FFN_COLS
    per_group = -(-n_chunks // FFN_DOWN_GROUPS)
    acc = None
    u_next = up(0)
    for j in range(n_chunks):
        u = u_next
        if j + 1 < n_chunks:
            u_next = up(j + 1)
        a = conv(u[0], slice(j * FFN_COLS, (j + 1) * FFN_COLS))
        g = conv(u[1], slice(d_ff + j * FFN_COLS, d_ff + (j + 1) * FFN_COLS))
        act_ref[:, j * FFN_COLS:(j + 1) * FFN_COLS] = (a * _sigmoid(a) * g).astype(BF16)
        if (j + 1) % per_group == 0 or j + 1 == n_chunks:
            lo = (j // per_group) * per_group * FFN_COLS
            part = _dot(act_ref[:, lo:(j + 1) * FFN_COLS], wd_ref[lo:(j + 1) * FFN_COLS, :])
            acc = part if acc is None else acc + part
    y = _rms(acc, g_ref[...])
    for b in range(nb):
        x_new = x_ref[b] + m_ref[b, 5:6, :] * y[b * rows:(b + 1) * rows]
        o_ref[b] = x_new
        if has_next:
            hx_ref[b] = (_rms(x_new, gn_ref[...]) * (1.0 + mn_ref[b, 1:2, :])
                         + mn_ref[b, 0:1, :]).astype(BF16)


def _ffn(h, xs, mods, w_up, conv_w, conv_b, g_post, w_down, n_ctx_rows, mods_next=None, g_next=None):
    b, s, d = xs.shape
    d_ff = w_down.shape[0]
    rows = FFN_ROWS if (n_ctx_rows % FFN_ROWS == 0 and s % FFN_ROWS == 0) else ROW_TILE
    nt = s // rows
    nct = n_ctx_rows // rows
    first_tiles = tuple(sorted({0, nct}))
    last_tiles = tuple(sorted({nct - 1, nt - 1} - {-1}))
    nb = FFN_ROWS // rows if b % (FFN_ROWS // rows) == 0 and FFN_ROWS // rows <= MERGE_SAMPLES else 1
    ctx_blk = b // nb
    has_next = mods_next is not None
    full = lambda shape: pl.BlockSpec(shape, lambda bi, i: (0,) * len(shape))
    mspec = pl.BlockSpec((nb, 6, d), lambda bi, i: (jnp.where(i < nct, ctx_blk, bi), 0, 0))
    ospec = pl.BlockSpec((nb, rows, d), lambda bi, i: (bi, i, 0))
    resident = lambda shape: pl.BlockSpec(shape, lambda bi, i: (0,) * len(shape), pipeline_mode=pl.Buffered(1))
    in_specs = _halo_specs(d, 0, 0, s, rows, nb) + [
        resident((d, 2 * d_ff)), full((3, 2 * d_ff)), full((1, 2 * d_ff)),
        ospec, mspec, full((1, d)), resident((d_ff, d))]
    args = [h, h, h, w_up, conv_w, conv_b.reshape(1, 2 * d_ff), xs, mods, g_post.reshape(1, d), w_down]
    out_specs = [ospec]
    out_shape = [jax.ShapeDtypeStruct((b, s, d), F32)]
    if has_next:
        in_specs += [mspec, full((1, d))]
        args += [mods_next, g_next.reshape(1, d)]
        out_specs.append(ospec)
        out_shape.append(jax.ShapeDtypeStruct((b, s, d), BF16))
    return pl.pallas_call(
        functools.partial(_ffn_kernel, first_tiles=first_tiles, last_tiles=last_tiles, has_next=has_next),
        grid=(b // nb, nt),
        in_specs=in_specs,
        out_specs=out_specs,
        out_shape=out_shape,
        scratch_shapes=[pltpu.VMEM((nb * rows, d_ff), BF16)],
        compiler_params=_cparams(2, 56),
        name="ffn",
    )(*args)


def _deinterleave():
    return np.concatenate([np.arange(0, AT_DIM, 2), np.arange(1, AT_DIM, 2)])


def _q_head_order():
    return [h for j in range(AT_GROUP) for h in (j, AT_GROUP + j)]


def _largest_tile(n, cap):
    best = LANE
    for t in range(LANE, cap + 1, LANE):
        if n % t == 0:
            best = t
    return best


def kernel(x, c, ctx, c_ctx, w_ada, b_ada, g_pre_mix, g_post_mix, g_pre_ffn, g_post_ffn, w_in, hg_lower_bounds, hg_norm, q_norm, k_norm, hy_conv_w, hy_conv_b, hy_w1, hy_b1, hy_wi, hy_bi, hy_freq, hy_w_last, hy_bias, w_oa, w_ob, w_oc, w_out, w_up, ffn_conv_w, ffn_conv_b, w_down):
    bsz, n_lat, d = x.shape
    n_ctx = ctx.shape[1]
    depth = w_ada.shape[0]
    d_ff = w_down.shape[1]
    assert AT_KV_HEADS == 2 and AT_GROUP * LANE == AT_WIDTH and AT_KV_WIDTH == LANE
    assert n_ctx % ROW_TILE == 0 and n_lat % ROW_TILE == 0 and n_lat % GRID_W == 0
    assert (bsz * (n_ctx + n_lat)) % MM_ROWS == 0 and (bsz * n_lat) % MM_ROWS == 0

    lbp = jax.nn.softmax(hg_lower_bounds.astype(F32), axis=0)
    lower = jnp.cumsum(lbp, axis=0) - lbp[0]

    rp = -(-(bsz + MERGE_SAMPLES) // 8) * 8
    src = jnp.concatenate([c, jnp.tile(c_ctx[None, :], (MERGE_SAMPLES, 1)),
                           jnp.zeros((rp - bsz - MERGE_SAMPLES, d), F32)], axis=0)
    mods_all = _ada(src, w_ada, b_ada).reshape(depth, rp, 6, d)

    o_q = 5 * HG_WIDTH
    o_k = o_q + AT_WIDTH
    o_v = o_k + AT_KV_WIDTH
    o_hy = o_v + AT_KV_WIDTH
    o_gate = o_hy + 3 * HY_WIDTH
    deint = _deinterleave()
    q_cols = np.concatenate([o_q + h * AT_DIM + deint for h in _q_head_order()])
    k_cols = np.concatenate([o_k + g * AT_DIM + deint for g in range(AT_KV_HEADS)])
    qk_cols = np.concatenate([q_cols, k_cols])
    col_hy = (3 * d) // (3 * HY_WIDTH)
    col_q = (3 * d + 3 * HY_WIDTH) // AT_WIDTH
    col_k = (3 * d + 3 * HY_WIDTH + AT_WIDTH) // AT_KV_WIDTH
    col_v = col_k + 1
    assert (3 * d) % (3 * HY_WIDTH) == 0 and (3 * d + 3 * HY_WIDTH) % AT_WIDTH == 0
    ob_rows = np.concatenate([np.arange(h * AT_DIM, (h + 1) * AT_DIM) for h in _q_head_order()])

    rope_tabs = _rope_tables(n_ctx, n_lat)
    dft_lat = _dft_tables(n_lat)
    dft_ctx = _dft_tables(n_ctx)
    nct = n_ctx // ROW_TILE

    s_all = n_ctx + n_lat
    xs, h = _join_modulate(ctx, x, mods_all[0], g_pre_mix[0], 0, 1)
    for l in range(depth):
        need_ctx = l < depth - 1
        mods = mods_all[l]
        w_hg = w_in[l][:, :5 * HG_WIDTH].astype(BF16)
        w_rest = jnp.concatenate([w_in[l][:, o_gate:o_gate + 3 * d], w_in[l][:, o_hy:o_hy + 3 * HY_WIDTH],
                                  w_in[l][:, qk_cols], w_in[l][:, o_v:o_v + AT_KV_WIDTH]], axis=1).astype(BF16)

        h = h.reshape(bsz * s_all, d)
        p_hg = _matmul(h, w_hg, F32, _largest_tile(5 * HG_WIDTH, 2560), "proj_hgrn").reshape(bsz, s_all, -1)
        p_rest = _matmul(h, w_rest, BF16, _largest_tile(w_rest.shape[1], 1792), "proj_rest").reshape(bsz, s_all, -1)

        o_f, o_b = _hgrn(p_hg, lower[l], n_ctx)

        gq = jnp.tile(q_norm[l][deint], AT_HEADS)[None, :]
        gk = jnp.tile(k_norm[l][deint], AT_KV_HEADS)[None, :]
        row_off = 0 if need_ctx else nct
        att = _attention(p_rest, col_q, col_k, col_v, rope_tabs, gq, gk, row_off, n_ctx)

        filt_args = (hy_w1[l], hy_b1[l], hy_wi[l], hy_bi[l], hy_freq[l], hy_w_last[l])
        coef = _hyena_coef(dft_lat, _hyena_filter_sums(n_lat, *filt_args))
        c_x = _hyena_conv(*_hyena_pre(p_rest, col_hy, nct, n_lat, hy_conv_w[l], hy_conv_b[l], hy_bias[l]),
                          dft_lat, coef)
        c_c = None
        if need_ctx:
            coef_c = _hyena_coef(dft_ctx, _hyena_filter_sums(n_ctx, *filt_args))
            c_c = _hyena_conv(*_hyena_pre(p_rest, col_hy, 0, n_ctx, hy_conv_w[l], hy_conv_b[l], hy_bias[l]),
                              dft_ctx, coef_c)

        xs, h2 = _merge(o_f, o_b, p_hg, att, c_x, c_c, p_rest, xs, mods, hg_norm[l], g_post_mix[l],
                        g_pre_ffn[l], w_oa[l].astype(BF16), w_ob[l][ob_rows].astype(BF16),
                        w_oc[l].astype(BF16), w_out[l].astype(BF16), row_off, n_ctx)
        n_ctx_now = n_ctx if need_ctx else 0
        ffn_args = (h2, xs, mods, w_up[l].astype(BF16), ffn_conv_w[l], ffn_conv_b[l], g_post_ffn[l],
                    w_down[l].astype(BF16), n_ctx_now)
        if need_ctx:
            xs, h = _ffn(*ffn_args, mods_all[l + 1], g_pre_mix[l + 1])
        else:
            xs, = _ffn(*ffn_args)
    return xs
```
